```python
import jax
import jax.numpy as jnp
from jax import lax
import numpy as np

D_MODEL = 1024
BATCH = 4
SEQ = 4096
DEPTH = 2

CTX_LEN = 256
GRID_W = 64
EPS = 1e-6
N_BRANCH = 3
N_MOD = 6

A_GROUPS = 8
A_GROUP_DIM = 128
A_WIDTH = A_GROUPS * A_GROUP_DIM
A_CHUNK = 128

B_HEADS = 4
B_DK = 128
B_DV = 256
B_GATE_RANK = 16
B_GATE_NORM = 16.0
B_CHUNK = 64

C_HEADS = 8
C_Q_RANK = 384
C_KV_RANK = 256
C_NOPE = 128
C_ROPE = 64
C_DV = 128
C_QBLOCK = 128
ROPE_BASE = 10000.0

N_EXPERTS = 16
CAPACITY_FACTOR = 2
D_EXPERT = 2048

IN_SPLITS = (A_WIDTH, A_WIDTH,
             B_HEADS * B_DK, B_HEADS * B_DK, B_HEADS * B_DV, B_HEADS * B_DV, 2 * B_GATE_RANK,
             C_Q_RANK, C_KV_RANK, C_ROPE,
             N_BRANCH * D_MODEL)
N_IN = sum(IN_SPLITS)

kernel_name = 'hybrid_gated_branch_dit_block'


def rms_norm(x, g):
    xf = x.astype(jnp.float32)
    y = xf * lax.rsqrt(jnp.mean(jnp.square(xf), axis=-1, keepdims=True) + EPS)
    return (y * g.astype(jnp.float32)).astype(x.dtype)


def layer_norm(x):
    xf = x.astype(jnp.float32)
    mu = jnp.mean(xf, axis=-1, keepdims=True)
    var = jnp.mean(jnp.square(xf - mu), axis=-1, keepdims=True)
    return ((xf - mu) * lax.rsqrt(var + EPS)).astype(x.dtype)


def modulate(x, g, shift, scale):
    return rms_norm(x, g) * (1 + scale) + shift


def split_cols(z):
    idx = np.cumsum(IN_SPLITS)[:-1].tolist()
    return jnp.split(z, idx, axis=-1)


def axial_rope_tables(n_tok):
    rows = n_tok // GRID_W
    row = jnp.broadcast_to(jnp.arange(rows, dtype=jnp.float32)[:, None], (rows, GRID_W)).reshape(-1)
    col = jnp.broadcast_to(jnp.arange(GRID_W, dtype=jnp.float32)[None, :], (rows, GRID_W)).reshape(-1)
    axis_dim = C_ROPE // 2
    inv_freq = ROPE_BASE ** (-jnp.arange(0, axis_dim, 2, dtype=jnp.float32) / axis_dim)
    ang = jnp.concatenate([row[:, None] * inv_freq, col[:, None] * inv_freq], axis=-1)
    return jnp.cos(ang), jnp.sin(ang)


def apply_axial_rope(x, cos, sin):
    n_freq = C_ROPE // 4
    xf = x.astype(jnp.float32).reshape(*x.shape[:-1], 2, 2, n_freq)
    x1, x2 = xf[..., 0, :], xf[..., 1, :]
    shape = (cos.shape[0],) + (1,) * (x.ndim - 3) + (2, n_freq)
    cs, sn = cos.reshape(shape), sin.reshape(shape)
    out = jnp.stack([x1 * cs - x2 * sn, x2 * cs + x1 * sn], axis=-2)
    return out.reshape(x.shape).astype(x.dtype)


def chunk_mlp(u, v, p):
    bsz, n_tok, _ = v.shape
    n_chunk = n_tok // A_CHUNK
    v = layer_norm(v.reshape(bsz, n_tok, A_GROUPS, A_GROUP_DIM))
    v = v.reshape(bsz, n_chunk, A_CHUNK, A_GROUPS, A_GROUP_DIM)
    mixed = jnp.einsum('gts,bnsgc->bntgc', p['w_s'], v) + p['b_s'].T[:, :, None]
    return u * mixed.reshape(bsz, n_tok, A_WIDTH)


def to_chunks(t):
    bsz, n_tok = t.shape[:2]
    return t.reshape(bsz, n_tok // B_CHUNK, B_CHUNK, *t.shape[2:])


def gla_inputs(q, k, v, dec, p):
    bsz, n_tok, _ = q.shape

    def heads(t, d):
        return t.astype(jnp.float32).reshape(bsz, n_tok, B_HEADS, d)

    dec_f, dec_b = jnp.split(dec, 2, axis=-1)

    def log_decay(z, j):
        pre = (z @ p['w_gla_decay'][j] + p['b_gla_decay'][j]).astype(jnp.float32)
        return jax.nn.log_sigmoid(pre) / B_GATE_NORM

    return (heads(q, B_DK) * B_DK ** -0.5, heads(k, B_DK), heads(v, B_DV),
            heads(log_decay(dec_f, 0), B_DK), heads(log_decay(dec_b, 1), B_DK))


def gla_state_scan(k, v, log_a, s0):
    k, v, log_a = to_chunks(k), to_chunks(v), to_chunks(log_a)
    b = jnp.cumsum(log_a, axis=2)
    b_last = b[:, :, -1]
    kv = jnp.einsum('bnjhd,bnjhv->bnhdv', k * jnp.exp(b_last[:, :, None] - b), v)

    def step(s, inp):
        decay, kv_n = inp
        return decay[..., None] * s + kv_n, s

    s_final, s_start = lax.scan(step, s0, (jnp.moveaxis(jnp.exp(b_last), 1, 0), jnp.moveaxis(kv, 1, 0)))
    return b, jnp.moveaxis(s_start, 0, 1), s_final


def gla_chunk_output(q, k, v, b, s_start):
    bsz, n_tok = q.shape[:2]
    q, k, v = to_chunks(q), to_chunks(k), to_chunks(v)
    q_dec = q * jnp.exp(b)
    k_inv = k * jnp.exp(-b)
    in_order = jnp.tril(jnp.ones((B_CHUNK, B_CHUNK), dtype=bool))
    scores = jnp.where(in_order, jnp.einsum('bnihd,bnjhd->bnhij', q_dec, k_inv), 0.0)
    o = (jnp.einsum('bnhij,bnjhv->bnihv', scores, v)
         + jnp.einsum('bnihd,bnhdv->bnihv', q_dec, s_start))
    return o.reshape(bsz, n_tok, B_HEADS, B_DV)


def gla_bidirectional(q, k, v, la_f, la_b, s0_f, s0_b, with_output):
    def flip(t):
        return jnp.flip(t, axis=1)

    b_f, st_f, sf_f = gla_state_scan(k, v, la_f, s0_f)
    b_b, st_b, sf_b = gla_state_scan(flip(k), flip(v), flip(la_b), s0_b)
    if not with_output:
        return None, sf_f, sf_b
    o = gla_chunk_output(q, k, v, b_f, st_f) + flip(gla_chunk_output(flip(q), flip(k), flip(v), b_b, st_b))
    return o, sf_f, sf_b


def gla_finish(o, r, p):
    bsz, n_tok, _ = r.shape
    o = rms_norm(o, p['g_gla_out']).astype(r.dtype)
    return o.reshape(bsz, n_tok, B_HEADS * B_DV) * jax.nn.silu(r)


def mla_keys(c_kv, k_rope, p, rope):
    bsz, n_tok, _ = c_kv.shape
    kv = (rms_norm(c_kv, p['g_ckv']) @ p['w_ukv']).reshape(bsz, n_tok, C_HEADS, C_NOPE + C_DV)
    k_nope = rms_norm(kv[..., :C_NOPE], p['g_kn'])
    k_rope = rms_norm(k_rope, p['g_kr'])
    if rope is not None:
        k_rope = apply_axial_rope(k_rope, *rope)
    return k_nope, k_rope, kv[..., C_NOPE:]


def mla_queries(c_q, p, rope):
    bsz, n_tok, _ = c_q.shape
    q = (rms_norm(c_q, p['g_cq']) @ p['w_uq']).reshape(bsz, n_tok, C_HEADS, C_NOPE + C_ROPE)
    q_nope = rms_norm(q[..., :C_NOPE], p['g_qn'])
    q_rope = rms_norm(q[..., C_NOPE:], p['g_qr'])
    if rope is not None:
        q_rope = apply_axial_rope(q_rope, *rope)
    return q_nope, q_rope


def mla_attend(q_nope, q_rope, k_nope, k_rope, v):
    scale = (C_NOPE + C_ROPE) ** -0.5
    s = (jnp.einsum('bqhd,bkhd->bhqk', q_nope, k_nope)
         + jnp.einsum('bqhr,bkr->bhqk', q_rope, k_rope)).astype(jnp.float32) * scale
    prob = jax.nn.softmax(s, axis=-1).astype(v.dtype)
    return jnp.einsum('bhqk,bkhd->bqhd', prob, v)


def mla_attend_blocked(q_nope, q_rope, k_nope, k_rope, v):
    bsz, n_tok = q_nope.shape[:2]
    nb = n_tok // C_QBLOCK

    def blocks(t):
        return jnp.moveaxis(t.reshape(bsz, nb, C_QBLOCK, *t.shape[2:]), 1, 0)

    o = lax.map(lambda qs: mla_attend(qs[0], qs[1], k_nope, k_rope, v), (blocks(q_nope), blocks(q_rope)))
    return jnp.moveaxis(o, 0, 1).reshape(bsz, n_tok, C_HEADS * C_DV)


def merge_branches(gate_cols, ya, yb, yc, p):
    lead = gate_cols.shape[:-1]
    gates = jax.nn.sigmoid((gate_cols + p['b_gate']).astype(jnp.float32)).astype(ya.dtype)
    gates = gates.reshape(*lead, N_BRANCH, D_MODEL)
    merged = (gates[..., 0, :] * (ya @ p['w_proj_a'])
              + gates[..., 1, :] * (yb @ p['w_proj_b'])
              + gates[..., 2, :] * (yc @ p['w_proj_c']))
    return merged @ p['w_out']


def token_mixer(h_ctx, h_lat, rope, p, need_ctx_out):
    bsz, n_ctx, _ = h_ctx.shape
    u_c, va_c, qb_c, kb_c, vb_c, rb_c, dec_c, cq_c, ckv_c, krope_c, gt_c = split_cols(h_ctx @ p['w_in'])
    u_l, va_l, qb_l, kb_l, vb_l, rb_l, dec_l, cq_l, ckv_l, krope_l, gt_l = split_cols(h_lat @ p['w_in'])

    zero_state = jnp.zeros((bsz, B_HEADS, B_DK, B_DV), jnp.float32)
    o_c, s_fwd, s_bwd = gla_bidirectional(*gla_inputs(qb_c, kb_c, vb_c, dec_c, p), zero_state, zero_state, need_ctx_out)
    o_l, _, _ = gla_bidirectional(*gla_inputs(qb_l, kb_l, vb_l, dec_l, p), s_fwd, s_bwd, True)

    kn_c, kr_c, v_c = mla_keys(ckv_c, krope_c, p, None)
    kn_l, kr_l, v_l = mla_keys(ckv_l, krope_l, p, rope)
    qn_l, qr_l = mla_queries(cq_l, p, rope)
    yc_l = mla_attend_blocked(qn_l, qr_l,
                              jnp.concatenate([kn_c, kn_l], axis=1),
                              jnp.concatenate([kr_c, kr_l], axis=1),
                              jnp.concatenate([v_c, v_l], axis=1))
    y_lat = merge_branches(gt_l, chunk_mlp(u_l, va_l, p), gla_finish(o_l, rb_l, p), yc_l, p)
    if not need_ctx_out:
        return None, y_lat
    qn_c, qr_c = mla_queries(cq_c, p, None)
    yc_c = mla_attend(qn_c, qr_c, kn_c, kr_c, v_c).reshape(bsz, n_ctx, C_HEADS * C_DV)
    y_ctx = merge_branches(gt_c, chunk_mlp(u_c, va_c, p), gla_finish(o_c, rb_c, p), yc_c, p)
    return y_ctx, y_lat


def expert_choice_ffn(h, p):
    bsz, n_tok, d = h.shape
    cap = CAPACITY_FACTOR * n_tok // N_EXPERTS
    aff = jax.nn.softmax((h @ p['w_router']).astype(jnp.float32), axis=-1)
    g, idx = lax.top_k(jnp.swapaxes(aff, 1, 2), cap)
    xs = jax.vmap(lambda hb, ib: hb[ib])(h, idx)
    hid = (jax.nn.silu(jnp.einsum('becd,edf->becf', xs, p['w_e_gate']))
           * jnp.einsum('becd,edf->becf', xs, p['w_e_up']))
    ys = jnp.einsum('becf,efd->becd', hid, p['w_e_down']) * g[..., None].astype(h.dtype)

    def scatter(yb, ib):
        return jnp.zeros((n_tok, d), yb.dtype).at[ib.reshape(-1)].add(yb.reshape(-1, d))

    return jax.vmap(scatter)(ys, idx)


def setup_inputs(seed: int = 0) -> dict:
    key = jax.random.key(seed)
    ks = iter(jax.random.split(key, 40))

    def nrm(shape, scale):
        return jax.random.normal(next(ks), shape, jnp.float32) * scale

    def gain(shape):
        return 1.0 + nrm(shape, 0.02)

    L = DEPTH
    return {
        'x': nrm((BATCH, SEQ, D_MODEL), 1.0),
        'c': nrm((BATCH, D_MODEL), 1.0),
        'ctx': nrm((BATCH, CTX_LEN, D_MODEL), 1.0),
        'c_ctx': nrm((D_MODEL,), 1.0),
        'w_mod': nrm((L, D_MODEL, N_MOD * D_MODEL), 0.5 * D_MODEL ** -0.5),
        'b_mod': nrm((L, N_MOD * D_MODEL), 0.01),
        'g_norm_mix': gain((L, D_MODEL)),
        'g_norm_ffn': gain((L, D_MODEL)),
        'w_in': nrm((L, D_MODEL, N_IN), D_MODEL ** -0.5),
        'b_gate': nrm((L, N_BRANCH * D_MODEL), 0.02),
        'w_s': nrm((L, A_GROUPS, A_CHUNK, A_CHUNK), A_CHUNK ** -0.5),
        'b_s': gain((L, A_GROUPS, A_CHUNK)),
        'w_gla_decay': nrm((L, 2, B_GATE_RANK, B_HEADS * B_DK), B_GATE_RANK ** -0.5),
        'b_gla_decay': nrm((L, 2, B_HEADS * B_DK), 0.1),
        'g_gla_out': gain((L, B_DV)),
        'g_cq': gain((L, C_Q_RANK)),
        'g_ckv': gain((L, C_KV_RANK)),
        'w_uq': nrm((L, C_Q_RANK, C_HEADS * (C_NOPE + C_ROPE)), C_Q_RANK ** -0.5),
        'w_ukv': nrm((L, C_KV_RANK, C_HEADS * (C_NOPE + C_DV)), C_KV_RANK ** -0.5),
        'g_qn': gain((L, C_NOPE)),
        'g_qr': gain((L, C_ROPE)),
        'g_kn': gain((L, C_NOPE)),
        'g_kr': gain((L, C_ROPE)),
        'w_proj_a': nrm((L, A_WIDTH, D_MODEL), A_WIDTH ** -0.5),
        'w_proj_b': nrm((L, B_HEADS * B_DV, D_MODEL), (B_HEADS * B_DV) ** -0.5),
        'w_proj_c': nrm((L, C_HEADS * C_DV, D_MODEL), (C_HEADS * C_DV) ** -0.5),
        'w_out': nrm((L, D_MODEL, D_MODEL), D_MODEL ** -0.5),
        'w_router': nrm((L, D_MODEL, N_EXPERTS), D_MODEL ** -0.5),
        'w_e_gate': nrm((L, N_EXPERTS, D_MODEL, D_EXPERT), D_MODEL ** -0.5),
        'w_e_up': nrm((L, N_EXPERTS, D_MODEL, D_EXPERT), D_MODEL ** -0.5),
        'w_e_down': nrm((L, N_EXPERTS, D_EXPERT, D_MODEL), D_EXPERT ** -0.5),
    }


def reference(x, c, ctx, c_ctx, w_mod, b_mod, g_norm_mix, g_norm_ffn, w_in, b_gate, w_s, b_s,
              w_gla_decay, b_gla_decay, g_gla_out, g_cq, g_ckv, w_uq, w_ukv, g_qn, g_qr, g_kn, g_kr,
              w_proj_a, w_proj_b, w_proj_c, w_out, w_router, w_e_gate, w_e_up, w_e_down):
    bsz = x.shape[0]
    rope = axial_rope_tables(x.shape[1])
    x_lat, x_ctx = x, ctx
    for i in range(DEPTH):
        p = {
            'w_in': w_in[i], 'b_gate': b_gate[i], 'w_s': w_s[i], 'b_s': b_s[i],
            'w_gla_decay': w_gla_decay[i], 'b_gla_decay': b_gla_decay[i], 'g_gla_out': g_gla_out[i],
            'g_cq': g_cq[i], 'g_ckv': g_ckv[i], 'w_uq': w_uq[i], 'w_ukv': w_ukv[i],
            'g_qn': g_qn[i], 'g_qr': g_qr[i], 'g_kn': g_kn[i], 'g_kr': g_kr[i],
            'w_proj_a': w_proj_a[i], 'w_proj_b': w_proj_b[i], 'w_proj_c': w_proj_c[i], 'w_out': w_out[i],
            'w_router': w_router[i], 'w_e_gate': w_e_gate[i], 'w_e_up': w_e_up[i], 'w_e_down': w_e_down[i],
        }
        need_ctx_out = i < DEPTH - 1
        mod_l = (jax.nn.silu(c) @ w_mod[i] + b_mod[i]).reshape(bsz, N_MOD, 1, D_MODEL)
        mod_c = (jax.nn.silu(c_ctx) @ w_mod[i] + b_mod[i]).reshape(N_MOD, 1, D_MODEL)
        h_l = modulate(x_lat, g_norm_mix[i], mod_l[:, 0], mod_l[:, 1])
        h_c = modulate(x_ctx, g_norm_mix[i], mod_c[0], mod_c[1])
        y_c, y_l = token_mixer(h_c, h_l, rope, p, need_ctx_out)
        x_lat = x_lat + mod_l[:, 2] * y_l
        x_lat = x_lat + mod_l[:, 5] * expert_choice_ffn(modulate(x_lat, g_norm_ffn[i], mod_l[:, 3], mod_l[:, 4]), p)
        if need_ctx_out:
            x_ctx = x_ctx + mod_c[2] * y_c
            x_ctx = x_ctx + mod_c[5] * expert_choice_ffn(modulate(x_ctx, g_norm_ffn[i], mod_c[3], mod_c[4]), p)
    return x_lat
```

```python
import functools

import jax
import jax.numpy as jnp
import numpy as np
from jax import lax
from jax.experimental import pallas as pl
from jax.experimental.pallas import tpu as pltpu

F32 = jnp.float32
BF16 = jnp.bfloat16

D_MODEL = 1024
BATCH = 4
SEQ = 4096
DEPTH = 2
CTX_LEN = 256
T_ALL = CTX_LEN + SEQ
GRID_W = 64
EPS = 1e-6
N_MOD = 6

A_GROUPS = 8
A_CHUNK = 128

B_HEADS = 4
B_DK = 128
B_DV = 256
B_GATE_RANK = 16
B_GATE_NORM = 16.0
B_CHUNK = 64

C_HEADS = 8
C_Q_RANK = 384
C_KV_RANK = 256
C_NOPE = 128
C_ROPE = 64
C_DV = 128
ROPE_BASE = 10000.0
C_QK = 2 * C_NOPE

N_EXPERTS = 16
CAPACITY_FACTOR = 2
D_EXPERT = 2048

Z_BLK = 1024
ZB_GATES, ZB_U, ZB_VA, ZB_VB, ZB_RB, ZB_QK, ZB_SMALL = 0, 3, 4, 5, 6, 7, 8
N_Z = 9 * Z_BLK
SM_CQ, SM_CKV, SM_KR, SM_DEC = 0, C_Q_RANK, C_Q_RANK + C_KV_RANK, C_Q_RANK + C_KV_RANK + 2 * C_ROPE

LANE = 128
VMEM_LIMIT = 56 * 1024 * 1024

TM_ROWS = 544
TM_MM = 1088
TQ = 256
TB_GLA = 256
TF_MOE = 512


def _cparams(sem):
    return pltpu.CompilerParams(dimension_semantics=sem, vmem_limit_bytes=VMEM_LIMIT)


def _rms(x, n):
    return x * lax.rsqrt(jnp.sum(x * x, axis=-1, keepdims=True) * (1.0 / n) + EPS)


def _silu(x):
    return x * (1.0 / (1.0 + jnp.exp(-x)))


def _sigmoid(x):
    return 1.0 / (1.0 + jnp.exp(-x))


def _row_is_ctx(tile_idx, tm):
    row = tile_idx * tm + lax.broadcasted_iota(jnp.int32, (tm, 1), 0)
    return row < CTX_LEN


def _pick_mod(is_ctx, ml_ref, mc_ref, k):
    return jnp.where(is_ctx, mc_ref[0, k:k + 1, :], ml_ref[0, k:k + 1, :])


def _mod_kernel(c_ref, w_ref, b_ref, o_ref):
    s = _silu(c_ref[...])
    o_ref[0] = jnp.dot(s.astype(BF16), w_ref[0].astype(BF16), preferred_element_type=F32) + b_ref[0]


def _mod_table(c8, w_mod, b_mod):
    tn = 512
    n = N_MOD * D_MODEL
    return pl.pallas_call(
        _mod_kernel,
        grid=(DEPTH, n // tn),
        in_specs=[
            pl.BlockSpec((8, D_MODEL), lambda l, j: (0, 0)),
            pl.BlockSpec((1, D_MODEL, tn), lambda l, j: (l, 0, j)),
            pl.BlockSpec((1, 1, tn), lambda l, j: (l, 0, j)),
        ],
        out_specs=pl.BlockSpec((1, 8, tn), lambda l, j: (l, 0, j)),
        out_shape=jax.ShapeDtypeStruct((DEPTH, 8, n), F32),
        compiler_params=_cparams(("parallel", "parallel")),
        name="mod_table",
    )(c8, w_mod, b_mod.reshape(DEPTH, 1, n))


def _norm_kernel(x_ref, g_ref, ml_ref, mc_ref, o_ref, *, tm, k_shift, k_scale):
    is_ctx = _row_is_ctx(pl.program_id(1), tm)
    y = _rms(x_ref[0], D_MODEL) * g_ref[...]
    y = y * (1.0 + _pick_mod(is_ctx, ml_ref, mc_ref, k_scale)) + _pick_mod(is_ctx, ml_ref, mc_ref, k_shift)
    o_ref[0] = y.astype(BF16)


def _mod_norm(x, g, ml, mc, k_shift, k_scale):
    tm = TM_ROWS
    return pl.pallas_call(
        functools.partial(_norm_kernel, tm=tm, k_shift=k_shift, k_scale=k_scale),
        grid=(BATCH, T_ALL // tm),
        in_specs=[
            pl.BlockSpec((1, tm, D_MODEL), lambda b, i: (b, i, 0)),
            pl.BlockSpec((1, D_MODEL), lambda b, i: (0, 0)),
            pl.BlockSpec((1, N_MOD, D_MODEL), lambda b, i: (b, 0, 0)),
            pl.BlockSpec((1, N_MOD, D_MODEL), lambda b, i: (0, 0, 0)),
        ],
        out_specs=pl.BlockSpec((1, tm, D_MODEL), lambda b, i: (b, i, 0)),
        out_shape=jax.ShapeDtypeStruct((BATCH, T_ALL, D_MODEL), BF16),
        compiler_params=_cparams(("parallel", "parallel")),
        name="mod_norm",
    )(x, g, ml, mc)


def _matmul_kernel(a_ref, w_ref, o_ref):
    o_ref[...] = jnp.dot(a_ref[...], w_ref[...], preferred_element_type=F32).astype(o_ref.dtype)


def _in_proj(h2d, w):
    m, k = h2d.shape
    n = w.shape[1]
    tm, tn = TM_MM, Z_BLK
    return pl.pallas_call(
        _matmul_kernel,
        grid=(m // tm, n // tn),
        in_specs=[
            pl.BlockSpec((tm, k), lambda i, j: (i, 0)),
            pl.BlockSpec((k, tn), lambda i, j: (0, j)),
        ],
        out_specs=pl.BlockSpec((tm, tn), lambda i, j: (i, j)),
        out_shape=jax.ShapeDtypeStruct((m, n), BF16),
        compiler_params=_cparams(("parallel", "parallel")),
        name="in_proj",
    )(h2d, w)


def _mix_a_kernel(u_ref, v_ref, ws_ref, bs_ref, o_ref, *, n_chunk):
    def chunk(c, carry):
        r0 = pl.multiple_of(c * A_CHUNK, A_CHUNK)
        for g in range(A_GROUPS):
            cols = slice(g * LANE, (g + 1) * LANE)
            v = v_ref[0, pl.ds(r0, A_CHUNK), cols].astype(F32)
            mu = jnp.mean(v, axis=-1, keepdims=True)
            d = v - mu
            ln = d * lax.rsqrt(jnp.mean(d * d, axis=-1, keepdims=True) + EPS)
            mixed = jnp.dot(ws_ref[g], ln.astype(BF16), preferred_element_type=F32) + bs_ref[:, cols]
            u = u_ref[0, pl.ds(r0, A_CHUNK), cols].astype(F32)
            o_ref[0, pl.ds(r0, A_CHUNK), cols] = (u * mixed).astype(BF16)
        return carry

    lax.fori_loop(0, n_chunk, chunk, 0)


def _mix_a(z, ws, bs_full):
    ta = T_ALL // 2
    return pl.pallas_call(
        functools.partial(_mix_a_kernel, n_chunk=ta // A_CHUNK),
        grid=(BATCH, T_ALL // ta),
        in_specs=[
            pl.BlockSpec((1, ta, Z_BLK), lambda b, i: (b, i, ZB_U)),
            pl.BlockSpec((1, ta, Z_BLK), lambda b, i: (b, i, ZB_VA)),
            pl.BlockSpec((A_GROUPS, A_CHUNK, A_CHUNK), lambda b, i: (0, 0, 0)),
            pl.BlockSpec((A_CHUNK, D_MODEL), lambda b, i: (0, 0)),
        ],
        out_specs=pl.BlockSpec((1, ta, D_MODEL), lambda b, i: (b, i, 0)),
        out_shape=jax.ShapeDtypeStruct((BATCH, T_ALL, D_MODEL), BF16),
        compiler_params=_cparams(("parallel", "parallel")),
        name="mix_a",
    )(z, z, ws, bs_full)


_NT = (((1,), (1,)), ((), ()))
_TN = (((0,), (0,)), ((), ()))


def _log_sigmoid(x):
    return jnp.minimum(x, 0.0) - jnp.log1p(jnp.exp(-jnp.abs(x)))


def _exact_tri_sum(tri, la):
    hi = la.astype(BF16)
    r1 = la - hi.astype(F32)
    mid = r1.astype(BF16)
    lo = (r1 - mid.astype(F32)).astype(BF16)
    return (jnp.dot(tri, hi, preferred_element_type=F32)
            + jnp.dot(tri, mid, preferred_element_type=F32)
            + jnp.dot(tri, lo, preferred_element_type=F32))


def _gla_chunk(qk_ref, v_ref, sm_ref, wdec_ref, bdec_ref, st_ref, r0, keep, edge, emit):
    rows = pl.ds(r0, B_CHUNK)
    dec = sm_ref[0, rows, SM_DEC:SM_DEC + LANE]
    pre = jnp.dot(dec, wdec_ref[...], preferred_element_type=F32) + bdec_ref[...]
    la = _log_sigmoid(pre) * (1.0 / B_GATE_NORM)
    bcum = _exact_tri_sum(jnp.where(keep, 1.0, 0.0).astype(BF16), la)
    for h in range(B_HEADS):
        kc = slice(h * B_DK, (h + 1) * B_DK)
        vc = slice(h * B_DV, (h + 1) * B_DV)
        bh = bcum[:, kc]
        q = qk_ref[0, rows, kc].astype(F32)
        k = qk_ref[0, rows, B_HEADS * B_DK + h * B_DK:B_HEADS * B_DK + (h + 1) * B_DK].astype(F32)
        v = v_ref[0, rows, vc]
        b_last = bh[edge:edge + 1, :]
        qd = ((q * B_DK ** -0.5) * jnp.exp(bh)).astype(BF16)
        kinv = (k * jnp.exp(-bh)).astype(BF16)
        kdec = (k * jnp.exp(b_last - bh)).astype(BF16)
        s = lax.dot_general(qd, kinv, _NT, preferred_element_type=F32)
        s = jnp.where(keep, s, 0.0).astype(BF16)
        st = st_ref[h]
        o = (jnp.dot(s, v, preferred_element_type=F32)
             + lax.dot_general(qd, st.astype(BF16), _NT, preferred_element_type=F32))
        emit(h, o)
        kv_t = lax.dot_general(v, kdec, _TN, preferred_element_type=F32)
        st_ref[h] = jnp.exp(b_last) * st + kv_t


def _gla_fwd_kernel(qk_ref, v_ref, sm_ref, wdec_ref, bdec_ref, o_ref, st_ref, *, n_chunk):
    @pl.when(pl.program_id(1) == 0)
    def _():
        st_ref[...] = jnp.zeros_like(st_ref)

    ii = lax.broadcasted_iota(jnp.int32, (B_CHUNK, B_CHUNK), 0)
    jj = lax.broadcasted_iota(jnp.int32, (B_CHUNK, B_CHUNK), 1)
    keep = jj <= ii

    def chunk(c, carry):
        r0 = pl.multiple_of(c * B_CHUNK, B_CHUNK)

        def emit(h, o):
            o_ref[0, pl.ds(r0, B_CHUNK), h * B_DV:(h + 1) * B_DV] = o

        _gla_chunk(qk_ref, v_ref, sm_ref, wdec_ref, bdec_ref, st_ref, r0, keep, B_CHUNK - 1, emit)
        return carry

    lax.fori_loop(0, n_chunk, chunk, 0)


def _gla_bwd_kernel(qk_ref, v_ref, sm_ref, wdec_ref, bdec_ref, of_ref, r_ref, g_ref, y_ref, st_ref, *, n_chunk):
    @pl.when(pl.program_id(1) == 0)
    def _():
        st_ref[...] = jnp.zeros_like(st_ref)

    ii = lax.broadcasted_iota(jnp.int32, (B_CHUNK, B_CHUNK), 0)
    jj = lax.broadcasted_iota(jnp.int32, (B_CHUNK, B_CHUNK), 1)
    keep = jj >= ii

    def chunk(c, carry):
        r0 = pl.multiple_of((n_chunk - 1 - c) * B_CHUNK, B_CHUNK)
        rows = pl.ds(r0, B_CHUNK)

        def emit(h, o):
            vc = slice(h * B_DV, (h + 1) * B_DV)
            o = o + of_ref[0, rows, vc]
            on = _rms(o, B_DV) * g_ref[...]
            y_ref[0, rows, vc] = (on * _silu(r_ref[0, rows, vc].astype(F32))).astype(BF16)

        _gla_chunk(qk_ref, v_ref, sm_ref, wdec_ref, bdec_ref, st_ref, r0, keep, 0, emit)
        return carry

    lax.fori_loop(0, n_chunk, chunk, 0)


def _bwd_block(i):
    nb = T_ALL // TB_GLA
    return jnp.where(i == 0, 0, nb - i)


def _gla(z, wdec_f, bdec_f, wdec_b, bdec_b, g_out):
    tb = TB_GLA
    nb = T_ALL // tb
    n_chunk = tb // B_CHUNK
    wide = B_HEADS * B_DV

    def zspec(blk, imap):
        return pl.BlockSpec((1, tb, Z_BLK), lambda b, i: (b, imap(i), blk))

    def const2(shape):
        return pl.BlockSpec(shape, lambda b, i: (0, 0))

    state = pltpu.VMEM((B_HEADS, B_DV, B_DK), F32)
    fwd = lambda i: i
    o_f = pl.pallas_call(
        functools.partial(_gla_fwd_kernel, n_chunk=n_chunk),
        grid=(BATCH, nb),
        in_specs=[zspec(ZB_QK, fwd), zspec(ZB_VB, fwd), zspec(ZB_SMALL, fwd),
                  const2((LANE, B_HEADS * B_DK)), const2((1, B_HEADS * B_DK))],
        out_specs=pl.BlockSpec((1, tb, wide), lambda b, i: (b, i, 0)),
        out_shape=jax.ShapeDtypeStruct((BATCH, T_ALL, wide), F32),
        scratch_shapes=[state],
        compiler_params=_cparams(("parallel", "arbitrary")),
        name="gla_fwd",
    )(z, z, z, wdec_f, bdec_f)
    return pl.pallas_call(
        functools.partial(_gla_bwd_kernel, n_chunk=n_chunk),
        grid=(BATCH, nb),
        in_specs=[zspec(ZB_QK, _bwd_block), zspec(ZB_VB, _bwd_block), zspec(ZB_SMALL, _bwd_block),
                  const2((LANE, B_HEADS * B_DK)), const2((1, B_HEADS * B_DK)),
                  pl.BlockSpec((1, tb, wide), lambda b, i: (b, _bwd_block(i), 0)),
                  zspec(ZB_RB, _bwd_block), const2((1, B_DV))],
        out_specs=pl.BlockSpec((1, tb, wide), lambda b, i: (b, _bwd_block(i), 0)),
        out_shape=jax.ShapeDtypeStruct((BATCH, T_ALL, wide), BF16),
        scratch_shapes=[state],
        compiler_params=_cparams(("parallel", "arbitrary")),
        name="gla_bwd",
    )(z, z, z, wdec_b, bdec_b, o_f, z, g_out)


def _mla_prep_kernel(sm_ref, cs_ref, wq_ref, wkv_ref, gcq_ref, gckv_ref, gqn_ref, gqr_ref, gkn_ref, gkr_ref,
                     q_ref, k_ref, v_ref):
    scale = (C_NOPE + C_ROPE) ** -0.5
    cs = cs_ref[...]
    lane = lax.broadcasted_iota(jnp.int32, (1, LANE), 1)
    low_half = (lane < C_ROPE).astype(F32)

    def rope(pack, gain):
        t = _rms(pack, LANE) * gain * cs
        return t + pltpu.roll(t, C_ROPE, axis=1)

    cq = sm_ref[0, :, SM_CQ:SM_CQ + C_Q_RANK].astype(F32)
    cqn = (_rms(cq, C_Q_RANK) * gcq_ref[...]).astype(BF16)
    q_all = jnp.dot(cqn, wq_ref[...], preferred_element_type=F32)
    ckv = sm_ref[0, :, SM_CKV:SM_CKV + C_KV_RANK].astype(F32)
    ckvn = (_rms(ckv, C_KV_RANK) * gckv_ref[...]).astype(BF16)
    kv_all = jnp.dot(ckvn, wkv_ref[...], preferred_element_type=F32)
    k_rot = rope(sm_ref[0, :, SM_KR:SM_KR + LANE].astype(F32), gkr_ref[...]).astype(BF16)
    for h in range(C_HEADS):
        c0 = h * C_QK
        qn = _rms(q_all[:, c0:c0 + C_NOPE], C_NOPE) * gqn_ref[...]
        q_ref[0, h, :, 0:C_NOPE] = (qn * scale).astype(BF16)
        q_rot = rope(q_all[:, c0 + C_NOPE:c0 + C_QK], gqr_ref[...]) * low_half
        q_ref[0, h, :, C_NOPE:C_QK] = (q_rot * scale).astype(BF16)
        kn = _rms(kv_all[:, c0:c0 + C_NOPE], C_NOPE) * gkn_ref[...]
        k_ref[0, h, :, 0:C_NOPE] = kn.astype(BF16)
        k_ref[0, h, :, C_NOPE:C_QK] = k_rot
        v_ref[0, h] = kv_all[:, c0 + C_NOPE:c0 + C_QK].astype(BF16)


def _mla_prep(z, cs_tab, wq, wkv, gcq, gckv, gqn, gqr2, gkn, gkr2):
    tm = TM_ROWS

    def const2(shape):
        return pl.BlockSpec(shape, lambda b, i: (0, 0))

    hspec = lambda w: pl.BlockSpec((1, C_HEADS, tm, w), lambda b, i: (b, 0, i, 0))
    hshape = lambda w: jax.ShapeDtypeStruct((BATCH, C_HEADS, T_ALL, w), BF16)
    return pl.pallas_call(
        _mla_prep_kernel,
        grid=(BATCH, T_ALL // tm),
        in_specs=[
            pl.BlockSpec((1, tm, Z_BLK), lambda b, i: (b, i, ZB_SMALL)),
            pl.BlockSpec((tm, LANE), lambda b, i: (i, 0)),
            const2((C_Q_RANK, C_HEADS * C_QK)), const2((C_KV_RANK, C_HEADS * C_QK)),
            const2((1, C_Q_RANK)), const2((1, C_KV_RANK)),
            const2((1, C_NOPE)), const2((1, LANE)), const2((1, C_NOPE)), const2((1, LANE)),
        ],
        out_specs=[hspec(C_QK), hspec(C_QK), hspec(C_DV)],
        out_shape=[hshape(C_QK), hshape(C_QK), hshape(C_DV)],
        compiler_params=_cparams(("parallel", "parallel")),
        name="mla_prep",
    )(z, cs_tab, wq, wkv, gcq, gckv, gqn, gqr2, gkn, gkr2)


def _attn_kernel(q_ref, k_ref, v_ref, o_ref):
    def attend(n_keys):
        q = q_ref[0, 0]
        s = lax.dot_general(q, k_ref[0, 0, 0:n_keys, :], _NT, preferred_element_type=F32)
        p = jnp.exp(s - jnp.max(s, axis=-1, keepdims=True))
        o = jnp.dot(p.astype(BF16), v_ref[0, 0, 0:n_keys, :], preferred_element_type=F32)
        o_ref[0] = (o * (1.0 / jnp.sum(p, axis=-1, keepdims=True))).astype(BF16)

    @pl.when(pl.program_id(2) == 0)
    def _():
        attend(CTX_LEN)

    @pl.when(pl.program_id(2) > 0)
    def _():
        attend(T_ALL)


def _attention(q, k, v):
    tq = TQ
    return pl.pallas_call(
        _attn_kernel,
        grid=(BATCH, C_HEADS, T_ALL // tq),
        in_specs=[
            pl.BlockSpec((1, 1, tq, C_QK), lambda b, h, i: (b, h, i, 0)),
            pl.BlockSpec((1, 1, T_ALL, C_QK), lambda b, h, i: (b, h, 0, 0)),
            pl.BlockSpec((1, 1, T_ALL, C_DV), lambda b, h, i: (b, h, 0, 0)),
        ],
        out_specs=pl.BlockSpec((1, tq, C_DV), lambda b, h, i: (b, i, h)),
        out_shape=jax.ShapeDtypeStruct((BATCH, T_ALL, C_HEADS * C_DV), BF16),
        compiler_params=_cparams(("parallel", "parallel", "parallel")),
        name="mla_attention",
    )(q, k, v)


def _merge_kernel(ya_ref, yb_ref, yc_ref, gt_ref, bg_ref, wa_ref, wb_ref, wc_ref, wo_ref, x_ref, ml_ref, mc_ref,
                  gn_ref, wr_ref, x1_ref, h2_ref, aff_ref, *, tm):
    is_ctx = _row_is_ctx(pl.program_id(1), tm)
    merged = None
    for j, (y_ref, w_ref) in enumerate(((ya_ref, wa_ref), (yb_ref, wb_ref), (yc_ref, wc_ref))):
        cols = slice(j * D_MODEL, (j + 1) * D_MODEL)
        gate = _sigmoid(gt_ref[0, :, cols].astype(F32) + bg_ref[:, cols])
        term = gate * jnp.dot(y_ref[0], w_ref[...], preferred_element_type=F32)
        merged = term if merged is None else merged + term
    y = jnp.dot(merged.astype(BF16), wo_ref[...], preferred_element_type=F32)
    x1 = x_ref[0] + _pick_mod(is_ctx, ml_ref, mc_ref, 2) * y
    x1_ref[0] = x1
    h2 = _rms(x1, D_MODEL) * gn_ref[...]
    h2 = (h2 * (1.0 + _pick_mod(is_ctx, ml_ref, mc_ref, 4)) + _pick_mod(is_ctx, ml_ref, mc_ref, 3)).astype(BF16)
    h2_ref[0] = h2
    logits = jnp.dot(h2, wr_ref[...], preferred_element_type=F32)
    lane = lax.broadcasted_iota(jnp.int32, (1, LANE), 1)
    logits = jnp.where(lane < N_EXPERTS, logits, -jnp.inf)
    e = jnp.exp(logits - jnp.max(logits, axis=-1, keepdims=True))
    aff_ref[0] = e / jnp.sum(e, axis=-1, keepdims=True)


def _merge(ya, yb, yc, z, b_gate, wa, wb, wc, wo, x, ml, mc, g_ffn, w_router):
    tm = TM_ROWS
    row = lambda w: pl.BlockSpec((1, tm, w), lambda b, i: (b, i, 0))

    def const2(shape):
        return pl.BlockSpec(shape, lambda b, i: (0, 0))

    sq = const2((D_MODEL, D_MODEL))
    return pl.pallas_call(
        functools.partial(_merge_kernel, tm=tm),
        grid=(BATCH, T_ALL // tm),
        in_specs=[
            row(D_MODEL), row(D_MODEL), row(D_MODEL), row(3 * D_MODEL), const2((1, 3 * D_MODEL)),
            sq, sq, sq, sq, row(D_MODEL),
            pl.BlockSpec((1, N_MOD, D_MODEL), lambda b, i: (b, 0, 0)),
            pl.BlockSpec((1, N_MOD, D_MODEL), lambda b, i: (0, 0, 0)),
            const2((1, D_MODEL)), const2((D_MODEL, LANE)),
        ],
        out_specs=[row(D_MODEL), row(D_MODEL), row(LANE)],
        out_shape=[jax.ShapeDtypeStruct((BATCH, T_ALL, D_MODEL), F32),
                   jax.ShapeDtypeStruct((BATCH, T_ALL, D_MODEL), BF16),
                   jax.ShapeDtypeStruct((BATCH, T_ALL, LANE), F32)],
        compiler_params=_cparams(("parallel", "parallel")),
        name="merge",
    )(ya, yb, yc, z, b_gate, wa, wb, wc, wo, x, ml, mc, g_ffn, w_router)


def _moe_kernel(x_ref, g_ref, wg_ref, wu_ref, wd_ref, o_ref):
    f = pl.program_id(2)
    x = x_ref[0]
    hid = (_silu(jnp.dot(x, wg_ref[0], preferred_element_type=F32))
           * jnp.dot(x, wu_ref[0], preferred_element_type=F32))
    part = jnp.dot(hid.astype(BF16), wd_ref[0], preferred_element_type=F32)

    @pl.when(f == 0)
    def _():
        o_ref[0] = part

    @pl.when(f > 0)
    def _():
        o_ref[0] += part

    @pl.when(f == pl.num_programs(2) - 1)
    def _():
        o_ref[0] = o_ref[0] * g_ref[0]


def _moe_ffn(xs, gs, wg, wu, wd):
    n_rows = xs.shape[1]
    tr = n_rows // 2
    tf = TF_MOE
    return pl.pallas_call(
        _moe_kernel,
        grid=(N_EXPERTS, n_rows // tr, D_EXPERT // tf),
        in_specs=[
            pl.BlockSpec((1, tr, D_MODEL), lambda e, r, f: (e, r, 0)),
            pl.BlockSpec((1, tr, 1), lambda e, r, f: (e, r, 0)),
            pl.BlockSpec((1, D_MODEL, tf), lambda e, r, f: (e, 0, f)),
            pl.BlockSpec((1, D_MODEL, tf), lambda e, r, f: (e, 0, f)),
            pl.BlockSpec((1, tf, D_MODEL), lambda e, r, f: (e, f, 0)),
        ],
        out_specs=pl.BlockSpec((1, tr, D_MODEL), lambda e, r, f: (e, r, 0)),
        out_shape=jax.ShapeDtypeStruct((N_EXPERTS, n_rows, D_MODEL), F32),
        compiler_params=_cparams(("parallel", "parallel", "arbitrary")),
        name="moe_ffn",
    )(xs, gs, wg, wu, wd)


def _rope_swap_perm():
    idx = np.arange(C_ROPE).reshape(2, 2, C_ROPE // 4)
    return idx[:, ::-1, :].reshape(-1)


def _rope_table():
    rows = SEQ // GRID_W
    row = jnp.broadcast_to(jnp.arange(rows, dtype=F32)[:, None], (rows, GRID_W)).reshape(-1)
    col = jnp.broadcast_to(jnp.arange(GRID_W, dtype=F32)[None, :], (rows, GRID_W)).reshape(-1)
    axis_dim = C_ROPE // 2
    inv_freq = ROPE_BASE ** (-jnp.arange(0, axis_dim, 2, dtype=F32) / axis_dim)
    n_freq = C_ROPE // 4
    ang = jnp.stack([row[:, None] * inv_freq, col[:, None] * inv_freq], axis=1)
    cos = jnp.broadcast_to(jnp.cos(ang)[:, :, None, :], (SEQ, 2, 2, n_freq)).reshape(SEQ, C_ROPE)
    sin = jnp.sin(ang)
    sin = jnp.stack([-sin, sin], axis=2).reshape(SEQ, C_ROPE)
    lat = jnp.concatenate([cos, sin], axis=1)
    ctx = jnp.concatenate([jnp.ones((CTX_LEN, C_ROPE), F32), jnp.zeros((CTX_LEN, C_ROPE), F32)], axis=1)
    return jnp.concatenate([ctx, lat], axis=0)


def _layer_weights(i, w_in, w_gla_decay, b_gla_decay, w_uq, g_qr, g_kr, b_s):
    perm = _rope_swap_perm()
    o = np.cumsum((0, 1024, 1024, 512, 512, 1024, 1024, 32, C_Q_RANK, C_KV_RANK, C_ROPE, 3 * D_MODEL))
    w = w_in[i]
    seg = lambda j: w[:, o[j]:o[j + 1]]
    u, va, qb, kb, vb, rb, dec, cq, ckv, kr, gates = (seg(j) for j in range(11))
    pad = jnp.zeros((D_MODEL, Z_BLK - (SM_DEC + 2 * B_GATE_RANK)), F32)
    w_in_ext = jnp.concatenate([gates, u, va, vb, rb, qb, kb, cq, ckv, kr, kr[:, perm], dec, pad], axis=1).astype(BF16)

    def dec_weight(j):
        wd = jnp.zeros((LANE, B_HEADS * B_DK), F32)
        return wd.at[j * B_GATE_RANK:(j + 1) * B_GATE_RANK].set(w_gla_decay[i, j]).astype(BF16)

    wq = w_uq[i].reshape(C_Q_RANK, C_HEADS, C_NOPE + C_ROPE)
    wq_ext = jnp.concatenate([wq, wq[:, :, C_NOPE:][:, :, perm]], axis=2).reshape(C_Q_RANK, C_HEADS * C_QK).astype(BF16)
    pack = lambda g: jnp.concatenate([g[i], g[i][perm]])[None, :]
    bs_full = jnp.repeat(b_s[i].T, A_CHUNK, axis=1)
    return dict(w_in=w_in_ext, wdec_f=dec_weight(0), wdec_b=dec_weight(1),
                bdec_f=b_gla_decay[i, 0][None, :], bdec_b=b_gla_decay[i, 1][None, :],
                wq=wq_ext, gqr2=pack(g_qr), gkr2=pack(g_kr), bs_full=bs_full)


def _route(aff, h2, n_tok, t0):
    cap = CAPACITY_FACTOR * n_tok // N_EXPERTS
    a = jnp.swapaxes(aff[:, t0:t0 + n_tok, :N_EXPERTS], 1, 2)
    g, idx = lax.top_k(a, cap)
    xs = jax.vmap(lambda hb, ib: hb[ib])(h2[:, t0:t0 + n_tok], idx)
    xs = jnp.swapaxes(xs, 0, 1).reshape(N_EXPERTS, BATCH * cap, D_MODEL)
    gs = jnp.swapaxes(g, 0, 1).reshape(N_EXPERTS, BATCH * cap, 1)
    return xs, gs, idx, cap


def _unroute(ys, idx, cap, n_tok):
    ys = jnp.swapaxes(ys.reshape(N_EXPERTS, BATCH, cap, D_MODEL), 0, 1)

    def scatter(yb, ib):
        return jnp.zeros((n_tok, D_MODEL), yb.dtype).at[ib.reshape(-1)].add(yb.reshape(-1, D_MODEL))

    return jax.vmap(scatter)(ys, idx)


def kernel(x, c, ctx, c_ctx, w_mod, b_mod, g_norm_mix, g_norm_ffn, w_in, b_gate, w_s, b_s, w_gla_decay, b_gla_decay,
           g_gla_out, g_cq, g_ckv, w_uq, w_ukv, g_qn, g_qr, g_kn, g_kr, w_proj_a, w_proj_b, w_proj_c, w_out,
           w_router, w_e_gate, w_e_up, w_e_down):
    c8 = jnp.concatenate([c, c_ctx[None, :], jnp.zeros((8 - BATCH - 1, D_MODEL), F32)], axis=0)
    mod = _mod_table(c8, w_mod, b_mod).reshape(DEPTH, 8, N_MOD, D_MODEL)
    cs_tab = _rope_table()
    xc = jnp.concatenate([ctx, x], axis=1)
    row2 = lambda a: a[None, :]
    for i in range(DEPTH):
        lw = _layer_weights(i, w_in, w_gla_decay, b_gla_decay, w_uq, g_qr, g_kr, b_s)
        ml, mc = mod[i, :BATCH], mod[i, BATCH:BATCH + 1]
        h = _mod_norm(xc, row2(g_norm_mix[i]), ml, mc, 0, 1)
        z = _in_proj(h.reshape(BATCH * T_ALL, D_MODEL), lw['w_in']).reshape(BATCH, T_ALL, N_Z)
        ya = _mix_a(z, w_s[i].astype(BF16), lw['bs_full'])
        yb = _gla(z, lw['wdec_f'], lw['bdec_f'], lw['wdec_b'], lw['bdec_b'], row2(g_gla_out[i]))
        q, k, v = _mla_prep(z, cs_tab, lw['wq'], w_ukv[i].astype(BF16), row2(g_cq[i]), row2(g_ckv[i]),
                            row2(g_qn[i]), lw['gqr2'], row2(g_kn[i]), lw['gkr2'])
        yc = _attention(q, k, v)
        w_r = jnp.concatenate([w_router[i], jnp.zeros((D_MODEL, LANE - N_EXPERTS), F32)], axis=1).astype(BF16)
        x1, h2, aff = _merge(ya, yb, yc, z, row2(b_gate[i]), w_proj_a[i].astype(BF16), w_proj_b[i].astype(BF16),
                             w_proj_c[i].astype(BF16), w_out[i].astype(BF16), xc, ml, mc, row2(g_norm_ffn[i]), w_r)
        wg, wu, wd = w_e_gate[i].astype(BF16), w_e_up[i].astype(BF16), w_e_down[i].astype(BF16)
        xs_l, gs_l, idx_l, cap_l = _route(aff, h2, SEQ, CTX_LEN)
        if i < DEPTH - 1:
            xs_c, gs_c, idx_c, cap_c = _route(aff, h2, CTX_LEN, 0)
            ys = _moe_ffn(jnp.concatenate([xs_l, xs_c], axis=1), jnp.concatenate([gs_l, gs_c], axis=1), wg, wu, wd)
            n_l = BATCH * cap_l
            moe = jnp.concatenate([_unroute(ys[:, n_l:], idx_c, cap_c, CTX_LEN),
                                   _unroute(ys[:, :n_l], idx_l, cap_l, SEQ)], axis=1)
            gate = jnp.concatenate([jnp.broadcast_to(mc[:, 5:6], (BATCH, CTX_LEN, D_MODEL)),
                                    jnp.broadcast_to(ml[:, 5:6], (BATCH, SEQ, D_MODEL))], axis=1)
            xc = x1 + gate * moe
        else:
            ys = _moe_ffn(xs_l, gs_l, wg, wu, wd)
            return x1[:, CTX_LEN:] + ml[:, 5:6] * _unroute(ys, idx_l, cap_l, SEQ)
```

```python
import functools

import jax
import jax.numpy as jnp
import numpy as np
from jax import lax
from jax.experimental import pallas as pl
from jax.experimental.pallas import tpu as pltpu

F32 = jnp.float32
BF16 = jnp.bfloat16

D_MODEL = 1024
BATCH = 4
SEQ = 4096
DEPTH = 2
CTX_LEN = 256
T_ALL = CTX_LEN + SEQ
GRID_W = 64
EPS = 1e-6
N_MOD = 6

A_GROUPS = 8
A_CHUNK = 128

B_HEADS = 4
B_DK = 128
B_DV = 256
B_GATE_RANK = 16
B_GATE_NORM = 16.0
B_CHUNK = 64

C_HEADS = 8
C_Q_RANK = 384
C_KV_RANK = 256
C_NOPE = 128
C_ROPE = 64
C_DV = 128
ROPE_BASE = 10000.0
C_QK = 2 * C_NOPE

N_EXPERTS = 16
CAPACITY_FACTOR = 2
D_EXPERT = 2048

Z_BLK = 1024
ZB_GATES, ZB_U, ZB_VA, ZB_VB, ZB_RB, ZB_QK, ZB_SMALL = 0, 3, 4, 5, 6, 7, 8
N_Z = 9 * Z_BLK
SM_CQ, SM_CKV, SM_KR, SM_DEC = 0, C_Q_RANK, C_Q_RANK + C_KV_RANK, C_Q_RANK + C_KV_RANK + 2 * C_ROPE

LANE = 128
VMEM_LIMIT = 56 * 1024 * 1024

TM_ROWS = 544
TM_MM = 1088
TQ = 256
TB_GLA = 256
TF_MOE = 512


def _cparams(sem):
    return pltpu.CompilerParams(dimension_semantics=sem, vmem_limit_bytes=VMEM_LIMIT)


def _rms(x, n):
    return x * lax.rsqrt(jnp.sum(x * x, axis=-1, keepdims=True) * (1.0 / n) + EPS)


def _silu(x):
    return x * (1.0 / (1.0 + jnp.exp(-x)))


def _sigmoid(x):
    return 1.0 / (1.0 + jnp.exp(-x))


def _row_is_ctx(tile_idx, tm):
    row = tile_idx * tm + lax.broadcasted_iota(jnp.int32, (tm, 1), 0)
    return row < CTX_LEN


def _pick_mod(is_ctx, ml_ref, mc_ref, k):
    return jnp.where(is_ctx, mc_ref[0, k:k + 1, :], ml_ref[0, k:k + 1, :])


def _mod_kernel(c_ref, w_ref, b_ref, o_ref):
    s = _silu(c_ref[...])
    o_ref[0] = jnp.dot(s.astype(BF16), w_ref[0].astype(BF16), preferred_element_type=F32) + b_ref[0]


def _mod_table(c8, w_mod, b_mod):
    tn = 512
    n = N_MOD * D_MODEL
    return pl.pallas_call(
        _mod_kernel,
        grid=(DEPTH, n // tn),
        in_specs=[
            pl.BlockSpec((8, D_MODEL), lambda l, j: (0, 0)),
            pl.BlockSpec((1, D_MODEL, tn), lambda l, j: (l, 0, j)),
            pl.BlockSpec((1, 1, tn), lambda l, j: (l, 0, j)),
        ],
        out_specs=pl.BlockSpec((1, 8, tn), lambda l, j: (l, 0, j)),
        out_shape=jax.ShapeDtypeStruct((DEPTH, 8, n), F32),
        compiler_params=_cparams(("parallel", "parallel")),
        name="mod_table",
    )(c8, w_mod, b_mod.reshape(DEPTH, 1, n))


def _norm_kernel(x_ref, g_ref, ml_ref, mc_ref, o_ref, *, tm, k_shift, k_scale):
    is_ctx = _row_is_ctx(pl.program_id(1), tm)
    y = _rms(x_ref[0], D_MODEL) * g_ref[...]
    y = y * (1.0 + _pick_mod(is_ctx, ml_ref, mc_ref, k_scale)) + _pick_mod(is_ctx, ml_ref, mc_ref, k_shift)
    o_ref[0] = y.astype(BF16)


def _mod_norm(x, g, ml, mc, k_shift, k_scale):
    tm = TM_ROWS
    return pl.pallas_call(
        functools.partial(_norm_kernel, tm=tm, k_shift=k_shift, k_scale=k_scale),
        grid=(BATCH, T_ALL // tm),
        in_specs=[
            pl.BlockSpec((1, tm, D_MODEL), lambda b, i: (b, i, 0)),
            pl.BlockSpec((1, D_MODEL), lambda b, i: (0, 0)),
            pl.BlockSpec((1, N_MOD, D_MODEL), lambda b, i: (b, 0, 0)),
            pl.BlockSpec((1, N_MOD, D_MODEL), lambda b, i: (0, 0, 0)),
        ],
        out_specs=pl.BlockSpec((1, tm, D_MODEL), lambda b, i: (b, i, 0)),
        out_shape=jax.ShapeDtypeStruct((BATCH, T_ALL, D_MODEL), BF16),
        compiler_params=_cparams(("parallel", "parallel")),
        name="mod_norm",
    )(x, g, ml, mc)


def _matmul_kernel(a_ref, w_ref, o_ref):
    o_ref[...] = jnp.dot(a_ref[...], w_ref[...], preferred_element_type=F32).astype(o_ref.dtype)


def _in_proj(h2d, w):
    m, k = h2d.shape
    n = w.shape[1]
    tm, tn = TM_MM, Z_BLK
    return pl.pallas_call(
        _matmul_kernel,
        grid=(m // tm, n // tn),
        in_specs=[
            pl.BlockSpec((tm, k), lambda i, j: (i, 0)),
            pl.BlockSpec((k, tn), lambda i, j: (0, j)),
        ],
        out_specs=pl.BlockSpec((tm, tn), lambda i, j: (i, j)),
        out_shape=jax.ShapeDtypeStruct((m, n), BF16),
        compiler_params=_cparams(("parallel", "parallel")),
        name="in_proj",
    )(h2d, w)


def _mix_a_kernel(u_ref, v_ref, ws_ref, bs_ref, o_ref, *, n_chunk):
    def chunk(c, carry):
        r0 = pl.multiple_of(c * A_CHUNK, A_CHUNK)
        for g in range(A_GROUPS):
            cols = slice(g * LANE, (g + 1) * LANE)
            v = v_ref[0, pl.ds(r0, A_CHUNK), cols].astype(F32)
            mu = jnp.mean(v, axis=-1, keepdims=True)
            d = v - mu
            ln = d * lax.rsqrt(jnp.mean(d * d, axis=-1, keepdims=True) + EPS)
            mixed = jnp.dot(ws_ref[g], ln.astype(BF16), preferred_element_type=F32) + bs_ref[:, cols]
            u = u_ref[0, pl.ds(r0, A_CHUNK), cols].astype(F32)
            o_ref[0, pl.ds(r0, A_CHUNK), cols] = (u * mixed).astype(BF16)
        return carry

    lax.fori_loop(0, n_chunk, chunk, 0)


def _mix_a(z, ws, bs_full):
    ta = T_ALL // 2
    return pl.pallas_call(
        functools.partial(_mix_a_kernel, n_chunk=ta // A_CHUNK),
        grid=(BATCH, T_ALL // ta),
        in_specs=[
            pl.BlockSpec((1, ta, Z_BLK), lambda b, i: (b, i, ZB_U)),
            pl.BlockSpec((1, ta, Z_BLK), lambda b, i: (b, i, ZB_VA)),
            pl.BlockSpec((A_GROUPS, A_CHUNK, A_CHUNK), lambda b, i: (0, 0, 0)),
            pl.BlockSpec((A_CHUNK, D_MODEL), lambda b, i: (0, 0)),
        ],
        out_specs=pl.BlockSpec((1, ta, D_MODEL), lambda b, i: (b, i, 0)),
        out_shape=jax.ShapeDtypeStruct((BATCH, T_ALL, D_MODEL), BF16),
        compiler_params=_cparams(("parallel", "parallel")),
        name="mix_a",
    )(z, z, ws, bs_full)


_NT = (((1,), (1,)), ((), ()))
_TN = (((0,), (0,)), ((), ()))


def _log_sigmoid(x):
    return jnp.minimum(x, 0.0) - jnp.log1p(jnp.exp(-jnp.abs(x)))


def _exact_tri_sum(tri, la):
    hi = la.astype(BF16)
    r1 = la - hi.astype(F32)
    mid = r1.astype(BF16)
    lo = (r1 - mid.astype(F32)).astype(BF16)
    return (jnp.dot(tri, hi, preferred_element_type=F32)
            + jnp.dot(tri, mid, preferred_element_type=F32)
            + jnp.dot(tri, lo, preferred_element_type=F32))


def _gla_chunk(qk_ref, v_ref, sm_ref, wdec_ref, bdec_ref, st_ref, r0, keep, edge, emit):
    rows = pl.ds(r0, B_CHUNK)
    dec = sm_ref[0, rows, SM_DEC:SM_DEC + LANE]
    pre = jnp.dot(dec, wdec_ref[...], preferred_element_type=F32) + bdec_ref[...]
    la = _log_sigmoid(pre) * (1.0 / B_GATE_NORM)
    bcum = _exact_tri_sum(jnp.where(keep, 1.0, 0.0).astype(BF16), la)
    for h in range(B_HEADS):
        kc = slice(h * B_DK, (h + 1) * B_DK)
        vc = slice(h * B_DV, (h + 1) * B_DV)
        bh = bcum[:, kc]
        q = qk_ref[0, rows, kc].astype(F32)
        k = qk_ref[0, rows, B_HEADS * B_DK + h * B_DK:B_HEADS * B_DK + (h + 1) * B_DK].astype(F32)
        v = v_ref[0, rows, vc]
        b_last = bh[edge:edge + 1, :]
        qd = ((q * B_DK ** -0.5) * jnp.exp(bh)).astype(BF16)
        kinv = (k * jnp.exp(-bh)).astype(BF16)
        kdec = (k * jnp.exp(b_last - bh)).astype(BF16)
        s = lax.dot_general(qd, kinv, _NT, preferred_element_type=F32)
        s = jnp.where(keep, s, 0.0).astype(BF16)
        st = st_ref[h]
        o = (jnp.dot(s, v, preferred_element_type=F32)
             + lax.dot_general(qd, st.astype(BF16), _NT, preferred_element_type=F32))
        emit(h, o)
        kv_t = lax.dot_general(v, kdec, _TN, preferred_element_type=F32)
        st_ref[h] = jnp.exp(b_last) * st + kv_t


def _gla_fwd_kernel(qk_ref, v_ref, sm_ref, wdec_ref, bdec_ref, o_ref, st_ref, *, n_chunk):
    @pl.when(pl.program_id(1) == 0)
    def _():
        st_ref[...] = jnp.zeros_like(st_ref)

    ii = lax.broadcasted_iota(jnp.int32, (B_CHUNK, B_CHUNK), 0)
    jj = lax.broadcasted_iota(jnp.int32, (B_CHUNK, B_CHUNK), 1)
    keep = jj <= ii

    def chunk(c, carry):
        r0 = pl.multiple_of(c * B_CHUNK, B_CHUNK)

        def emit(h, o):
            o_ref[0, pl.ds(r0, B_CHUNK), h * B_DV:(h + 1) * B_DV] = o

        _gla_chunk(qk_ref, v_ref, sm_ref, wdec_ref, bdec_ref, st_ref, r0, keep, B_CHUNK - 1, emit)
        return carry

    lax.fori_loop(0, n_chunk, chunk, 0)


def _gla_bwd_kernel(qk_ref, v_ref, sm_ref, wdec_ref, bdec_ref, of_ref, r_ref, g_ref, y_ref, st_ref, *, n_chunk):
    @pl.when(pl.program_id(1) == 0)
    def _():
        st_ref[...] = jnp.zeros_like(st_ref)

    ii = lax.broadcasted_iota(jnp.int32, (B_CHUNK, B_CHUNK), 0)
    jj = lax.broadcasted_iota(jnp.int32, (B_CHUNK, B_CHUNK), 1)
    keep = jj >= ii

    def chunk(c, carry):
        r0 = pl.multiple_of((n_chunk - 1 - c) * B_CHUNK, B_CHUNK)
        rows = pl.ds(r0, B_CHUNK)

        def emit(h, o):
            vc = slice(h * B_DV, (h + 1) * B_DV)
            o = o + of_ref[0, rows, vc]
            on = _rms(o, B_DV) * g_ref[...]
            y_ref[0, rows, vc] = (on * _silu(r_ref[0, rows, vc].astype(F32))).astype(BF16)

        _gla_chunk(qk_ref, v_ref, sm_ref, wdec_ref, bdec_ref, st_ref, r0, keep, 0, emit)
        return carry

    lax.fori_loop(0, n_chunk, chunk, 0)


def _bwd_block(i):
    nb = T_ALL // TB_GLA
    return jnp.where(i == 0, 0, nb - i)


def _gla(z, wdec_f, bdec_f, wdec_b, bdec_b, g_out):
    tb = TB_GLA
    nb = T_ALL // tb
    n_chunk = tb // B_CHUNK
    wide = B_HEADS * B_DV

    def zspec(blk, imap):
        return pl.BlockSpec((1, tb, Z_BLK), lambda b, i: (b, imap(i), blk))

    def const2(shape):
        return pl.BlockSpec(shape, lambda b, i: (0, 0))

    state = pltpu.VMEM((B_HEADS, B_DV, B_DK), F32)
    fwd = lambda i: i
    o_f = pl.pallas_call(
        functools.partial(_gla_fwd_kernel, n_chunk=n_chunk),
        grid=(BATCH, nb),
        in_specs=[zspec(ZB_QK, fwd), zspec(ZB_VB, fwd), zspec(ZB_SMALL, fwd),
                  const2((LANE, B_HEADS * B_DK)), const2((1, B_HEADS * B_DK))],
        out_specs=pl.BlockSpec((1, tb, wide), lambda b, i: (b, i, 0)),
        out_shape=jax.ShapeDtypeStruct((BATCH, T_ALL, wide), F32),
        scratch_shapes=[state],
        compiler_params=_cparams(("parallel", "arbitrary")),
        name="gla_fwd",
    )(z, z, z, wdec_f, bdec_f)
    return pl.pallas_call(
        functools.partial(_gla_bwd_kernel, n_chunk=n_chunk),
        grid=(BATCH, nb),
        in_specs=[zspec(ZB_QK, _bwd_block), zspec(ZB_VB, _bwd_block), zspec(ZB_SMALL, _bwd_block),
                  const2((LANE, B_HEADS * B_DK)), const2((1, B_HEADS * B_DK)),
                  pl.BlockSpec((1, tb, wide), lambda b, i: (b, _bwd_block(i), 0)),
                  zspec(ZB_RB, _bwd_block), const2((1, B_DV))],
        out_specs=pl.BlockSpec((1, tb, wide), lambda b, i: (b, _bwd_block(i), 0)),
        out_shape=jax.ShapeDtypeStruct((BATCH, T_ALL, wide), BF16),
        scratch_shapes=[state],
        compiler_params=_cparams(("parallel", "arbitrary")),
        name="gla_bwd",
    )(z, z, z, wdec_b, bdec_b, o_f, z, g_out)


def _mla_prep_kernel(sm_ref, cs_ref, wq_ref, wkv_ref, gcq_ref, gckv_ref, gqn_ref, gqr_ref, gkn_ref, gkr_ref,
                     q_ref, k_ref, v_ref):
    scale = (C_NOPE + C_ROPE) ** -0.5
    cs = cs_ref[...]
    lane = lax.broadcasted_iota(jnp.int32, (1, LANE), 1)
    low_half = (lane < C_ROPE).astype(F32)

    def rope(pack, gain):
        t = _rms(pack, LANE) * gain * cs
        return t + pltpu.roll(t, C_ROPE, axis=1)

    cq = sm_ref[0, :, SM_CQ:SM_CQ + C_Q_RANK].astype(F32)
    cqn = (_rms(cq, C_Q_RANK) * gcq_ref[...]).astype(BF16)
    q_all = jnp.dot(cqn, wq_ref[...], preferred_element_type=F32)
    ckv = sm_ref[0, :, SM_CKV:SM_CKV + C_KV_RANK].astype(F32)
    ckvn = (_rms(ckv, C_KV_RANK) * gckv_ref[...]).astype(BF16)
    kv_all = jnp.dot(ckvn, wkv_ref[...], preferred_element_type=F32)
    k_rot = rope(sm_ref[0, :, SM_KR:SM_KR + LANE].astype(F32), gkr_ref[...]).astype(BF16)
    for h in range(C_HEADS):
        c0 = h * C_QK
        qn = _rms(q_all[:, c0:c0 + C_NOPE], C_NOPE) * gqn_ref[...]
        q_ref[0, h, :, 0:C_NOPE] = (qn * scale).astype(BF16)
        q_rot = rope(q_all[:, c0 + C_NOPE:c0 + C_QK], gqr_ref[...]) * low_half
        q_ref[0, h, :, C_NOPE:C_QK] = (q_rot * scale).astype(BF16)
        kn = _rms(kv_all[:, c0:c0 + C_NOPE], C_NOPE) * gkn_ref[...]
        k_ref[0, h, :, 0:C_NOPE] = kn.astype(BF16)
        k_ref[0, h, :, C_NOPE:C_QK] = k_rot
        v_ref[0, h] = kv_all[:, c0 + C_NOPE:c0 + C_QK].astype(BF16)


def _mla_prep(z, cs_tab, wq, wkv, gcq, gckv, gqn, gqr2, gkn, gkr2):
    tm = TM_ROWS

    def const2(shape):
        return pl.BlockSpec(shape, lambda b, i: (0, 0))

    hspec = lambda w: pl.BlockSpec((1, C_HEADS, tm, w), lambda b, i: (b, 0, i, 0))
    hshape = lambda w: jax.ShapeDtypeStruct((BATCH, C_HEADS, T_ALL, w), BF16)
    return pl.pallas_call(
        _mla_prep_kernel,
        grid=(BATCH, T_ALL // tm),
        in_specs=[
            pl.BlockSpec((1, tm, Z_BLK), lambda b, i: (b, i, ZB_SMALL)),
            pl.BlockSpec((tm, LANE), lambda b, i: (i, 0)),
            const2((C_Q_RANK, C_HEADS * C_QK)), const2((C_KV_RANK, C_HEADS * C_QK)),
            const2((1, C_Q_RANK)), const2((1, C_KV_RANK)),
            const2((1, C_NOPE)), const2((1, LANE)), const2((1, C_NOPE)), const2((1, LANE)),
        ],
        out_specs=[hspec(C_QK), hspec(C_QK), hspec(C_DV)],
        out_shape=[hshape(C_QK), hshape(C_QK), hshape(C_DV)],
        compiler_params=_cparams(("parallel", "parallel")),
        name="mla_prep",
    )(z, cs_tab, wq, wkv, gcq, gckv, gqn, gqr2, gkn, gkr2)


def _attn_kernel(q_ref, k_ref, v_ref, o_ref):
    def attend(n_keys):
        q = q_ref[0, 0]
        s = lax.dot_general(q, k_ref[0, 0, 0:n_keys, :], _NT, preferred_element_type=F32)
        p = jnp.exp(s - jnp.max(s, axis=-1, keepdims=True))
        o = jnp.dot(p.astype(BF16), v_ref[0, 0, 0:n_keys, :], preferred_element_type=F32)
        o_ref[0] = (o * (1.0 / jnp.sum(p, axis=-1, keepdims=True))).astype(BF16)

    @pl.when(pl.program_id(2) == 0)
    def _():
        attend(CTX_LEN)

    @pl.when(pl.program_id(2) > 0)
    def _():
        attend(T_ALL)


def _attention(q, k, v):
    tq = TQ
    return pl.pallas_call(
        _attn_kernel,
        grid=(BATCH, C_HEADS, T_ALL // tq),
        in_specs=[
            pl.BlockSpec((1, 1, tq, C_QK), lambda b, h, i: (b, h, i, 0)),
            pl.BlockSpec((1, 1, T_ALL, C_QK), lambda b, h, i: (b, h, 0, 0)),
            pl.BlockSpec((1, 1, T_ALL, C_DV), lambda b, h, i: (b, h, 0, 0)),
        ],
        out_specs=pl.BlockSpec((1, tq, C_DV), lambda b, h, i: (b, i, h)),
        out_shape=jax.ShapeDtypeStruct((BATCH, T_ALL, C_HEADS * C_DV), BF16),
        compiler_params=_cparams(("parallel", "parallel", "parallel")),
        name="mla_attention",
    )(q, k, v)


def _merge_kernel(ya_ref, yb_ref, yc_ref, gt_ref, bg_ref, wa_ref, wb_ref, wc_ref, wo_ref, x_ref, ml_ref, mc_ref,
                  gn_ref, wr_ref, x1_ref, h2a_ref, *, tm):
    is_ctx = _row_is_ctx(pl.program_id(1), tm)
    merged = None
    for j, (y_ref, w_ref) in enumerate(((ya_ref, wa_ref), (yb_ref, wb_ref), (yc_ref, wc_ref))):
        cols = slice(j * D_MODEL, (j + 1) * D_MODEL)
        gate = _sigmoid(gt_ref[0, :, cols].astype(F32) + bg_ref[:, cols])
        term = gate * jnp.dot(y_ref[0], w_ref[...], preferred_element_type=F32)
        merged = term if merged is None else merged + term
    y = jnp.dot(merged.astype(BF16), wo_ref[...], preferred_element_type=F32)
    x1 = x_ref[0] + _pick_mod(is_ctx, ml_ref, mc_ref, 2) * y
    x1_ref[0] = x1
    h2 = _rms(x1, D_MODEL) * gn_ref[...]
    h2 = h2 * (1.0 + _pick_mod(is_ctx, ml_ref, mc_ref, 4)) + _pick_mod(is_ctx, ml_ref, mc_ref, 3)
    logits = jnp.dot(h2.astype(BF16), wr_ref[...], preferred_element_type=F32)
    lane = lax.broadcasted_iota(jnp.int32, (1, LANE), 1)
    logits = jnp.where(lane < N_EXPERTS, logits, -jnp.inf)
    e = jnp.exp(logits - jnp.max(logits, axis=-1, keepdims=True))
    h2a_ref[0, :, 0:D_MODEL] = h2
    h2a_ref[0, :, D_MODEL:D_MODEL + LANE] = e / jnp.sum(e, axis=-1, keepdims=True)


def _merge(ya, yb, yc, z, b_gate, wa, wb, wc, wo, x, ml, mc, g_ffn, w_router):
    tm = TM_ROWS
    row = lambda w: pl.BlockSpec((1, tm, w), lambda b, i: (b, i, 0))

    def const2(shape):
        return pl.BlockSpec(shape, lambda b, i: (0, 0))

    sq = const2((D_MODEL, D_MODEL))
    return pl.pallas_call(
        functools.partial(_merge_kernel, tm=tm),
        grid=(BATCH, T_ALL // tm),
        in_specs=[
            row(D_MODEL), row(D_MODEL), row(D_MODEL), row(3 * D_MODEL), const2((1, 3 * D_MODEL)),
            sq, sq, sq, sq, row(D_MODEL),
            pl.BlockSpec((1, N_MOD, D_MODEL), lambda b, i: (b, 0, 0)),
            pl.BlockSpec((1, N_MOD, D_MODEL), lambda b, i: (0, 0, 0)),
            const2((1, D_MODEL)), const2((D_MODEL, LANE)),
        ],
        out_specs=[row(D_MODEL), row(D_MODEL + LANE)],
        out_shape=[jax.ShapeDtypeStruct((BATCH, T_ALL, D_MODEL), F32),
                   jax.ShapeDtypeStruct((BATCH, T_ALL, D_MODEL + LANE), F32)],
        compiler_params=_cparams(("parallel", "parallel")),
        name="merge",
    )(ya, yb, yc, z, b_gate, wa, wb, wc, wo, x, ml, mc, g_ffn, w_router)


N_ROUTE_ROWS = BATCH * N_EXPERTS
N_BISECT = 150


def _count(mask):
    return jnp.sum(jnp.where(mask, 1.0, 0.0), axis=1, keepdims=True)


def _lane_cumsum_blocks(x01):
    r, n = x01.shape
    nblk = n // LANE
    x3 = jnp.concatenate([x01[:, j * LANE:(j + 1) * LANE] for j in range(nblk)], axis=0).astype(BF16)
    ii = lax.broadcasted_iota(jnp.int32, (LANE, LANE), 0)
    jj = lax.broadcasted_iota(jnp.int32, (LANE, LANE), 1)
    inc = jnp.dot(x3, jnp.where(ii <= jj, 1.0, 0.0).astype(BF16), preferred_element_type=F32)
    off = jnp.zeros((r, 1), F32)
    out = []
    for j in range(nblk):
        blk = inc[j * r:(j + 1) * r]
        out.append(blk + off)
        off = off + blk[:, LANE - 1:LANE]
    return out


def _select_top(a, cap):
    def bisect(_, c):
        thr, step = c
        cand = thr + step
        return jnp.where(_count(a >= cand) >= cap, cand, thr), step * 0.5

    thr, _ = lax.fori_loop(0, N_BISECT, bisect, (jnp.zeros((a.shape[0], 1), F32), jnp.ones((a.shape[0], 1), F32)))
    gt = a > thr
    eq = a == thr
    need = cap - _count(gt)
    eq_rank = jnp.concatenate(_lane_cumsum_blocks(jnp.where(eq, 1.0, 0.0)), axis=1)
    sel = gt | (eq & (eq_rank <= need))
    return _lane_cumsum_blocks(jnp.where(sel, 1.0, 0.0))


def _route_kernel(aff_ref, idx_ref, cum_ref, end_ref, *, caps):
    aff_t = jnp.concatenate([aff_ref[b].T[:N_EXPERTS] for b in range(BATCH)], axis=0)
    lane = lax.broadcasted_iota(jnp.int32, (1, LANE), 1)
    p0 = 0
    for (t_lo, t_hi, cap) in caps:
        nblk = (t_hi - t_lo) // LANE
        blocks = _select_top(aff_t[:, t_lo:t_hi], cap)
        ends = jnp.full((N_ROUTE_ROWS, LANE), float(cap), F32)
        for j, blk in enumerate(blocks):
            cum_ref[:, j, :] = blk
            ends = jnp.where(lane == j, blk[:, LANE - 1:LANE], ends)
        end_ref[...] = ends
        p_col = lax.broadcasted_iota(jnp.int32, (cap, 1), 0).astype(F32)

        def per_row(r, res):
            n_full = _count(end_ref[pl.ds(r, 1), :] <= p_col)
            cum = cum_ref[r][0:nblk]
            cum = jnp.concatenate([cum, jnp.zeros((LANE - nblk, LANE), F32)], axis=0)
            hi = jnp.floor(cum * (1.0 / 64.0))
            lo = cum - 64.0 * hi
            pick = jnp.where(lane.astype(F32) == n_full, 1.0, 0.0).astype(BF16)
            part = (64.0 * jnp.dot(pick, hi.astype(BF16), preferred_element_type=F32)
                    + jnp.dot(pick, lo.astype(BF16), preferred_element_type=F32))
            cnt = LANE * n_full + _count(part <= p_col)
            return jnp.where(lane == r, cnt, res)

        res = lax.fori_loop(0, N_ROUTE_ROWS, per_row, jnp.zeros((cap, LANE), F32))
        if cap % LANE:
            res = jnp.concatenate([res, jnp.zeros((LANE - cap % LANE, LANE), F32)], axis=0)
        idx_ref[:, p0:p0 + cap] = (res.T[:N_ROUTE_ROWS, :cap] + float(t_lo)).astype(jnp.int32)
        p0 += cap


def _route(h2a, caps):
    n_sel = sum(c[2] for c in caps)
    return pl.pallas_call(
        functools.partial(_route_kernel, caps=caps),
        grid=(1,),
        in_specs=[pl.BlockSpec((BATCH, T_ALL, LANE), lambda i: (0, 0, D_MODEL // LANE))],
        out_specs=pl.BlockSpec((N_ROUTE_ROWS, n_sel), lambda i: (0, 0)),
        out_shape=jax.ShapeDtypeStruct((N_ROUTE_ROWS, n_sel), jnp.int32),
        scratch_shapes=[pltpu.VMEM((N_ROUTE_ROWS, SEQ // LANE, LANE), F32), pltpu.VMEM((N_ROUTE_ROWS, LANE), F32)],
        compiler_params=_cparams(("arbitrary",)),
        name="moe_route",
    )(h2a)


def _row_gather_copy(h2a_hbm, buf, sem, b, t, p):
    return pltpu.make_async_copy(h2a_hbm.at[b, pl.ds(t, 1), :], buf.at[pl.ds(p, 1), :], sem)


def _gather_kernel(idx_ref, h2a_hbm, xs_ref, g_ref, buf, sem, *, n_sel):
    b = pl.program_id(0)
    e = pl.program_id(1)

    def issue(p, carry):
        _row_gather_copy(h2a_hbm, buf, sem, b, idx_ref[0, 0, 0, p], p).start()
        return carry

    lax.fori_loop(0, n_sel, issue, 0, unroll=8)

    def drain(p, carry):
        _row_gather_copy(h2a_hbm, buf, sem, b, 0, p).wait()
        return carry

    lax.fori_loop(0, n_sel, drain, 0, unroll=8)
    xs_ref[0] = buf[:, 0:D_MODEL].astype(BF16)
    lane = lax.broadcasted_iota(jnp.int32, (1, LANE), 1)
    g_ref[0] = jnp.sum(jnp.where(lane == e, buf[:, D_MODEL:D_MODEL + LANE], 0.0), axis=1, keepdims=True)


def _gather(idx4, h2a):
    n_sel = idx4.shape[-1]
    return pl.pallas_call(
        functools.partial(_gather_kernel, n_sel=n_sel),
        grid=(BATCH, N_EXPERTS),
        in_specs=[pl.BlockSpec((1, 1, 1, n_sel), lambda b, e: (b, e, 0, 0), memory_space=pltpu.SMEM),
                  pl.BlockSpec(memory_space=pl.ANY)],
        out_specs=[pl.BlockSpec((1, n_sel, D_MODEL), lambda b, e: (e, b, 0)),
                   pl.BlockSpec((1, n_sel, 1), lambda b, e: (e, b, 0))],
        out_shape=[jax.ShapeDtypeStruct((N_EXPERTS, BATCH * n_sel, D_MODEL), BF16),
                   jax.ShapeDtypeStruct((N_EXPERTS, BATCH * n_sel, 1), F32)],
        scratch_shapes=[pltpu.VMEM((n_sel, D_MODEL + LANE), F32), pltpu.SemaphoreType.DMA(())],
        compiler_params=_cparams(("arbitrary", "arbitrary")),
        name="moe_gather",
    )(idx4, h2a)


def _moe_kernel(x_ref, g_ref, wg_ref, wu_ref, wd_ref, o_ref):
    x = x_ref[0]
    hid = (_silu(jnp.dot(x, wg_ref[0, 0], preferred_element_type=F32))
           * jnp.dot(x, wu_ref[0, 0], preferred_element_type=F32))
    o_ref[0] = jnp.dot(hid.astype(BF16), wd_ref[0, 0], preferred_element_type=F32) * g_ref[0]


def _moe_ffn(layer, xs, gs, wg, wu, wd):
    n_rows = xs.shape[1]
    tr = n_rows // 4
    wspec = lambda a, b: pl.BlockSpec((1, 1, a, b), lambda e, r: (layer, e, 0, 0))
    return pl.pallas_call(
        _moe_kernel,
        grid=(N_EXPERTS, n_rows // tr),
        in_specs=[
            pl.BlockSpec((1, tr, D_MODEL), lambda e, r: (e, r, 0)),
            pl.BlockSpec((1, tr, 1), lambda e, r: (e, r, 0)),
            wspec(D_MODEL, D_EXPERT), wspec(D_MODEL, D_EXPERT), wspec(D_EXPERT, D_MODEL),
        ],
        out_specs=pl.BlockSpec((1, tr, D_MODEL), lambda e, r: (e, r, 0)),
        out_shape=jax.ShapeDtypeStruct((N_EXPERTS, n_rows, D_MODEL), F32),
        compiler_params=_cparams(("parallel", "parallel")),
        name="moe_ffn",
    )(xs, gs, wg, wu, wd)


def _combine_kernel(idx_ref, ys_ref, x1_ref, ml_ref, mc_ref, o_ref, *, th, n_lat, n_sel):
    lo = pl.program_id(1) * th

    @pl.when(pl.program_id(2) == 0)
    def _():
        o_ref[...] = x1_ref[...]

    def first_at_least(bound):
        def step(_, c):
            l, h = c
            m = jnp.minimum((l + h) // 2, n_lat - 1)
            go = idx_ref[0, 0, 0, m] < bound
            return jnp.where((l < h) & go, m + 1, l), jnp.where((l < h) & jnp.logical_not(go), m, h)

        return lax.fori_loop(0, n_lat.bit_length(), step, (jnp.int32(0), jnp.int32(n_lat)))[0]

    def add_rows(p0, p1, gate):
        def body(p, carry):
            r = idx_ref[0, 0, 0, p] - lo
            o_ref[0, pl.ds(r, 1), :] = o_ref[0, pl.ds(r, 1), :] + gate * ys_ref[0, pl.ds(p, 1), :]
            return carry

        lax.fori_loop(p0, p1, body, 0)

    add_rows(first_at_least(jnp.maximum(lo, CTX_LEN)), first_at_least(lo + th), ml_ref[0, 5:6, :])
    if n_sel > n_lat:
        @pl.when(lo == 0)
        def _():
            add_rows(n_lat, n_sel, mc_ref[0, 5:6, :])


def _combine(idx4, ys, x1, ml, mc, n_lat):
    n_sel = idx4.shape[-1]
    th = T_ALL // 2
    return pl.pallas_call(
        functools.partial(_combine_kernel, th=th, n_lat=n_lat, n_sel=n_sel),
        grid=(BATCH, T_ALL // th, N_EXPERTS),
        in_specs=[pl.BlockSpec((1, 1, 1, n_sel), lambda b, t, e: (b, e, 0, 0), memory_space=pltpu.SMEM),
                  pl.BlockSpec((1, n_sel, D_MODEL), lambda b, t, e: (e, b, 0)),
                  pl.BlockSpec((1, th, D_MODEL), lambda b, t, e: (b, t, 0)),
                  pl.BlockSpec((1, N_MOD, D_MODEL), lambda b, t, e: (b, 0, 0)),
                  pl.BlockSpec((1, N_MOD, D_MODEL), lambda b, t, e: (0, 0, 0))],
        out_specs=pl.BlockSpec((1, th, D_MODEL), lambda b, t, e: (b, t, 0)),
        out_shape=jax.ShapeDtypeStruct((BATCH, T_ALL, D_MODEL), F32),
        compiler_params=_cparams(("parallel", "parallel", "arbitrary")),
        name="moe_combine",
    )(idx4, ys, x1, ml, mc)


def _rope_swap_perm():
    idx = np.arange(C_ROPE, dtype=np.int32).reshape(2, 2, C_ROPE // 4)
    return idx[:, ::-1, :].reshape(-1)


def _rope_table():
    rows = SEQ // GRID_W
    row = jnp.broadcast_to(jnp.arange(rows, dtype=F32)[:, None], (rows, GRID_W)).reshape(-1)
    col = jnp.broadcast_to(jnp.arange(GRID_W, dtype=F32)[None, :], (rows, GRID_W)).reshape(-1)
    axis_dim = C_ROPE // 2
    inv_freq = ROPE_BASE ** (-jnp.arange(0, axis_dim, 2, dtype=F32) / axis_dim)
    n_freq = C_ROPE // 4
    ang = jnp.stack([row[:, None] * inv_freq, col[:, None] * inv_freq], axis=1)
    cos = jnp.broadcast_to(jnp.cos(ang)[:, :, None, :], (SEQ, 2, 2, n_freq)).reshape(SEQ, C_ROPE)
    sin = jnp.sin(ang)
    sin = jnp.stack([-sin, sin], axis=2).reshape(SEQ, C_ROPE)
    lat = jnp.concatenate([cos, sin], axis=1)
    ctx = jnp.concatenate([jnp.ones((CTX_LEN, C_ROPE), F32), jnp.zeros((CTX_LEN, C_ROPE), F32)], axis=1)
    return jnp.concatenate([ctx, lat], axis=0)


def _layer_weights(i, w_in, w_gla_decay, b_gla_decay, w_uq, g_qr, g_kr, b_s):
    perm = _rope_swap_perm()
    o = [int(v) for v in np.cumsum((0, 1024, 1024, 512, 512, 1024, 1024, 32, C_Q_RANK, C_KV_RANK, C_ROPE, 3 * D_MODEL))]
    w = w_in[i]
    seg = lambda j: w[:, o[j]:o[j + 1]]
    u, va, qb, kb, vb, rb, dec, cq, ckv, kr, gates = (seg(j) for j in range(11))
    pad = jnp.zeros((D_MODEL, Z_BLK - (SM_DEC + 2 * B_GATE_RANK)), F32)
    w_in_ext = jnp.concatenate([gates, u, va, vb, rb, qb, kb, cq, ckv, kr, kr[:, perm], dec, pad], axis=1).astype(BF16)

    def dec_weight(j):
        wd = jnp.zeros((LANE, B_HEADS * B_DK), F32)
        return wd.at[j * B_GATE_RANK:(j + 1) * B_GATE_RANK].set(w_gla_decay[i, j]).astype(BF16)

    wq = w_uq[i].reshape(C_Q_RANK, C_HEADS, C_NOPE + C_ROPE)
    wq_ext = jnp.concatenate([wq, wq[:, :, C_NOPE:][:, :, perm]], axis=2).reshape(C_Q_RANK, C_HEADS * C_QK).astype(BF16)
    pack = lambda g: jnp.concatenate([g[i], g[i][perm]])[None, :]
    bs_full = jnp.repeat(b_s[i].T, A_CHUNK, axis=1)
    return dict(w_in=w_in_ext, wdec_f=dec_weight(0), wdec_b=dec_weight(1),
                bdec_f=b_gla_decay[i, 0][None, :], bdec_b=b_gla_decay[i, 1][None, :],
                wq=wq_ext, gqr2=pack(g_qr), gkr2=pack(g_kr), bs_full=bs_full)


def kernel(x, c, ctx, c_ctx, w_mod, b_mod, g_norm_mix, g_norm_ffn, w_in, b_gate, w_s, b_s, w_gla_decay, b_gla_decay,
           g_gla_out, g_cq, g_ckv, w_uq, w_ukv, g_qn, g_qr, g_kn, g_kr, w_proj_a, w_proj_b, w_proj_c, w_out,
           w_router, w_e_gate, w_e_up, w_e_down):
    c8 = jnp.concatenate([c, c_ctx[None, :], jnp.zeros((8 - BATCH - 1, D_MODEL), F32)], axis=0)
    mod = _mod_table(c8, w_mod, b_mod).reshape(DEPTH, 8, N_MOD, D_MODEL)
    cs_tab = _rope_table()
    xc = jnp.concatenate([ctx, x], axis=1)
    row2 = lambda a: a[None, :]
    wg_all, wu_all, wd_all = w_e_gate.astype(BF16), w_e_up.astype(BF16), w_e_down.astype(BF16)
    cap_lat = CAPACITY_FACTOR * SEQ // N_EXPERTS
    cap_ctx = CAPACITY_FACTOR * CTX_LEN // N_EXPERTS
    for i in range(DEPTH):
        lw = _layer_weights(i, w_in, w_gla_decay, b_gla_decay, w_uq, g_qr, g_kr, b_s)
        ml, mc = mod[i, :BATCH], mod[i, BATCH:BATCH + 1]
        h = _mod_norm(xc, row2(g_norm_mix[i]), ml, mc, 0, 1)
        z = _in_proj(h.reshape(BATCH * T_ALL, D_MODEL), lw['w_in']).reshape(BATCH, T_ALL, N_Z)
        ya = _mix_a(z, w_s[i].astype(BF16), lw['bs_full'])
        yb = _gla(z, lw['wdec_f'], lw['bdec_f'], lw['wdec_b'], lw['bdec_b'], row2(g_gla_out[i]))
        q, k, v = _mla_prep(z, cs_tab, lw['wq'], w_ukv[i].astype(BF16), row2(g_cq[i]), row2(g_ckv[i]),
                            row2(g_qn[i]), lw['gqr2'], row2(g_kn[i]), lw['gkr2'])
        yc = _attention(q, k, v)
        w_r = jnp.concatenate([w_router[i], jnp.zeros((D_MODEL, LANE - N_EXPERTS), F32)], axis=1).astype(BF16)
        x1, h2a = _merge(ya, yb, yc, z, row2(b_gate[i]), w_proj_a[i].astype(BF16), w_proj_b[i].astype(BF16),
                         w_proj_c[i].astype(BF16), w_out[i].astype(BF16), xc, ml, mc, row2(g_norm_ffn[i]), w_r)
        caps = ((CTX_LEN, T_ALL, cap_lat),) + (((0, CTX_LEN, cap_ctx),) if i < DEPTH - 1 else ())
        idx = _route(h2a, caps)
        idx4 = idx.reshape(BATCH, N_EXPERTS, 1, idx.shape[-1])
        xs, gs = _gather(idx4, h2a)
        ys = _moe_ffn(i, xs, gs, wg_all, wu_all, wd_all)
        xc = _combine(idx4, ys, x1, ml, mc, cap_lat)
    return xc[:, CTX_LEN:]
```

```python
import functools

import jax
import jax.numpy as jnp
import numpy as np
from jax import lax
from jax.experimental import pallas as pl
from jax.experimental.pallas import tpu as pltpu

F32 = jnp.float32
BF16 = jnp.bfloat16

D_MODEL = 1024
BATCH = 4
SEQ = 4096
DEPTH = 2
CTX_LEN = 256
T_ALL = CTX_LEN + SEQ
GRID_W = 64
EPS = 1e-6
N_MOD = 6

A_GROUPS = 8
A_CHUNK = 128

B_HEADS = 4
B_DK = 128
B_DV = 256
B_GATE_RANK = 16
B_GATE_NORM = 16.0
B_CHUNK = 64

C_HEADS = 8
C_Q_RANK = 384
C_KV_RANK = 256
C_NOPE = 128
C_ROPE = 64
C_DV = 128
ROPE_BASE = 10000.0
C_QK = 2 * C_NOPE

N_EXPERTS = 16
CAPACITY_FACTOR = 2
D_EXPERT = 2048

Z_BLK = 1024
ZB_GATES, ZB_U, ZB_VA, ZB_VB, ZB_RB, ZB_QK, ZB_SMALL = 0, 3, 4, 5, 6, 7, 8
N_Z = 9 * Z_BLK
SM_CQ, SM_CKV, SM_KR, SM_DEC = 0, C_Q_RANK, C_Q_RANK + C_KV_RANK, C_Q_RANK + C_KV_RANK + 2 * C_ROPE

LANE = 128
VMEM_LIMIT = 56 * 1024 * 1024

TM_ROWS = 544
TM_MM = 1088
TM_PREP = 256
TQ = 256
HEADS_PER_STEP = 2
VT_ROWS = C_DV + 16
TB_GLA = 256


def _cparams(sem):
    return pltpu.CompilerParams(dimension_semantics=sem, vmem_limit_bytes=VMEM_LIMIT)


def _rms(x, n):
    return x * lax.rsqrt(jnp.sum(x * x, axis=-1, keepdims=True) * (1.0 / n) + EPS)


def _silu(x):
    return x * (1.0 / (1.0 + jnp.exp(-x)))


def _sigmoid(x):
    return 1.0 / (1.0 + jnp.exp(-x))


def _row_is_ctx(tile_idx, tm):
    row = tile_idx * tm + lax.broadcasted_iota(jnp.int32, (tm, 1), 0)
    return row < CTX_LEN


def _pick_mod(is_ctx, ml_ref, mc_ref, k):
    return jnp.where(is_ctx, mc_ref[0, k:k + 1, :], ml_ref[0, k:k + 1, :])


def _mod_kernel(c_ref, w_ref, b_ref, o_ref):
    s = _silu(c_ref[...])
    o_ref[0] = jnp.dot(s.astype(BF16), w_ref[0].astype(BF16), preferred_element_type=F32) + b_ref[0]


def _mod_table(c8, w_mod, b_mod):
    tn = 512
    n = N_MOD * D_MODEL
    return pl.pallas_call(
        _mod_kernel,
        grid=(DEPTH, n // tn),
        in_specs=[
            pl.BlockSpec((8, D_MODEL), lambda l, j: (0, 0)),
            pl.BlockSpec((1, D_MODEL, tn), lambda l, j: (l, 0, j)),
            pl.BlockSpec((1, 1, tn), lambda l, j: (l, 0, j)),
        ],
        out_specs=pl.BlockSpec((1, 8, tn), lambda l, j: (l, 0, j)),
        out_shape=jax.ShapeDtypeStruct((DEPTH, 8, n), F32),
        compiler_params=_cparams(("parallel", "parallel")),
        name="mod_table",
    )(c8, w_mod, b_mod.reshape(DEPTH, 1, n))


def _norm_kernel(x_ref, g_ref, ml_ref, mc_ref, o_ref, *, tm, k_shift, k_scale):
    is_ctx = _row_is_ctx(pl.program_id(1), tm)
    y = _rms(x_ref[0], D_MODEL) * g_ref[...]
    y = y * (1.0 + _pick_mod(is_ctx, ml_ref, mc_ref, k_scale)) + _pick_mod(is_ctx, ml_ref, mc_ref, k_shift)
    o_ref[0] = y.astype(BF16)


def _mod_norm(x, g, ml, mc, k_shift, k_scale):
    tm = TM_ROWS
    return pl.pallas_call(
        functools.partial(_norm_kernel, tm=tm, k_shift=k_shift, k_scale=k_scale),
        grid=(BATCH, T_ALL // tm),
        in_specs=[
            pl.BlockSpec((1, tm, D_MODEL), lambda b, i: (b, i, 0)),
            pl.BlockSpec((1, D_MODEL), lambda b, i: (0, 0)),
            pl.BlockSpec((1, N_MOD, D_MODEL), lambda b, i: (b, 0, 0)),
            pl.BlockSpec((1, N_MOD, D_MODEL), lambda b, i: (0, 0, 0)),
        ],
        out_specs=pl.BlockSpec((1, tm, D_MODEL), lambda b, i: (b, i, 0)),
        out_shape=jax.ShapeDtypeStruct((BATCH, T_ALL, D_MODEL), BF16),
        compiler_params=_cparams(("parallel", "parallel")),
        name="mod_norm",
    )(x, g, ml, mc)


def _matmul_kernel(a_ref, w_ref, o_ref):
    o_ref[...] = jnp.dot(a_ref[...], w_ref[...], preferred_element_type=F32).astype(o_ref.dtype)


def _in_proj(h2d, w):
    m, k = h2d.shape
    n = w.shape[1]
    tm, tn = TM_MM, Z_BLK
    return pl.pallas_call(
        _matmul_kernel,
        grid=(m // tm, n // tn),
        in_specs=[
            pl.BlockSpec((tm, k), lambda i, j: (i, 0)),
            pl.BlockSpec((k, tn), lambda i, j: (0, j)),
        ],
        out_specs=pl.BlockSpec((tm, tn), lambda i, j: (i, j)),
        out_shape=jax.ShapeDtypeStruct((m, n), BF16),
        compiler_params=_cparams(("parallel", "parallel")),
        name="in_proj",
    )(h2d, w)


def _mix_a_kernel(u_ref, v_ref, ws_ref, bs_ref, o_ref, *, n_chunk):
    def chunk(c, carry):
        r0 = pl.multiple_of(c * A_CHUNK, A_CHUNK)
        for g in range(A_GROUPS):
            cols = slice(g * LANE, (g + 1) * LANE)
            v = v_ref[0, pl.ds(r0, A_CHUNK), cols].astype(F32)
            mu = jnp.mean(v, axis=-1, keepdims=True)
            d = v - mu
            ln = d * lax.rsqrt(jnp.mean(d * d, axis=-1, keepdims=True) + EPS)
            mixed = jnp.dot(ws_ref[g], ln.astype(BF16), preferred_element_type=F32) + bs_ref[:, cols]
            u = u_ref[0, pl.ds(r0, A_CHUNK), cols].astype(F32)
            o_ref[0, pl.ds(r0, A_CHUNK), cols] = (u * mixed).astype(BF16)
        return carry

    lax.fori_loop(0, n_chunk, chunk, 0)


def _mix_a(z, ws, bs_full):
    ta = T_ALL // 2
    return pl.pallas_call(
        functools.partial(_mix_a_kernel, n_chunk=ta // A_CHUNK),
        grid=(BATCH, T_ALL // ta),
        in_specs=[
            pl.BlockSpec((1, ta, Z_BLK), lambda b, i: (b, i, ZB_U)),
            pl.BlockSpec((1, ta, Z_BLK), lambda b, i: (b, i, ZB_VA)),
            pl.BlockSpec((A_GROUPS, A_CHUNK, A_CHUNK), lambda b, i: (0, 0, 0)),
            pl.BlockSpec((A_CHUNK, D_MODEL), lambda b, i: (0, 0)),
        ],
        out_specs=pl.BlockSpec((1, ta, D_MODEL), lambda b, i: (b, i, 0)),
        out_shape=jax.ShapeDtypeStruct((BATCH, T_ALL, D_MODEL), BF16),
        compiler_params=_cparams(("parallel", "parallel")),
        name="mix_a",
    )(z, z, ws, bs_full)


_NT = (((1,), (1,)), ((), ()))
_TN = (((0,), (0,)), ((), ()))


def _log_sigmoid(x):
    return jnp.minimum(x, 0.0) - jnp.log(1.0 + jnp.exp(-jnp.abs(x)))


def _exact_tri_sum(tri, la):
    hi = la.astype(BF16)
    r1 = la - hi.astype(F32)
    mid = r1.astype(BF16)
    lo = (r1 - mid.astype(F32)).astype(BF16)
    return (jnp.dot(tri, hi, preferred_element_type=F32)
            + jnp.dot(tri, mid, preferred_element_type=F32)
            + jnp.dot(tri, lo, preferred_element_type=F32))


def _gla_block(qk_ref, v_ref, sm_ref, wdec_ref, bdec_ref, st_ref, *, reverse, emit):
    n_chunk = TB_GLA // B_CHUNK
    shift = B_CHUNK.bit_length() - 1
    ii = lax.broadcasted_iota(jnp.int32, (TB_GLA, TB_GLA), 0)
    jj = lax.broadcasted_iota(jnp.int32, (TB_GLA, TB_GLA), 1)
    keep = ((ii >> shift) == (jj >> shift)) & ((jj >= ii) if reverse else (jj <= ii))
    edge = 0 if reverse else B_CHUNK - 1

    pre = jnp.dot(sm_ref[0, :, SM_DEC:SM_DEC + LANE], wdec_ref[...], preferred_element_type=F32) + bdec_ref[...]
    la = _log_sigmoid(pre) * (1.0 / B_GATE_NORM)
    bcum = _exact_tri_sum(jnp.where(keep, 1.0, 0.0).astype(BF16), la)
    blast = jnp.concatenate(
        [jnp.broadcast_to(bcum[c * B_CHUNK + edge:c * B_CHUNK + edge + 1], (B_CHUNK, B_HEADS * B_DK))
         for c in range(n_chunk)], axis=0)
    e_fwd, e_inv, e_rest = jnp.exp(bcum), jnp.exp(-bcum), jnp.exp(blast - bcum)
    e_tot = jnp.exp(blast)
    for h in range(B_HEADS):
        kc = slice(h * B_DK, (h + 1) * B_DK)
        vc = slice(h * B_DV, (h + 1) * B_DV)
        q = qk_ref[0, :, kc].astype(F32)
        k = qk_ref[0, :, B_HEADS * B_DK + h * B_DK:B_HEADS * B_DK + (h + 1) * B_DK].astype(F32)
        v = v_ref[0, :, vc]
        qd = ((q * B_DK ** -0.5) * e_fwd[:, kc]).astype(BF16)
        kinv = (k * e_inv[:, kc]).astype(BF16)
        kdec = (k * e_rest[:, kc]).astype(BF16)
        s = lax.dot_general(qd, kinv, _NT, preferred_element_type=F32)
        o_intra = jnp.dot(jnp.where(keep, s, 0.0).astype(BF16), v, preferred_element_type=F32)
        st = st_ref[h]
        for c in (range(n_chunk - 1, -1, -1) if reverse else range(n_chunk)):
            rows = slice(c * B_CHUNK, (c + 1) * B_CHUNK)
            emit(h, rows, o_intra[rows] + lax.dot_general(qd[rows], st.astype(BF16), _NT, preferred_element_type=F32))
            kv_t = lax.dot_general(v[rows], kdec[rows], _TN, preferred_element_type=F32)
            st = e_tot[c * B_CHUNK:c * B_CHUNK + 1, kc] * st + kv_t
        st_ref[h] = st


def _gla_fwd_kernel(qk_ref, v_ref, sm_ref, wdec_ref, bdec_ref, o_ref, st_ref):
    @pl.when(pl.program_id(1) == 0)
    def _():
        st_ref[...] = jnp.zeros_like(st_ref)

    def emit(h, rows, o):
        o_ref[0, rows, h * B_DV:(h + 1) * B_DV] = o

    _gla_block(qk_ref, v_ref, sm_ref, wdec_ref, bdec_ref, st_ref, reverse=False, emit=emit)


def _gla_bwd_kernel(qk_ref, v_ref, sm_ref, wdec_ref, bdec_ref, of_ref, r_ref, g_ref, y_ref, st_ref):
    @pl.when(pl.program_id(1) == 0)
    def _():
        st_ref[...] = jnp.zeros_like(st_ref)

    def emit(h, rows, o):
        vc = slice(h * B_DV, (h + 1) * B_DV)
        o = o + of_ref[0, rows, vc]
        on = _rms(o, B_DV) * g_ref[...]
        y_ref[0, rows, vc] = (on * _silu(r_ref[0, rows, vc].astype(F32))).astype(BF16)

    _gla_block(qk_ref, v_ref, sm_ref, wdec_ref, bdec_ref, st_ref, reverse=True, emit=emit)


def _bwd_block(i):
    nb = T_ALL // TB_GLA
    return jnp.where(i == 0, 0, nb - i)


def _gla(z, wdec_f, bdec_f, wdec_b, bdec_b, g_out):
    tb = TB_GLA
    nb = T_ALL // tb
    wide = B_HEADS * B_DV

    def zspec(blk, imap):
        return pl.BlockSpec((1, tb, Z_BLK), lambda b, i: (b, imap(i), blk))

    def const2(shape):
        return pl.BlockSpec(shape, lambda b, i: (0, 0))

    state = pltpu.VMEM((B_HEADS, B_DV, B_DK), F32)
    fwd = lambda i: i
    o_f = pl.pallas_call(
        _gla_fwd_kernel,
        grid=(BATCH, nb),
        in_specs=[zspec(ZB_QK, fwd), zspec(ZB_VB, fwd), zspec(ZB_SMALL, fwd),
                  const2((LANE, B_HEADS * B_DK)), const2((1, B_HEADS * B_DK))],
        out_specs=pl.BlockSpec((1, tb, wide), lambda b, i: (b, i, 0)),
        out_shape=jax.ShapeDtypeStruct((BATCH, T_ALL, wide), F32),
        scratch_shapes=[state],
        compiler_params=_cparams(("parallel", "arbitrary")),
        name="gla_fwd",
    )(z, z, z, wdec_f, bdec_f)
    return pl.pallas_call(
        _gla_bwd_kernel,
        grid=(BATCH, nb),
        in_specs=[zspec(ZB_QK, _bwd_block), zspec(ZB_VB, _bwd_block), zspec(ZB_SMALL, _bwd_block),
                  const2((LANE, B_HEADS * B_DK)), const2((1, B_HEADS * B_DK)),
                  pl.BlockSpec((1, tb, wide), lambda b, i: (b, _bwd_block(i), 0)),
                  zspec(ZB_RB, _bwd_block), const2((1, B_DV))],
        out_specs=pl.BlockSpec((1, tb, wide), lambda b, i: (b, _bwd_block(i), 0)),
        out_shape=jax.ShapeDtypeStruct((BATCH, T_ALL, wide), BF16),
        scratch_shapes=[state],
        compiler_params=_cparams(("parallel", "arbitrary")),
        name="gla_bwd",
    )(z, z, z, wdec_b, bdec_b, o_f, z, g_out)


def _mla_prep_kernel(sm_ref, cs_ref, wq_ref, wkn_ref, wvt_ref, gcq_ref, gckv_ref, gqn_ref, gqr_ref, gkn_ref, gkr_ref,
                     q_ref, k_ref, vt_ref):
    scale = (C_NOPE + C_ROPE) ** -0.5
    vt_row = lax.broadcasted_iota(jnp.int32, (VT_ROWS, 1), 0)
    cs = cs_ref[...]
    lane = lax.broadcasted_iota(jnp.int32, (1, LANE), 1)
    low_half = (lane < C_ROPE).astype(F32)

    def rope(pack, gain):
        t = _rms(pack, LANE) * gain * cs
        return t + pltpu.roll(t, C_ROPE, axis=1)

    cq = sm_ref[0, :, SM_CQ:SM_CQ + C_Q_RANK].astype(F32)
    cqn = (_rms(cq, C_Q_RANK) * gcq_ref[...]).astype(BF16)
    q_all = jnp.dot(cqn, wq_ref[...], preferred_element_type=F32)
    ckv = sm_ref[0, :, SM_CKV:SM_CKV + C_KV_RANK].astype(F32)
    ckvn = (_rms(ckv, C_KV_RANK) * gckv_ref[...]).astype(BF16)
    kn_all = jnp.dot(ckvn, wkn_ref[...], preferred_element_type=F32)
    k_rot = rope(sm_ref[0, :, SM_KR:SM_KR + LANE].astype(F32), gkr_ref[...]).astype(BF16)
    for h in range(C_HEADS):
        c0 = h * C_QK
        qn = _rms(q_all[:, c0:c0 + C_NOPE], C_NOPE) * gqn_ref[...]
        q_ref[0, h, :, 0:C_NOPE] = (qn * scale).astype(BF16)
        q_rot = rope(q_all[:, c0 + C_NOPE:c0 + C_QK], gqr_ref[...]) * low_half
        q_ref[0, h, :, C_NOPE:C_QK] = (q_rot * scale).astype(BF16)
        kn = _rms(kn_all[:, h * C_NOPE:(h + 1) * C_NOPE], C_NOPE) * gkn_ref[...]
        k_ref[0, h, :, 0:C_NOPE] = kn.astype(BF16)
        k_ref[0, h, :, C_NOPE:C_QK] = k_rot
        vt = lax.dot_general(wvt_ref[h], ckvn, _NT, preferred_element_type=F32)
        vt_ref[0, h] = jnp.where(vt_row == C_DV, 1.0, vt).astype(BF16)


def _mla_prep(z, cs_tab, wq, wkn, wvt, gcq, gckv, gqn, gqr2, gkn, gkr2):
    tm = TM_PREP

    def const2(shape):
        return pl.BlockSpec(shape, lambda b, i: (0, 0))

    hspec = lambda w: pl.BlockSpec((1, C_HEADS, tm, w), lambda b, i: (b, 0, i, 0))
    hshape = lambda w: jax.ShapeDtypeStruct((BATCH, C_HEADS, T_ALL, w), BF16)
    return pl.pallas_call(
        _mla_prep_kernel,
        grid=(BATCH, T_ALL // tm),
        in_specs=[
            pl.BlockSpec((1, tm, Z_BLK), lambda b, i: (b, i, ZB_SMALL)),
            pl.BlockSpec((tm, LANE), lambda b, i: (i, 0)),
            const2((C_Q_RANK, C_HEADS * C_QK)), const2((C_KV_RANK, C_HEADS * C_NOPE)),
            pl.BlockSpec((C_HEADS, VT_ROWS, C_KV_RANK), lambda b, i: (0, 0, 0)),
            const2((1, C_Q_RANK)), const2((1, C_KV_RANK)),
            const2((1, C_NOPE)), const2((1, LANE)), const2((1, C_NOPE)), const2((1, LANE)),
        ],
        out_specs=[hspec(C_QK), hspec(C_QK),
                   pl.BlockSpec((1, C_HEADS, VT_ROWS, tm), lambda b, i: (b, 0, 0, i))],
        out_shape=[hshape(C_QK), hshape(C_QK),
                   jax.ShapeDtypeStruct((BATCH, C_HEADS, VT_ROWS, T_ALL), BF16)],
        compiler_params=_cparams(("parallel", "parallel")),
        name="mla_prep",
    )(z, cs_tab, wq, wkn, wvt, gcq, gckv, gqn, gqr2, gkn, gkr2)


def _attn_kernel(q_ref, k_ref, vt_ref, o_ref):
    def max_over_keys(s):
        parts = [s[i * 8:(i + 1) * 8] for i in range(s.shape[0] // 8)]
        while len(parts) > 1:
            parts = [jnp.maximum(parts[i], parts[i + 1]) for i in range(0, len(parts) - 1, 2)] + parts[len(parts) & ~1:]
        return jnp.max(parts[0], axis=0, keepdims=True)

    def attend(n_keys):
        heads = range(HEADS_PER_STEP)
        s = [lax.dot_general(k_ref[0, h, 0:n_keys, :], q_ref[0, h], _NT, preferred_element_type=F32)
             for h in heads]
        for h in heads:
            p = jnp.exp(s[h] - max_over_keys(s[h])).astype(BF16)
            acc = jnp.dot(vt_ref[0, h, :, 0:n_keys], p, preferred_element_type=F32)
            o = acc[0:C_DV] * (1.0 / acc[C_DV:C_DV + 1])
            o_ref[0, :, h * C_DV:(h + 1) * C_DV] = o.T.astype(BF16)

    @pl.when(pl.program_id(2) == 0)
    def _():
        attend(CTX_LEN)

    @pl.when(pl.program_id(2) > 0)
    def _():
        attend(T_ALL)


def _attention(q, k, vt):
    tq = TQ
    hs = HEADS_PER_STEP
    return pl.pallas_call(
        _attn_kernel,
        grid=(BATCH, C_HEADS // hs, T_ALL // tq),
        in_specs=[
            pl.BlockSpec((1, hs, tq, C_QK), lambda b, h, i: (b, h, i, 0)),
            pl.BlockSpec((1, hs, T_ALL, C_QK), lambda b, h, i: (b, h, 0, 0)),
            pl.BlockSpec((1, hs, VT_ROWS, T_ALL), lambda b, h, i: (b, h, 0, 0)),
        ],
        out_specs=pl.BlockSpec((1, tq, hs * C_DV), lambda b, h, i: (b, i, h)),
        out_shape=jax.ShapeDtypeStruct((BATCH, T_ALL, C_HEADS * C_DV), BF16),
        compiler_params=_cparams(("parallel", "parallel", "parallel")),
        name="mla_attention",
    )(q, k, vt)


def _merge_kernel(ya_ref, yb_ref, yc_ref, gt_ref, bg_ref, wa_ref, wb_ref, wc_ref, wo_ref, x_ref, ml_ref, mc_ref,
                  gn_ref, wr_ref, x1_ref, h2a_ref, *, tm):
    is_ctx = _row_is_ctx(pl.program_id(1), tm)
    merged = None
    for j, (y_ref, w_ref) in enumerate(((ya_ref, wa_ref), (yb_ref, wb_ref), (yc_ref, wc_ref))):
        cols = slice(j * D_MODEL, (j + 1) * D_MODEL)
        gate = _sigmoid(gt_ref[0, :, cols].astype(F32) + bg_ref[:, cols])
        term = gate * jnp.dot(y_ref[0], w_ref[...], preferred_element_type=F32)
        merged = term if merged is None else merged + term
    y = jnp.dot(merged.astype(BF16), wo_ref[...], preferred_element_type=F32)
    x1 = x_ref[0] + _pick_mod(is_ctx, ml_ref, mc_ref, 2) * y
    x1_ref[0] = x1
    h2 = _rms(x1, D_MODEL) * gn_ref[...]
    h2 = h2 * (1.0 + _pick_mod(is_ctx, ml_ref, mc_ref, 4)) + _pick_mod(is_ctx, ml_ref, mc_ref, 3)
    logits = jnp.dot(h2.astype(BF16), wr_ref[...], preferred_element_type=F32)
    lane = lax.broadcasted_iota(jnp.int32, (1, LANE), 1)
    logits = jnp.where(lane < N_EXPERTS, logits, -jnp.inf)
    e = jnp.exp(logits - jnp.max(logits, axis=-1, keepdims=True))
    h2a_ref[0, :, 0:D_MODEL] = h2
    h2a_ref[0, :, D_MODEL:D_MODEL + LANE] = e / jnp.sum(e, axis=-1, keepdims=True)


def _merge(ya, yb, yc, z, b_gate, wa, wb, wc, wo, x, ml, mc, g_ffn, w_router):
    tm = TM_ROWS
    row = lambda w: pl.BlockSpec((1, tm, w), lambda b, i: (b, i, 0))

    def const2(shape):
        return pl.BlockSpec(shape, lambda b, i: (0, 0))

    sq = const2((D_MODEL, D_MODEL))
    return pl.pallas_call(
        functools.partial(_merge_kernel, tm=tm),
        grid=(BATCH, T_ALL // tm),
        in_specs=[
            row(D_MODEL), row(D_MODEL), row(D_MODEL), row(3 * D_MODEL), const2((1, 3 * D_MODEL)),
            sq, sq, sq, sq, row(D_MODEL),
            pl.BlockSpec((1, N_MOD, D_MODEL), lambda b, i: (b, 0, 0)),
            pl.BlockSpec((1, N_MOD, D_MODEL), lambda b, i: (0, 0, 0)),
            const2((1, D_MODEL)), const2((D_MODEL, LANE)),
        ],
        out_specs=[row(D_MODEL), row(D_MODEL + LANE)],
        out_shape=[jax.ShapeDtypeStruct((BATCH, T_ALL, D_MODEL), F32),
                   jax.ShapeDtypeStruct((BATCH, T_ALL, D_MODEL + LANE), F32)],
        compiler_params=_cparams(("parallel", "parallel")),
        name="merge",
    )(ya, yb, yc, z, b_gate, wa, wb, wc, wo, x, ml, mc, g_ffn, w_router)


N_ROUTE_ROWS = BATCH * N_EXPERTS
N_BISECT = 150


def _count(mask):
    return jnp.sum(jnp.where(mask, 1.0, 0.0), axis=1, keepdims=True)


def _lane_cumsum_blocks(x01):
    r, n = x01.shape
    nblk = n // LANE
    x3 = jnp.concatenate([x01[:, j * LANE:(j + 1) * LANE] for j in range(nblk)], axis=0).astype(BF16)
    ii = lax.broadcasted_iota(jnp.int32, (LANE, LANE), 0)
    jj = lax.broadcasted_iota(jnp.int32, (LANE, LANE), 1)
    inc = jnp.dot(x3, jnp.where(ii <= jj, 1.0, 0.0).astype(BF16), preferred_element_type=F32)
    off = jnp.zeros((r, 1), F32)
    out = []
    for j in range(nblk):
        blk = inc[j * r:(j + 1) * r]
        out.append(blk + off)
        off = off + blk[:, LANE - 1:LANE]
    return out


def _select_top(a, cap):
    def bisect(_, c):
        thr, step = c
        cand = thr + step
        return jnp.where(_count(a >= cand) >= cap, cand, thr), step * 0.5

    thr, _ = lax.fori_loop(0, N_BISECT, bisect, (jnp.zeros((a.shape[0], 1), F32), jnp.ones((a.shape[0], 1), F32)))
    gt = a > thr
    eq = a == thr
    need = cap - _count(gt)
    eq_rank = jnp.concatenate(_lane_cumsum_blocks(jnp.where(eq, 1.0, 0.0)), axis=1)
    sel = gt | (eq & (eq_rank <= need))
    return _lane_cumsum_blocks(jnp.where(sel, 1.0, 0.0))


def _route_kernel(aff_ref, idx_ref, cum_ref, end_ref, *, caps):
    aff_t = jnp.concatenate([aff_ref[b].T[:N_EXPERTS] for b in range(BATCH)], axis=0)
    lane = lax.broadcasted_iota(jnp.int32, (1, LANE), 1)
    p0 = 0
    for (t_lo, t_hi, cap) in caps:
        nblk = (t_hi - t_lo) // LANE
        blocks = _select_top(aff_t[:, t_lo:t_hi], cap)
        ends = jnp.full((N_ROUTE_ROWS, LANE), float(cap), F32)
        for j, blk in enumerate(blocks):
            cum_ref[:, j, :] = blk
            ends = jnp.where(lane == j, blk[:, LANE - 1:LANE], ends)
        end_ref[...] = ends
        p_col = lax.broadcasted_iota(jnp.int32, (cap, 1), 0).astype(F32)

        def per_row(r, res):
            n_full = _count(end_ref[pl.ds(r, 1), :] <= p_col)
            cum = cum_ref[r][0:nblk]
            cum = jnp.concatenate([cum, jnp.zeros((LANE - nblk, LANE), F32)], axis=0)
            hi = jnp.floor(cum * (1.0 / 64.0))
            lo = cum - 64.0 * hi
            pick = jnp.where(lane.astype(F32) == n_full, 1.0, 0.0).astype(BF16)
            part = (64.0 * jnp.dot(pick, hi.astype(BF16), preferred_element_type=F32)
                    + jnp.dot(pick, lo.astype(BF16), preferred_element_type=F32))
            cnt = LANE * n_full + _count(part <= p_col)
            return jnp.where(lane == r, cnt, res)

        res = lax.fori_loop(0, N_ROUTE_ROWS, per_row, jnp.zeros((cap, LANE), F32))
        if cap % LANE:
            res = jnp.concatenate([res, jnp.zeros((LANE - cap % LANE, LANE), F32)], axis=0)
        idx_ref[:, p0:p0 + cap] = (res.T[:N_ROUTE_ROWS, :cap] + float(t_lo)).astype(jnp.int32)
        p0 += cap


def _route(h2a, caps):
    n_sel = sum(c[2] for c in caps)
    return pl.pallas_call(
        functools.partial(_route_kernel, caps=caps),
        grid=(1,),
        in_specs=[pl.BlockSpec((BATCH, T_ALL, LANE), lambda i: (0, 0, D_MODEL // LANE))],
        out_specs=pl.BlockSpec((N_ROUTE_ROWS, n_sel), lambda i: (0, 0)),
        out_shape=jax.ShapeDtypeStruct((N_ROUTE_ROWS, n_sel), jnp.int32),
        scratch_shapes=[pltpu.VMEM((N_ROUTE_ROWS, SEQ // LANE, LANE), F32), pltpu.VMEM((N_ROUTE_ROWS, LANE), F32)],
        compiler_params=_cparams(("arbitrary",)),
        name="moe_route",
    )(h2a)


def _row_gather_copy(h2a_hbm, buf, sem, b, t, p):
    return pltpu.make_async_copy(h2a_hbm.at[b, pl.ds(t, 1), :], buf.at[pl.ds(p, 1), :], sem)


def _gather_kernel(idx_ref, h2a_hbm, xs_ref, g_ref, buf, sem, *, n_sel):
    b = pl.program_id(0)
    e = pl.program_id(1)

    def issue(p, carry):
        _row_gather_copy(h2a_hbm, buf, sem, b, idx_ref[0, 0, 0, p], p).start()
        return carry

    lax.fori_loop(0, n_sel, issue, 0, unroll=8)

    def drain(p, carry):
        _row_gather_copy(h2a_hbm, buf, sem, b, 0, p).wait()
        return carry

    lax.fori_loop(0, n_sel, drain, 0, unroll=8)
    xs_ref[0] = buf[:, 0:D_MODEL].astype(BF16)
    lane = lax.broadcasted_iota(jnp.int32, (1, LANE), 1)
    g_ref[0] = jnp.sum(jnp.where(lane == e, buf[:, D_MODEL:D_MODEL + LANE], 0.0), axis=1, keepdims=True)


def _gather(idx4, h2a):
    n_sel = idx4.shape[-1]
    return pl.pallas_call(
        functools.partial(_gather_kernel, n_sel=n_sel),
        grid=(BATCH, N_EXPERTS),
        in_specs=[pl.BlockSpec((1, 1, 1, n_sel), lambda b, e: (b, e, 0, 0), memory_space=pltpu.SMEM),
                  pl.BlockSpec(memory_space=pl.ANY)],
        out_specs=[pl.BlockSpec((1, n_sel, D_MODEL), lambda b, e: (e, b, 0)),
                   pl.BlockSpec((1, n_sel, 1), lambda b, e: (e, b, 0))],
        out_shape=[jax.ShapeDtypeStruct((N_EXPERTS, BATCH * n_sel, D_MODEL), BF16),
                   jax.ShapeDtypeStruct((N_EXPERTS, BATCH * n_sel, 1), F32)],
        scratch_shapes=[pltpu.VMEM((n_sel, D_MODEL + LANE), F32), pltpu.SemaphoreType.DMA(())],
        compiler_params=_cparams(("arbitrary", "arbitrary")),
        name="moe_gather",
    )(idx4, h2a)


def _moe_kernel(x_ref, g_ref, wg_ref, wu_ref, wd_ref, o_ref):
    x = x_ref[0]
    hid = (_silu(jnp.dot(x, wg_ref[0, 0], preferred_element_type=F32))
           * jnp.dot(x, wu_ref[0, 0], preferred_element_type=F32))
    o_ref[0] = jnp.dot(hid.astype(BF16), wd_ref[0, 0], preferred_element_type=F32) * g_ref[0]


def _moe_ffn(layer, xs, gs, wg, wu, wd):
    n_rows = xs.shape[1]
    tr = n_rows // 4
    wspec = lambda a, b: pl.BlockSpec((1, 1, a, b), lambda e, r: (layer, e, 0, 0))
    return pl.pallas_call(
        _moe_kernel,
        grid=(N_EXPERTS, n_rows // tr),
        in_specs=[
            pl.BlockSpec((1, tr, D_MODEL), lambda e, r: (e, r, 0)),
            pl.BlockSpec((1, tr, 1), lambda e, r: (e, r, 0)),
            wspec(D_MODEL, D_EXPERT), wspec(D_MODEL, D_EXPERT), wspec(D_EXPERT, D_MODEL),
        ],
        out_specs=pl.BlockSpec((1, tr, D_MODEL), lambda e, r: (e, r, 0)),
        out_shape=jax.ShapeDtypeStruct((N_EXPERTS, n_rows, D_MODEL), F32),
        compiler_params=_cparams(("parallel", "parallel")),
        name="moe_ffn",
    )(xs, gs, wg, wu, wd)


def _combine_kernel(idx_ref, ys_ref, x1_ref, ml_ref, mc_ref, o_ref, *, th, n_lat, n_sel):
    lo = pl.program_id(1) * th

    @pl.when(pl.program_id(2) == 0)
    def _():
        o_ref[...] = x1_ref[...]

    def first_at_least(bound):
        def step(_, c):
            l, h = c
            m = jnp.minimum((l + h) // 2, n_lat - 1)
            go = idx_ref[0, 0, 0, m] < bound
            return jnp.where((l < h) & go, m + 1, l), jnp.where((l < h) & jnp.logical_not(go), m, h)

        return lax.fori_loop(0, n_lat.bit_length(), step, (jnp.int32(0), jnp.int32(n_lat)))[0]

    def add_rows(p0, p1, gate):
        def body(p, carry):
            r = idx_ref[0, 0, 0, p] - lo
            o_ref[0, pl.ds(r, 1), :] = o_ref[0, pl.ds(r, 1), :] + gate * ys_ref[0, pl.ds(p, 1), :]
            return carry

        lax.fori_loop(p0, p1, body, 0)

    add_rows(first_at_least(jnp.maximum(lo, CTX_LEN)), first_at_least(lo + th), ml_ref[0, 5:6, :])
    if n_sel > n_lat:
        @pl.when(lo == 0)
        def _():
            add_rows(n_lat, n_sel, mc_ref[0, 5:6, :])


def _combine(idx4, ys, x1, ml, mc, n_lat):
    n_sel = idx4.shape[-1]
    th = T_ALL // 2
    return pl.pallas_call(
        functools.partial(_combine_kernel, th=th, n_lat=n_lat, n_sel=n_sel),
        grid=(BATCH, T_ALL // th, N_EXPERTS),
        in_specs=[pl.BlockSpec((1, 1, 1, n_sel), lambda b, t, e: (b, e, 0, 0), memory_space=pltpu.SMEM),
                  pl.BlockSpec((1, n_sel, D_MODEL), lambda b, t, e: (e, b, 0)),
                  pl.BlockSpec((1, th, D_MODEL), lambda b, t, e: (b, t, 0)),
                  pl.BlockSpec((1, N_MOD, D_MODEL), lambda b, t, e: (b, 0, 0)),
                  pl.BlockSpec((1, N_MOD, D_MODEL), lambda b, t, e: (0, 0, 0))],
        out_specs=pl.BlockSpec((1, th, D_MODEL), lambda b, t, e: (b, t, 0)),
        out_shape=jax.ShapeDtypeStruct((BATCH, T_ALL, D_MODEL), F32),
        compiler_params=_cparams(("parallel", "parallel", "arbitrary")),
        name="moe_combine",
    )(idx4, ys, x1, ml, mc)


def _rope_swap_perm():
    idx = np.arange(C_ROPE, dtype=np.int32).reshape(2, 2, C_ROPE // 4)
    return idx[:, ::-1, :].reshape(-1)


def _rope_table():
    rows = SEQ // GRID_W
    row = jnp.broadcast_to(jnp.arange(rows, dtype=F32)[:, None], (rows, GRID_W)).reshape(-1)
    col = jnp.broadcast_to(jnp.arange(GRID_W, dtype=F32)[None, :], (rows, GRID_W)).reshape(-1)
    axis_dim = C_ROPE // 2
    inv_freq = ROPE_BASE ** (-jnp.arange(0, axis_dim, 2, dtype=F32) / axis_dim)
    n_freq = C_ROPE // 4
    ang = jnp.stack([row[:, None] * inv_freq, col[:, None] * inv_freq], axis=1)
    cos = jnp.broadcast_to(jnp.cos(ang)[:, :, None, :], (SEQ, 2, 2, n_freq)).reshape(SEQ, C_ROPE)
    sin = jnp.sin(ang)
    sin = jnp.stack([-sin, sin], axis=2).reshape(SEQ, C_ROPE)
    lat = jnp.concatenate([cos, sin], axis=1)
    ctx = jnp.concatenate([jnp.ones((CTX_LEN, C_ROPE), F32), jnp.zeros((CTX_LEN, C_ROPE), F32)], axis=1)
    return jnp.concatenate([ctx, lat], axis=0)


def _layer_weights(i, w_in, w_gla_decay, b_gla_decay, w_uq, w_ukv, g_qr, g_kr, b_s):
    perm = _rope_swap_perm()
    o = [int(v) for v in np.cumsum((0, 1024, 1024, 512, 512, 1024, 1024, 32, C_Q_RANK, C_KV_RANK, C_ROPE, 3 * D_MODEL))]
    w = w_in[i]
    seg = lambda j: w[:, o[j]:o[j + 1]]
    u, va, qb, kb, vb, rb, dec, cq, ckv, kr, gates = (seg(j) for j in range(11))
    pad = jnp.zeros((D_MODEL, Z_BLK - (SM_DEC + 2 * B_GATE_RANK)), F32)
    w_in_ext = jnp.concatenate([gates, u, va, vb, rb, qb, kb, cq, ckv, kr, kr[:, perm], dec, pad], axis=1).astype(BF16)

    def dec_weight(j):
        wd = jnp.zeros((LANE, B_HEADS * B_DK), F32)
        return wd.at[j * B_GATE_RANK:(j + 1) * B_GATE_RANK].set(w_gla_decay[i, j]).astype(BF16)

    wq = w_uq[i].reshape(C_Q_RANK, C_HEADS, C_NOPE + C_ROPE)
    wq_ext = jnp.concatenate([wq, wq[:, :, C_NOPE:][:, :, perm]], axis=2).reshape(C_Q_RANK, C_HEADS * C_QK).astype(BF16)
    pack = lambda g: jnp.concatenate([g[i], g[i][perm]])[None, :]
    bs_full = jnp.repeat(b_s[i].T, A_CHUNK, axis=1)
    wkv = w_ukv[i].reshape(C_KV_RANK, C_HEADS, C_NOPE + C_DV)
    wkn = wkv[:, :, :C_NOPE].reshape(C_KV_RANK, C_HEADS * C_NOPE).astype(BF16)
    wvt = jnp.transpose(wkv[:, :, C_NOPE:], (1, 2, 0))
    wvt = jnp.concatenate([wvt, jnp.zeros((C_HEADS, VT_ROWS - C_DV, C_KV_RANK), F32)], axis=1).astype(BF16)
    return dict(w_in=w_in_ext, wdec_f=dec_weight(0), wdec_b=dec_weight(1),
                bdec_f=b_gla_decay[i, 0][None, :], bdec_b=b_gla_decay[i, 1][None, :],
                wq=wq_ext, wkn=wkn, wvt=wvt, gqr2=pack(g_qr), gkr2=pack(g_kr), bs_full=bs_full)


def kernel(x, c, ctx, c_ctx, w_mod, b_mod, g_norm_mix, g_norm_ffn, w_in, b_gate, w_s, b_s, w_gla_decay, b_gla_decay,
           g_gla_out, g_cq, g_ckv, w_uq, w_ukv, g_qn, g_qr, g_kn, g_kr, w_proj_a, w_proj_b, w_proj_c, w_out,
           w_router, w_e_gate, w_e_up, w_e_down):
    c8 = jnp.concatenate([c, c_ctx[None, :], jnp.zeros((8 - BATCH - 1, D_MODEL), F32)], axis=0)
    mod = _mod_table(c8, w_mod, b_mod).reshape(DEPTH, 8, N_MOD, D_MODEL)
    cs_tab = _rope_table()
    xc = jnp.concatenate([ctx, x], axis=1)
    row2 = lambda a: a[None, :]
    wg_all, wu_all, wd_all = w_e_gate.astype(BF16), w_e_up.astype(BF16), w_e_down.astype(BF16)
    cap_lat = CAPACITY_FACTOR * SEQ // N_EXPERTS
    cap_ctx = CAPACITY_FACTOR * CTX_LEN // N_EXPERTS
    for i in range(DEPTH):
        lw = _layer_weights(i, w_in, w_gla_decay, b_gla_decay, w_uq, w_ukv, g_qr, g_kr, b_s)
        ml, mc = mod[i, :BATCH], mod[i, BATCH:BATCH + 1]
        h = _mod_norm(xc, row2(g_norm_mix[i]), ml, mc, 0, 1)
        z = _in_proj(h.reshape(BATCH * T_ALL, D_MODEL), lw['w_in']).reshape(BATCH, T_ALL, N_Z)
        ya = _mix_a(z, w_s[i].astype(BF16), lw['bs_full'])
        yb = _gla(z, lw['wdec_f'], lw['bdec_f'], lw['wdec_b'], lw['bdec_b'], row2(g_gla_out[i]))
        q, k, vt = _mla_prep(z, cs_tab, lw['wq'], lw['wkn'], lw['wvt'], row2(g_cq[i]), row2(g_ckv[i]),
                             row2(g_qn[i]), lw['gqr2'], row2(g_kn[i]), lw['gkr2'])
        yc = _attention(q, k, vt)
        w_r = jnp.concatenate([w_router[i], jnp.zeros((D_MODEL, LANE - N_EXPERTS), F32)], axis=1).astype(BF16)
        x1, h2a = _merge(ya, yb, yc, z, row2(b_gate[i]), w_proj_a[i].astype(BF16), w_proj_b[i].astype(BF16),
                         w_proj_c[i].astype(BF16), w_out[i].astype(BF16), xc, ml, mc, row2(g_norm_ffn[i]), w_r)
        caps = ((CTX_LEN, T_ALL, cap_lat),) + (((0, CTX_LEN, cap_ctx),) if i < DEPTH - 1 else ())
        idx = _route(h2a, caps)
        idx4 = idx.reshape(BATCH, N_EXPERTS, 1, idx.shape[-1])
        xs, gs = _gather(idx4, h2a)
        ys = _moe_ffn(i, xs, gs, wg_all, wu_all, wd_all)
        xc = _combine(idx4, ys, x1, ml, mc, cap_lat)
    return xc[:, CTX_LEN:]
```

```python
import functools

import jax
import jax.numpy as jnp
import numpy as np
from jax import lax
from jax.experimental import pallas as pl
from jax.experimental.pallas import tpu as pltpu

F32 = jnp.float32
BF16 = jnp.bfloat16

D_MODEL = 1024
BATCH = 4
SEQ = 4096
DEPTH = 2
CTX_LEN = 256
T_ALL = CTX_LEN + SEQ
GRID_W = 64
EPS = 1e-6
N_MOD = 6

A_GROUPS = 8
A_CHUNK = 128

B_HEADS = 4
B_DK = 128
B_DV = 256
B_GATE_RANK = 16
B_GATE_NORM = 16.0
B_CHUNK = 64

C_HEADS = 8
C_Q_RANK = 384
C_KV_RANK = 256
C_NOPE = 128
C_ROPE = 64
C_DV = 128
ROPE_BASE = 10000.0
C_QK = 2 * C_NOPE

N_EXPERTS = 16
CAPACITY_FACTOR = 2
D_EXPERT = 2048

Z_BLK = 1024
ZB_GATES, ZB_U, ZB_VA, ZB_VB, ZB_RB, ZB_QK, ZB_SMALL = 0, 3, 4, 5, 6, 7, 8
N_Z = 9 * Z_BLK
SM_CQ, SM_CKV, SM_KR, SM_DEC = 0, C_Q_RANK, C_Q_RANK + C_KV_RANK, C_Q_RANK + C_KV_RANK + 2 * C_ROPE

LOG2_E = 1.4426950408889634
LANE = 128
VMEM_LIMIT = 56 * 1024 * 1024

TM_ROWS = 544
TM_MM = 1088
TM_PREP = 256
TQ = 256
HEADS_PER_STEP = 4
KEY_BLOCK = 1152
VT_ROWS = C_DV + 16
TB_GLA = 256
TW_MOE = 256


def _cparams(sem):
    return pltpu.CompilerParams(dimension_semantics=sem, vmem_limit_bytes=VMEM_LIMIT)


def _rms(x, n):
    return x * lax.rsqrt(jnp.sum(x * x, axis=-1, keepdims=True) * (1.0 / n) + EPS)


def _silu(x):
    return x * (1.0 / (1.0 + jnp.exp(-x)))


def _sigmoid(x):
    return 1.0 / (1.0 + jnp.exp(-x))


def _row_is_ctx(tile_idx, tm):
    row = tile_idx * tm + lax.broadcasted_iota(jnp.int32, (tm, 1), 0)
    return row < CTX_LEN


def _pick_mod(is_ctx, ml_ref, mc_ref, k):
    return jnp.where(is_ctx, mc_ref[0, k:k + 1, :], ml_ref[0, k:k + 1, :])


def _mod_kernel(c_ref, w_ref, b_ref, o_ref):
    s = _silu(c_ref[...])
    o_ref[0] = jnp.dot(s.astype(BF16), w_ref[0].astype(BF16), preferred_element_type=F32) + b_ref[0]


def _mod_table(c8, w_mod, b_mod):
    tn = 512
    n = N_MOD * D_MODEL
    return pl.pallas_call(
        _mod_kernel,
        grid=(DEPTH, n // tn),
        in_specs=[
            pl.BlockSpec((8, D_MODEL), lambda l, j: (0, 0)),
            pl.BlockSpec((1, D_MODEL, tn), lambda l, j: (l, 0, j)),
            pl.BlockSpec((1, 1, tn), lambda l, j: (l, 0, j)),
        ],
        out_specs=pl.BlockSpec((1, 8, tn), lambda l, j: (l, 0, j)),
        out_shape=jax.ShapeDtypeStruct((DEPTH, 8, n), F32),
        compiler_params=_cparams(("parallel", "parallel")),
        name="mod_table",
    )(c8, w_mod, b_mod.reshape(DEPTH, 1, n))


def _norm_kernel(x_ref, g_ref, ml_ref, mc_ref, o_ref, *, tm, k_shift, k_scale):
    is_ctx = _row_is_ctx(pl.program_id(1), tm)
    y = _rms(x_ref[0], D_MODEL) * g_ref[...]
    y = y * (1.0 + _pick_mod(is_ctx, ml_ref, mc_ref, k_scale)) + _pick_mod(is_ctx, ml_ref, mc_ref, k_shift)
    o_ref[0] = y.astype(BF16)


def _mod_norm(x, g, ml, mc, k_shift, k_scale):
    tm = TM_ROWS
    return pl.pallas_call(
        functools.partial(_norm_kernel, tm=tm, k_shift=k_shift, k_scale=k_scale),
        grid=(BATCH, T_ALL // tm),
        in_specs=[
            pl.BlockSpec((1, tm, D_MODEL), lambda b, i: (b, i, 0)),
            pl.BlockSpec((1, D_MODEL), lambda b, i: (0, 0)),
            pl.BlockSpec((1, N_MOD, D_MODEL), lambda b, i: (b, 0, 0)),
            pl.BlockSpec((1, N_MOD, D_MODEL), lambda b, i: (0, 0, 0)),
        ],
        out_specs=pl.BlockSpec((1, tm, D_MODEL), lambda b, i: (b, i, 0)),
        out_shape=jax.ShapeDtypeStruct((BATCH, T_ALL, D_MODEL), BF16),
        compiler_params=_cparams(("parallel", "parallel")),
        name="mod_norm",
    )(x, g, ml, mc)


def _matmul_kernel(a_ref, w_ref, o_ref):
    o_ref[...] = jnp.dot(a_ref[...], w_ref[...], preferred_element_type=F32).astype(o_ref.dtype)


def _in_proj(h2d, w):
    m, k = h2d.shape
    n = w.shape[1]
    tm, tn = TM_MM, Z_BLK
    return pl.pallas_call(
        _matmul_kernel,
        grid=(m // tm, n // tn),
        in_specs=[
            pl.BlockSpec((tm, k), lambda i, j: (i, 0)),
            pl.BlockSpec((k, tn), lambda i, j: (0, j)),
        ],
        out_specs=pl.BlockSpec((tm, tn), lambda i, j: (i, j)),
        out_shape=jax.ShapeDtypeStruct((m, n), BF16),
        compiler_params=_cparams(("parallel", "parallel")),
        name="in_proj",
    )(h2d, w)


def _mix_a_kernel(u_ref, v_ref, ws_ref, bs_ref, o_ref, *, n_chunk):
    def chunk(c, carry):
        r0 = pl.multiple_of(c * A_CHUNK, A_CHUNK)
        for g in range(A_GROUPS):
            cols = slice(g * LANE, (g + 1) * LANE)
            v = v_ref[0, pl.ds(r0, A_CHUNK), cols].astype(F32)
            mu = jnp.mean(v, axis=-1, keepdims=True)
            d = v - mu
            ln = d * lax.rsqrt(jnp.mean(d * d, axis=-1, keepdims=True) + EPS)
            mixed = jnp.dot(ws_ref[g], ln.astype(BF16), preferred_element_type=F32) + bs_ref[:, cols]
            u = u_ref[0, pl.ds(r0, A_CHUNK), cols].astype(F32)
            o_ref[0, pl.ds(r0, A_CHUNK), cols] = (u * mixed).astype(BF16)
        return carry

    lax.fori_loop(0, n_chunk, chunk, 0)


def _mix_a(z, ws, bs_full):
    ta = T_ALL // 2
    return pl.pallas_call(
        functools.partial(_mix_a_kernel, n_chunk=ta // A_CHUNK),
        grid=(BATCH, T_ALL // ta),
        in_specs=[
            pl.BlockSpec((1, ta, Z_BLK), lambda b, i: (b, i, ZB_U)),
            pl.BlockSpec((1, ta, Z_BLK), lambda b, i: (b, i, ZB_VA)),
            pl.BlockSpec((A_GROUPS, A_CHUNK, A_CHUNK), lambda b, i: (0, 0, 0)),
            pl.BlockSpec((A_CHUNK, D_MODEL), lambda b, i: (0, 0)),
        ],
        out_specs=pl.BlockSpec((1, ta, D_MODEL), lambda b, i: (b, i, 0)),
        out_shape=jax.ShapeDtypeStruct((BATCH, T_ALL, D_MODEL), BF16),
        compiler_params=_cparams(("parallel", "parallel")),
        name="mix_a",
    )(z, z, ws, bs_full)


_NT = (((1,), (1,)), ((), ()))
_TN = (((0,), (0,)), ((), ()))


def _log_sigmoid(x):
    return jnp.minimum(x, 0.0) - jnp.log(1.0 + jnp.exp(-jnp.abs(x)))


def _exact_tri_sum(tri, la):
    hi = la.astype(BF16)
    r1 = la - hi.astype(F32)
    mid = r1.astype(BF16)
    lo = (r1 - mid.astype(F32)).astype(BF16)
    return (jnp.dot(tri, hi, preferred_element_type=F32)
            + jnp.dot(tri, mid, preferred_element_type=F32)
            + jnp.dot(tri, lo, preferred_element_type=F32))


def _gla_block(qk_ref, v_ref, sm_ref, wdec_ref, bdec_ref, st_ref, *, reverse, emit):
    n_chunk = TB_GLA // B_CHUNK
    shift = B_CHUNK.bit_length() - 1
    ii = lax.broadcasted_iota(jnp.int32, (TB_GLA, TB_GLA), 0)
    jj = lax.broadcasted_iota(jnp.int32, (TB_GLA, TB_GLA), 1)
    keep = ((ii >> shift) == (jj >> shift)) & ((jj >= ii) if reverse else (jj <= ii))
    edge = 0 if reverse else B_CHUNK - 1

    pre = jnp.dot(sm_ref[0, :, SM_DEC:SM_DEC + LANE], wdec_ref[...], preferred_element_type=F32) + bdec_ref[...]
    la = _log_sigmoid(pre) * (1.0 / B_GATE_NORM)
    bcum = _exact_tri_sum(jnp.where(keep, 1.0, 0.0).astype(BF16), la)
    blast = jnp.concatenate(
        [jnp.broadcast_to(bcum[c * B_CHUNK + edge:c * B_CHUNK + edge + 1], (B_CHUNK, B_HEADS * B_DK))
         for c in range(n_chunk)], axis=0)
    e_fwd, e_inv, e_rest = jnp.exp(bcum), jnp.exp(-bcum), jnp.exp(blast - bcum)
    e_tot = jnp.exp(blast)
    for h in range(B_HEADS):
        kc = slice(h * B_DK, (h + 1) * B_DK)
        vc = slice(h * B_DV, (h + 1) * B_DV)
        q = qk_ref[0, :, kc].astype(F32)
        k = qk_ref[0, :, B_HEADS * B_DK + h * B_DK:B_HEADS * B_DK + (h + 1) * B_DK].astype(F32)
        v = v_ref[0, :, vc]
        qd = ((q * B_DK ** -0.5) * e_fwd[:, kc]).astype(BF16)
        kinv = (k * e_inv[:, kc]).astype(BF16)
        kdec = (k * e_rest[:, kc]).astype(BF16)
        s = lax.dot_general(qd, kinv, _NT, preferred_element_type=F32)
        o_intra = jnp.dot(jnp.where(keep, s, 0.0).astype(BF16), v, preferred_element_type=F32)
        st = st_ref[h]
        for c in (range(n_chunk - 1, -1, -1) if reverse else range(n_chunk)):
            rows = slice(c * B_CHUNK, (c + 1) * B_CHUNK)
            emit(h, rows, o_intra[rows] + lax.dot_general(qd[rows], st.astype(BF16), _NT, preferred_element_type=F32))
            kv_t = lax.dot_general(v[rows], kdec[rows], _TN, preferred_element_type=F32)
            st = e_tot[c * B_CHUNK:c * B_CHUNK + 1, kc] * st + kv_t
        st_ref[h] = st


def _gla_fwd_kernel(qk_ref, v_ref, sm_ref, wdec_ref, bdec_ref, o_ref, st_ref):
    @pl.when(pl.program_id(1) == 0)
    def _():
        st_ref[...] = jnp.zeros_like(st_ref)

    def emit(h, rows, o):
        o_ref[0, rows, h * B_DV:(h + 1) * B_DV] = o

    _gla_block(qk_ref, v_ref, sm_ref, wdec_ref, bdec_ref, st_ref, reverse=False, emit=emit)


def _gla_bwd_kernel(qk_ref, v_ref, sm_ref, wdec_ref, bdec_ref, of_ref, r_ref, g_ref, y_ref, st_ref):
    @pl.when(pl.program_id(1) == 0)
    def _():
        st_ref[...] = jnp.zeros_like(st_ref)

    def emit(h, rows, o):
        vc = slice(h * B_DV, (h + 1) * B_DV)
        o = o + of_ref[0, rows, vc]
        on = _rms(o, B_DV) * g_ref[...]
        y_ref[0, rows, vc] = (on * _silu(r_ref[0, rows, vc].astype(F32))).astype(BF16)

    _gla_block(qk_ref, v_ref, sm_ref, wdec_ref, bdec_ref, st_ref, reverse=True, emit=emit)


def _bwd_block(i):
    nb = T_ALL // TB_GLA
    return jnp.where(i == 0, 0, nb - i)


def _gla(z, wdec_f, bdec_f, wdec_b, bdec_b, g_out):
    tb = TB_GLA
    nb = T_ALL // tb
    wide = B_HEADS * B_DV

    def zspec(blk, imap):
        return pl.BlockSpec((1, tb, Z_BLK), lambda b, i: (b, imap(i), blk))

    def const2(shape):
        return pl.BlockSpec(shape, lambda b, i: (0, 0))

    state = pltpu.VMEM((B_HEADS, B_DV, B_DK), F32)
    fwd = lambda i: i
    o_f = pl.pallas_call(
        _gla_fwd_kernel,
        grid=(BATCH, nb),
        in_specs=[zspec(ZB_QK, fwd), zspec(ZB_VB, fwd), zspec(ZB_SMALL, fwd),
                  const2((LANE, B_HEADS * B_DK)), const2((1, B_HEADS * B_DK))],
        out_specs=pl.BlockSpec((1, tb, wide), lambda b, i: (b, i, 0)),
        out_shape=jax.ShapeDtypeStruct((BATCH, T_ALL, wide), F32),
        scratch_shapes=[state],
        compiler_params=_cparams(("parallel", "arbitrary")),
        name="gla_fwd",
    )(z, z, z, wdec_f, bdec_f)
    return pl.pallas_call(
        _gla_bwd_kernel,
        grid=(BATCH, nb),
        in_specs=[zspec(ZB_QK, _bwd_block), zspec(ZB_VB, _bwd_block), zspec(ZB_SMALL, _bwd_block),
                  const2((LANE, B_HEADS * B_DK)), const2((1, B_HEADS * B_DK)),
                  pl.BlockSpec((1, tb, wide), lambda b, i: (b, _bwd_block(i), 0)),
                  zspec(ZB_RB, _bwd_block), const2((1, B_DV))],
        out_specs=pl.BlockSpec((1, tb, wide), lambda b, i: (b, _bwd_block(i), 0)),
        out_shape=jax.ShapeDtypeStruct((BATCH, T_ALL, wide), BF16),
        scratch_shapes=[state],
        compiler_params=_cparams(("parallel", "arbitrary")),
        name="gla_bwd",
    )(z, z, z, wdec_b, bdec_b, o_f, z, g_out)


def _mla_prep_kernel(sm_ref, cs_ref, wq_ref, wkn_ref, wvt_ref, gcq_ref, gckv_ref, gqn_ref, gqr_ref, gkn_ref, gkr_ref,
                     q_ref, k_ref, vt_ref):
    scale = (C_NOPE + C_ROPE) ** -0.5 * LOG2_E
    vt_row = lax.broadcasted_iota(jnp.int32, (VT_ROWS, 1), 0)
    cs = cs_ref[...]
    lane = lax.broadcasted_iota(jnp.int32, (1, LANE), 1)
    low_half = (lane < C_ROPE).astype(F32)

    def rope(pack, gain):
        t = _rms(pack, LANE) * gain * cs
        return t + pltpu.roll(t, C_ROPE, axis=1)

    cq = sm_ref[0, :, SM_CQ:SM_CQ + C_Q_RANK].astype(F32)
    cqn = (_rms(cq, C_Q_RANK) * gcq_ref[...]).astype(BF16)
    q_all = jnp.dot(cqn, wq_ref[...], preferred_element_type=F32)
    ckv = sm_ref[0, :, SM_CKV:SM_CKV + C_KV_RANK].astype(F32)
    ckvn = (_rms(ckv, C_KV_RANK) * gckv_ref[...]).astype(BF16)
    kn_all = jnp.dot(ckvn, wkn_ref[...], preferred_element_type=F32)
    k_rot = rope(sm_ref[0, :, SM_KR:SM_KR + LANE].astype(F32), gkr_ref[...]).astype(BF16)
    for h in range(C_HEADS):
        c0 = h * C_QK
        qn = _rms(q_all[:, c0:c0 + C_NOPE], C_NOPE) * gqn_ref[...]
        q_ref[0, h, :, 0:C_NOPE] = (qn * scale).astype(BF16)
        q_rot = rope(q_all[:, c0 + C_NOPE:c0 + C_QK], gqr_ref[...]) * low_half
        q_ref[0, h, :, C_NOPE:C_QK] = (q_rot * scale).astype(BF16)
        kn = _rms(kn_all[:, h * C_NOPE:(h + 1) * C_NOPE], C_NOPE) * gkn_ref[...]
        k_ref[0, h, :, 0:C_NOPE] = kn.astype(BF16)
        k_ref[0, h, :, C_NOPE:C_QK] = k_rot
        vt = lax.dot_general(wvt_ref[h], ckvn, _NT, preferred_element_type=F32)
        vt_ref[0, h] = jnp.where(vt_row == C_DV, 1.0, vt).astype(BF16)


def _mla_prep(z, cs_tab, wq, wkn, wvt, gcq, gckv, gqn, gqr2, gkn, gkr2):
    tm = TM_PREP

    def const2(shape):
        return pl.BlockSpec(shape, lambda b, i: (0, 0))

    hspec = lambda w: pl.BlockSpec((1, C_HEADS, tm, w), lambda b, i: (b, 0, i, 0))
    hshape = lambda w: jax.ShapeDtypeStruct((BATCH, C_HEADS, T_ALL, w), BF16)
    return pl.pallas_call(
        _mla_prep_kernel,
        grid=(BATCH, T_ALL // tm),
        in_specs=[
            pl.BlockSpec((1, tm, Z_BLK), lambda b, i: (b, i, ZB_SMALL)),
            pl.BlockSpec((tm, LANE), lambda b, i: (i, 0)),
            const2((C_Q_RANK, C_HEADS * C_QK)), const2((C_KV_RANK, C_HEADS * C_NOPE)),
            pl.BlockSpec((C_HEADS, VT_ROWS, C_KV_RANK), lambda b, i: (0, 0, 0)),
            const2((1, C_Q_RANK)), const2((1, C_KV_RANK)),
            const2((1, C_NOPE)), const2((1, LANE)), const2((1, C_NOPE)), const2((1, LANE)),
        ],
        out_specs=[hspec(C_QK), hspec(C_QK),
                   pl.BlockSpec((1, C_HEADS, VT_ROWS, tm), lambda b, i: (b, 0, 0, i))],
        out_shape=[hshape(C_QK), hshape(C_QK),
                   jax.ShapeDtypeStruct((BATCH, C_HEADS, VT_ROWS, T_ALL), BF16)],
        compiler_params=_cparams(("parallel", "parallel")),
        name="mla_prep",
    )(z, cs_tab, wq, wkn, wvt, gcq, gckv, gqn, gqr2, gkn, gkr2)


def _attn_kernel(q_ref, k_ref, vt_ref, o_ref):
    def max_over_keys(s):
        parts = [s[i * 8:(i + 1) * 8] for i in range(s.shape[0] // 8)]
        while len(parts) > 1:
            parts = [jnp.maximum(parts[i], parts[i + 1]) for i in range(0, len(parts) - 1, 2)] + parts[len(parts) & ~1:]
        return jnp.max(parts[0], axis=0, keepdims=True)

    def attend(n_keys):
        edges = list(range(0, n_keys, KEY_BLOCK)) + [n_keys]
        blocks = [slice(a, b) for a, b in zip(edges[:-1], edges[1:])]
        heads = range(HEADS_PER_STEP)
        s = [[lax.dot_general(k_ref[0, h, kb, :], q_ref[0, h], _NT, preferred_element_type=F32) for kb in blocks]
             for h in heads]
        for h in heads:
            m = functools.reduce(jnp.maximum, [max_over_keys(sb) for sb in s[h]])
            acc = None
            for kb, sb in zip(blocks, s[h]):
                p = jnp.exp2(sb - m).astype(BF16)
                part = jnp.dot(vt_ref[0, h, :, kb], p, preferred_element_type=F32)
                acc = part if acc is None else acc + part
            o = acc[0:C_DV] * (1.0 / acc[C_DV:C_DV + 1])
            o_ref[0, :, h * C_DV:(h + 1) * C_DV] = o.T.astype(BF16)

    @pl.when(pl.program_id(2) == 0)
    def _():
        attend(CTX_LEN)

    @pl.when(pl.program_id(2) > 0)
    def _():
        attend(T_ALL)


def _attention(q, k, vt):
    tq = TQ
    hs = HEADS_PER_STEP
    return pl.pallas_call(
        _attn_kernel,
        grid=(BATCH, C_HEADS // hs, T_ALL // tq),
        in_specs=[
            pl.BlockSpec((1, hs, tq, C_QK), lambda b, h, i: (b, h, i, 0)),
            pl.BlockSpec((1, hs, T_ALL, C_QK), lambda b, h, i: (b, h, 0, 0)),
            pl.BlockSpec((1, hs, VT_ROWS, T_ALL), lambda b, h, i: (b, h, 0, 0)),
        ],
        out_specs=pl.BlockSpec((1, tq, hs * C_DV), lambda b, h, i: (b, i, h)),
        out_shape=jax.ShapeDtypeStruct((BATCH, T_ALL, C_HEADS * C_DV), BF16),
        compiler_params=_cparams(("parallel", "parallel", "parallel")),
        name="mla_attention",
    )(q, k, vt)


def _merge_kernel(ya_ref, yb_ref, yc_ref, gt_ref, bg_ref, wa_ref, wb_ref, wc_ref, wo_ref, x_ref, ml_ref, mc_ref,
                  gn_ref, wr_ref, x1_ref, h2a_ref, *, tm):
    is_ctx = _row_is_ctx(pl.program_id(1), tm)
    merged = None
    for j, (y_ref, w_ref) in enumerate(((ya_ref, wa_ref), (yb_ref, wb_ref), (yc_ref, wc_ref))):
        cols = slice(j * D_MODEL, (j + 1) * D_MODEL)
        gate = _sigmoid(gt_ref[0, :, cols].astype(F32) + bg_ref[:, cols])
        term = gate * jnp.dot(y_ref[0], w_ref[...], preferred_element_type=F32)
        merged = term if merged is None else merged + term
    y = jnp.dot(merged.astype(BF16), wo_ref[...], preferred_element_type=F32)
    x1 = x_ref[0] + _pick_mod(is_ctx, ml_ref, mc_ref, 2) * y
    x1_ref[0] = x1
    h2 = _rms(x1, D_MODEL) * gn_ref[...]
    h2 = h2 * (1.0 + _pick_mod(is_ctx, ml_ref, mc_ref, 4)) + _pick_mod(is_ctx, ml_ref, mc_ref, 3)
    logits = jnp.dot(h2.astype(BF16), wr_ref[...], preferred_element_type=F32)
    lane = lax.broadcasted_iota(jnp.int32, (1, LANE), 1)
    logits = jnp.where(lane < N_EXPERTS, logits, -jnp.inf)
    e = jnp.exp(logits - jnp.max(logits, axis=-1, keepdims=True))
    h2a_ref[0, :, 0:D_MODEL] = h2
    h2a_ref[0, :, D_MODEL:D_MODEL + LANE] = e / jnp.sum(e, axis=-1, keepdims=True)


def _merge(ya, yb, yc, z, b_gate, wa, wb, wc, wo, x, ml, mc, g_ffn, w_router):
    tm = TM_ROWS
    row = lambda w: pl.BlockSpec((1, tm, w), lambda b, i: (b, i, 0))

    def const2(shape):
        return pl.BlockSpec(shape, lambda b, i: (0, 0))

    sq = const2((D_MODEL, D_MODEL))
    return pl.pallas_call(
        functools.partial(_merge_kernel, tm=tm),
        grid=(BATCH, T_ALL // tm),
        in_specs=[
            row(D_MODEL), row(D_MODEL), row(D_MODEL), row(3 * D_MODEL), const2((1, 3 * D_MODEL)),
            sq, sq, sq, sq, row(D_MODEL),
            pl.BlockSpec((1, N_MOD, D_MODEL), lambda b, i: (b, 0, 0)),
            pl.BlockSpec((1, N_MOD, D_MODEL), lambda b, i: (0, 0, 0)),
            const2((1, D_MODEL)), const2((D_MODEL, LANE)),
        ],
        out_specs=[row(D_MODEL), row(D_MODEL + LANE)],
        out_shape=[jax.ShapeDtypeStruct((BATCH, T_ALL, D_MODEL), F32),
                   jax.ShapeDtypeStruct((BATCH, T_ALL, D_MODEL + LANE), F32)],
        compiler_params=_cparams(("parallel", "parallel")),
        name="merge",
    )(ya, yb, yc, z, b_gate, wa, wb, wc, wo, x, ml, mc, g_ffn, w_router)


N_ROUTE_ROWS = BATCH * N_EXPERTS
N_BISECT = 150


def _count(mask):
    return jnp.sum(jnp.where(mask, 1.0, 0.0), axis=1, keepdims=True)


def _lane_cumsum_blocks(x01):
    r, n = x01.shape
    nblk = n // LANE
    x3 = jnp.concatenate([x01[:, j * LANE:(j + 1) * LANE] for j in range(nblk)], axis=0).astype(BF16)
    ii = lax.broadcasted_iota(jnp.int32, (LANE, LANE), 0)
    jj = lax.broadcasted_iota(jnp.int32, (LANE, LANE), 1)
    inc = jnp.dot(x3, jnp.where(ii <= jj, 1.0, 0.0).astype(BF16), preferred_element_type=F32)
    off = jnp.zeros((r, 1), F32)
    out = []
    for j in range(nblk):
        blk = inc[j * r:(j + 1) * r]
        out.append(blk + off)
        off = off + blk[:, LANE - 1:LANE]
    return out


def _select_top(a, cap):
    def bisect(_, c):
        thr, step = c
        cand = thr + step
        return jnp.where(_count(a >= cand) >= cap, cand, thr), step * 0.5

    thr, _ = lax.fori_loop(0, N_BISECT, bisect, (jnp.zeros((a.shape[0], 1), F32), jnp.ones((a.shape[0], 1), F32)))
    gt = a > thr
    eq = a == thr
    need = cap - _count(gt)
    eq_rank = jnp.concatenate(_lane_cumsum_blocks(jnp.where(eq, 1.0, 0.0)), axis=1)
    sel = gt | (eq & (eq_rank <= need))
    return _lane_cumsum_blocks(jnp.where(sel, 1.0, 0.0))


def _route_kernel(aff_ref, idx_ref, cum_ref, end_ref, *, caps):
    aff_t = jnp.concatenate([aff_ref[b].T[:N_EXPERTS] for b in range(BATCH)], axis=0)
    lane = lax.broadcasted_iota(jnp.int32, (1, LANE), 1)
    p0 = 0
    for (t_lo, t_hi, cap) in caps:
        nblk = (t_hi - t_lo) // LANE
        blocks = _select_top(aff_t[:, t_lo:t_hi], cap)
        ends = jnp.full((N_ROUTE_ROWS, LANE), float(cap), F32)
        for j, blk in enumerate(blocks):
            cum_ref[:, j, :] = blk
            ends = jnp.where(lane == j, blk[:, LANE - 1:LANE], ends)
        end_ref[...] = ends
        p_col = lax.broadcasted_iota(jnp.int32, (cap, 1), 0).astype(F32)

        def per_row(r, res):
            n_full = _count(end_ref[pl.ds(r, 1), :] <= p_col)
            cum = cum_ref[r][0:nblk]
            cum = jnp.concatenate([cum, jnp.zeros((LANE - nblk, LANE), F32)], axis=0)
            hi = jnp.floor(cum * (1.0 / 64.0))
            lo = cum - 64.0 * hi
            pick = jnp.where(lane.astype(F32) == n_full, 1.0, 0.0).astype(BF16)
            part = (64.0 * jnp.dot(pick, hi.astype(BF16), preferred_element_type=F32)
                    + jnp.dot(pick, lo.astype(BF16), preferred_element_type=F32))
            cnt = LANE * n_full + _count(part <= p_col)
            return jnp.where(lane == r, cnt, res)

        res = lax.fori_loop(0, N_ROUTE_ROWS, per_row, jnp.zeros((cap, LANE), F32))
        if cap % LANE:
            res = jnp.concatenate([res, jnp.zeros((LANE - cap % LANE, LANE), F32)], axis=0)
        idx_ref[:, p0:p0 + cap] = (res.T[:N_ROUTE_ROWS, :cap] + float(t_lo)).astype(jnp.int32)
        p0 += cap


def _route(h2a, caps):
    n_sel = sum(c[2] for c in caps)
    return pl.pallas_call(
        functools.partial(_route_kernel, caps=caps),
        grid=(1,),
        in_specs=[pl.BlockSpec((BATCH, T_ALL, LANE), lambda i: (0, 0, D_MODEL // LANE))],
        out_specs=pl.BlockSpec((N_ROUTE_ROWS, n_sel), lambda i: (0, 0)),
        out_shape=jax.ShapeDtypeStruct((N_ROUTE_ROWS, n_sel), jnp.int32),
        scratch_shapes=[pltpu.VMEM((N_ROUTE_ROWS, SEQ // LANE, LANE), F32), pltpu.VMEM((N_ROUTE_ROWS, LANE), F32)],
        compiler_params=_cparams(("arbitrary",)),
        name="moe_route",
    )(h2a)


def _row_gather_copy(h2a_hbm, buf, sem, b, t, p):
    return pltpu.make_async_copy(h2a_hbm.at[b, pl.ds(t, 1), :], buf.at[pl.ds(p, 1), :], sem)


def _gather_kernel(idx_ref, h2a_hbm, xs_ref, g_ref, buf, sem, *, n_sel):
    b = pl.program_id(0)
    e = pl.program_id(1)

    def issue(p, carry):
        _row_gather_copy(h2a_hbm, buf, sem, b, idx_ref[0, 0, 0, p], p).start()
        return carry

    lax.fori_loop(0, n_sel, issue, 0, unroll=8)

    def drain(p, carry):
        _row_gather_copy(h2a_hbm, buf, sem, b, 0, p).wait()
        return carry

    lax.fori_loop(0, n_sel, drain, 0, unroll=8)
    xs_ref[0] = buf[:, 0:D_MODEL].astype(BF16)
    lane = lax.broadcasted_iota(jnp.int32, (1, LANE), 1)
    g_ref[0] = jnp.sum(jnp.where(lane == e, buf[:, D_MODEL:D_MODEL + LANE], 0.0), axis=1, keepdims=True)


def _gather(idx4, h2a):
    n_sel = idx4.shape[-1]
    return pl.pallas_call(
        functools.partial(_gather_kernel, n_sel=n_sel),
        grid=(BATCH, N_EXPERTS),
        in_specs=[pl.BlockSpec((1, 1, 1, n_sel), lambda b, e: (b, e, 0, 0), memory_space=pltpu.SMEM),
                  pl.BlockSpec(memory_space=pl.ANY)],
        out_specs=[pl.BlockSpec((1, n_sel, D_MODEL), lambda b, e: (e, b, 0)),
                   pl.BlockSpec((1, n_sel, 1), lambda b, e: (e, b, 0))],
        out_shape=[jax.ShapeDtypeStruct((N_EXPERTS, BATCH * n_sel, D_MODEL), BF16),
                   jax.ShapeDtypeStruct((N_EXPERTS, BATCH * n_sel, 1), F32)],
        scratch_shapes=[pltpu.VMEM((n_sel, D_MODEL + LANE), F32), pltpu.SemaphoreType.DMA(())],
        compiler_params=_cparams(("arbitrary", "arbitrary")),
        name="moe_gather",
    )(idx4, h2a)


N_HID_TILES = D_EXPERT // TW_MOE
N_OUT_TILES = D_MODEL // TW_MOE


def _moe_kernel(x_ref, g_ref, wg_ref, wu_ref, wd_ref, o_ref, hid_ref):
    j = pl.program_id(1)

    @pl.when(j < N_HID_TILES)
    def _():
        x = x_ref[0]
        h = (_silu(jnp.dot(x, wg_ref[0, 0].astype(BF16), preferred_element_type=F32))
             * jnp.dot(x, wu_ref[0, 0].astype(BF16), preferred_element_type=F32)).astype(BF16)
        for f in range(N_HID_TILES):
            @pl.when(j == f)
            def _(f=f):
                hid_ref[:, f * TW_MOE:(f + 1) * TW_MOE] = h

    @pl.when(j >= N_HID_TILES)
    def _():
        o_ref[0] = jnp.dot(hid_ref[...], wd_ref[0, 0].astype(BF16), preferred_element_type=F32) * g_ref[0]


def _moe_ffn(layer, xs, gs, wg, wu, wd):
    n_rows = xs.shape[1]
    hid_tile = lambda e, j: (layer, e, 0, jnp.minimum(j, N_HID_TILES - 1))
    out_tile = lambda j: jnp.maximum(j - N_HID_TILES, 0)
    return pl.pallas_call(
        _moe_kernel,
        grid=(N_EXPERTS, N_HID_TILES + N_OUT_TILES),
        in_specs=[
            pl.BlockSpec((1, n_rows, D_MODEL), lambda e, j: (e, 0, 0)),
            pl.BlockSpec((1, n_rows, 1), lambda e, j: (e, 0, 0)),
            pl.BlockSpec((1, 1, D_MODEL, TW_MOE), hid_tile),
            pl.BlockSpec((1, 1, D_MODEL, TW_MOE), hid_tile),
            pl.BlockSpec((1, 1, D_EXPERT, TW_MOE), lambda e, j: (layer, e, 0, out_tile(j))),
        ],
        out_specs=pl.BlockSpec((1, n_rows, TW_MOE), lambda e, j: (e, 0, out_tile(j))),
        out_shape=jax.ShapeDtypeStruct((N_EXPERTS, n_rows, D_MODEL), F32),
        scratch_shapes=[pltpu.VMEM((n_rows, D_EXPERT), BF16)],
        compiler_params=_cparams(("parallel", "arbitrary")),
        name="moe_ffn",
    )(xs, gs, wg, wu, wd)


ROW_GROUP = 8


def _combine_kernel(idx_ref, ys_ref, x1_hbm, ml_ref, mc_ref, o_ref, sem, *, n_lat, n_sel):
    b = pl.program_id(0)

    @pl.when(pl.program_id(1) == 0)
    def _():
        seed = pltpu.make_async_copy(x1_hbm.at[b], o_ref.at[0], sem)
        seed.start()
        seed.wait()

    def add_rows(p_lo, p_hi, gate):
        def group(i, carry):
            p0 = p_lo + i * ROW_GROUP
            toks = [idx_ref[0, 0, 0, p0 + r] for r in range(ROW_GROUP)]
            new = [o_ref[0, pl.ds(toks[r], 1), :] + gate * ys_ref[0, pl.ds(p0 + r, 1), :] for r in range(ROW_GROUP)]
            for r in range(ROW_GROUP):
                o_ref[0, pl.ds(toks[r], 1), :] = new[r]
            return carry

        lax.fori_loop(0, (p_hi - p_lo) // ROW_GROUP, group, 0)

    add_rows(0, n_lat, ml_ref[0, 5:6, :])
    if n_sel > n_lat:
        add_rows(n_lat, n_sel, mc_ref[0, 5:6, :])


def _combine(idx4, ys, x1, ml, mc, n_lat):
    n_sel = idx4.shape[-1]
    assert n_lat % ROW_GROUP == 0 and (n_sel - n_lat) % ROW_GROUP == 0
    return pl.pallas_call(
        functools.partial(_combine_kernel, n_lat=n_lat, n_sel=n_sel),
        grid=(BATCH, N_EXPERTS),
        in_specs=[pl.BlockSpec((1, 1, 1, n_sel), lambda b, e: (b, e, 0, 0), memory_space=pltpu.SMEM),
                  pl.BlockSpec((1, n_sel, D_MODEL), lambda b, e: (e, b, 0)),
                  pl.BlockSpec(memory_space=pl.ANY),
                  pl.BlockSpec((1, N_MOD, D_MODEL), lambda b, e: (b, 0, 0)),
                  pl.BlockSpec((1, N_MOD, D_MODEL), lambda b, e: (0, 0, 0))],
        out_specs=pl.BlockSpec((1, T_ALL, D_MODEL), lambda b, e: (b, 0, 0)),
        out_shape=jax.ShapeDtypeStruct((BATCH, T_ALL, D_MODEL), F32),
        scratch_shapes=[pltpu.SemaphoreType.DMA(())],
        compiler_params=_cparams(("arbitrary", "arbitrary")),
        name="moe_combine",
    )(idx4, ys, x1, ml, mc)


def _rope_swap_perm():
    idx = np.arange(C_ROPE, dtype=np.int32).reshape(2, 2, C_ROPE // 4)
    return idx[:, ::-1, :].reshape(-1)


def _rope_table():
    rows = SEQ // GRID_W
    row = jnp.broadcast_to(jnp.arange(rows, dtype=F32)[:, None], (rows, GRID_W)).reshape(-1)
    col = jnp.broadcast_to(jnp.arange(GRID_W, dtype=F32)[None, :], (rows, GRID_W)).reshape(-1)
    axis_dim = C_ROPE // 2
    inv_freq = ROPE_BASE ** (-jnp.arange(0, axis_dim, 2, dtype=F32) / axis_dim)
    n_freq = C_ROPE // 4
    ang = jnp.stack([row[:, None] * inv_freq, col[:, None] * inv_freq], axis=1)
    cos = jnp.broadcast_to(jnp.cos(ang)[:, :, None, :], (SEQ, 2, 2, n_freq)).reshape(SEQ, C_ROPE)
    sin = jnp.sin(ang)
    sin = jnp.stack([-sin, sin], axis=2).reshape(SEQ, C_ROPE)
    lat = jnp.concatenate([cos, sin], axis=1)
    ctx = jnp.concatenate([jnp.ones((CTX_LEN, C_ROPE), F32), jnp.zeros((CTX_LEN, C_ROPE), F32)], axis=1)
    return jnp.concatenate([ctx, lat], axis=0)


def _layer_weights(i, w_in, w_gla_decay, b_gla_decay, w_uq, w_ukv, g_qr, g_kr, b_s):
    perm = _rope_swap_perm()
    o = [int(v) for v in np.cumsum((0, 1024, 1024, 512, 512, 1024, 1024, 32, C_Q_RANK, C_KV_RANK, C_ROPE, 3 * D_MODEL))]
    w = w_in[i]
    seg = lambda j: w[:, o[j]:o[j + 1]]
    u, va, qb, kb, vb, rb, dec, cq, ckv, kr, gates = (seg(j) for j in range(11))
    pad = jnp.zeros((D_MODEL, Z_BLK - (SM_DEC + 2 * B_GATE_RANK)), F32)
    w_in_ext = jnp.concatenate([gates, u, va, vb, rb, qb, kb, cq, ckv, kr, kr[:, perm], dec, pad], axis=1).astype(BF16)

    def dec_weight(j):
        wd = jnp.zeros((LANE, B_HEADS * B_DK), F32)
        return wd.at[j * B_GATE_RANK:(j + 1) * B_GATE_RANK].set(w_gla_decay[i, j]).astype(BF16)

    wq = w_uq[i].reshape(C_Q_RANK, C_HEADS, C_NOPE + C_ROPE)
    wq_ext = jnp.concatenate([wq, wq[:, :, C_NOPE:][:, :, perm]], axis=2).reshape(C_Q_RANK, C_HEADS * C_QK).astype(BF16)
    pack = lambda g: jnp.concatenate([g[i], g[i][perm]])[None, :]
    bs_full = jnp.repeat(b_s[i].T, A_CHUNK, axis=1)
    wkv = w_ukv[i].reshape(C_KV_RANK, C_HEADS, C_NOPE + C_DV)
    wkn = wkv[:, :, :C_NOPE].reshape(C_KV_RANK, C_HEADS * C_NOPE).astype(BF16)
    wvt = jnp.transpose(wkv[:, :, C_NOPE:], (1, 2, 0))
    wvt = jnp.concatenate([wvt, jnp.zeros((C_HEADS, VT_ROWS - C_DV, C_KV_RANK), F32)], axis=1).astype(BF16)
    return dict(w_in=w_in_ext, wdec_f=dec_weight(0), wdec_b=dec_weight(1),
                bdec_f=b_gla_decay[i, 0][None, :], bdec_b=b_gla_decay[i, 1][None, :],
                wq=wq_ext, wkn=wkn, wvt=wvt, gqr2=pack(g_qr), gkr2=pack(g_kr), bs_full=bs_full)


def kernel(x, c, ctx, c_ctx, w_mod, b_mod, g_norm_mix, g_norm_ffn, w_in, b_gate, w_s, b_s, w_gla_decay, b_gla_decay,
           g_gla_out, g_cq, g_ckv, w_uq, w_ukv, g_qn, g_qr, g_kn, g_kr, w_proj_a, w_proj_b, w_proj_c, w_out,
           w_router, w_e_gate, w_e_up, w_e_down):
    c8 = jnp.concatenate([c, c_ctx[None, :], jnp.zeros((8 - BATCH - 1, D_MODEL), F32)], axis=0)
    mod = _mod_table(c8, w_mod, b_mod).reshape(DEPTH, 8, N_MOD, D_MODEL)
    cs_tab = _rope_table()
    xc = jnp.concatenate([ctx, x], axis=1)
    row2 = lambda a: a[None, :]
    cap_lat = CAPACITY_FACTOR * SEQ // N_EXPERTS
    cap_ctx = CAPACITY_FACTOR * CTX_LEN // N_EXPERTS
    for i in range(DEPTH):
        lw = _layer_weights(i, w_in, w_gla_decay, b_gla_decay, w_uq, w_ukv, g_qr, g_kr, b_s)
        ml, mc = mod[i, :BATCH], mod[i, BATCH:BATCH + 1]
        h = _mod_norm(xc, row2(g_norm_mix[i]), ml, mc, 0, 1)
        z = _in_proj(h.reshape(BATCH * T_ALL, D_MODEL), lw['w_in']).reshape(BATCH, T_ALL, N_Z)
        ya = _mix_a(z, w_s[i].astype(BF16), lw['bs_full'])
        yb = _gla(z, lw['wdec_f'], lw['bdec_f'], lw['wdec_b'], lw['bdec_b'], row2(g_gla_out[i]))
        q, k, vt = _mla_prep(z, cs_tab, lw['wq'], lw['wkn'], lw['wvt'], row2(g_cq[i]), row2(g_ckv[i]),
                             row2(g_qn[i]), lw['gqr2'], row2(g_kn[i]), lw['gkr2'])
        yc = _attention(q, k, vt)
        w_r = jnp.concatenate([w_router[i], jnp.zeros((D_MODEL, LANE - N_EXPERTS), F32)], axis=1).astype(BF16)
        x1, h2a = _merge(ya, yb, yc, z, row2(b_gate[i]), w_proj_a[i].astype(BF16), w_proj_b[i].astype(BF16),
                         w_proj_c[i].astype(BF16), w_out[i].astype(BF16), xc, ml, mc, row2(g_norm_ffn[i]), w_r)
        caps = ((CTX_LEN, T_ALL, cap_lat),) + (((0, CTX_LEN, cap_ctx),) if i < DEPTH - 1 else ())
        idx = _route(h2a, caps)
        idx4 = idx.reshape(BATCH, N_EXPERTS, 1, idx.shape[-1])
        xs, gs = _gather(idx4, h2a)
        ys = _moe_ffn(i, xs, gs, w_e_gate, w_e_up, w_e_down)
        xc = _combine(idx4, ys, x1, ml, mc, cap_lat)
    return xc[:, CTX_LEN:]
```

```python
import functools

import jax
import jax.numpy as jnp
import numpy as np
from jax import lax
from jax.experimental import pallas as pl
from jax.experimental.pallas import tpu as pltpu

F32 = jnp.float32
BF16 = jnp.bfloat16

D_MODEL = 1024
BATCH = 4
SEQ = 4096
DEPTH = 2
CTX_LEN = 256
T_ALL = CTX_LEN + SEQ
GRID_W = 64
EPS = 1e-6
N_MOD = 6

A_GROUPS = 8
A_CHUNK = 128

B_HEADS = 4
B_DK = 128
B_DV = 256
B_GATE_RANK = 16
B_GATE_NORM = 16.0
B_CHUNK = 64

C_HEADS = 8
C_Q_RANK = 384
C_KV_RANK = 256
C_NOPE = 128
C_ROPE = 64
C_DV = 128
ROPE_BASE = 10000.0
C_QK = 2 * C_NOPE

N_EXPERTS = 16
CAPACITY_FACTOR = 2
D_EXPERT = 2048

Z_BLK = 1024
ZB_GATES, ZB_U, ZB_VA, ZB_VB, ZB_RB, ZB_QK, ZB_SMALL = 0, 3, 4, 5, 6, 7, 8
N_Z = 9 * Z_BLK
SM_CQ, SM_CKV, SM_KR, SM_DEC = 0, C_Q_RANK, C_Q_RANK + C_KV_RANK, C_Q_RANK + C_KV_RANK + 2 * C_ROPE

LOG2_E = 1.4426950408889634
LANE = 128
VMEM_LIMIT = 56 * 1024 * 1024

TM_ROWS = 544
TM_MM = 1088
TM_PREP = 256
TQ = 256
HEADS_PER_STEP = 4
KEY_BLOCK = 1152
VT_ROWS = C_DV + 16
TB_GLA = 256
TW_MOE = 256


def _cparams(sem):
    return pltpu.CompilerParams(dimension_semantics=sem, vmem_limit_bytes=VMEM_LIMIT)


def _rms(x, n):
    return x * lax.rsqrt(jnp.sum(x * x, axis=-1, keepdims=True) * (1.0 / n) + EPS)


def _silu(x):
    return x * (1.0 / (1.0 + jnp.exp(-x)))


def _sigmoid(x):
    return 1.0 / (1.0 + jnp.exp(-x))


def _row_is_ctx(tile_idx, tm):
    row = tile_idx * tm + lax.broadcasted_iota(jnp.int32, (tm, 1), 0)
    return row < CTX_LEN


def _pick_mod(is_ctx, ml_ref, mc_ref, k):
    return jnp.where(is_ctx, mc_ref[0, k:k + 1, :], ml_ref[0, k:k + 1, :])


def _mod_kernel(c_ref, w_ref, b_ref, o_ref):
    s = _silu(c_ref[...])
    o_ref[0] = jnp.dot(s.astype(BF16), w_ref[0].astype(BF16), preferred_element_type=F32) + b_ref[0]


def _mod_table(c8, w_mod, b_mod):
    tn = 512
    n = N_MOD * D_MODEL
    return pl.pallas_call(
        _mod_kernel,
        grid=(DEPTH, n // tn),
        in_specs=[
            pl.BlockSpec((8, D_MODEL), lambda l, j: (0, 0)),
            pl.BlockSpec((1, D_MODEL, tn), lambda l, j: (l, 0, j)),
            pl.BlockSpec((1, 1, tn), lambda l, j: (l, 0, j)),
        ],
        out_specs=pl.BlockSpec((1, 8, tn), lambda l, j: (l, 0, j)),
        out_shape=jax.ShapeDtypeStruct((DEPTH, 8, n), F32),
        compiler_params=_cparams(("parallel", "parallel")),
        name="mod_table",
    )(c8, w_mod, b_mod.reshape(DEPTH, 1, n))


def _norm_kernel(x_ref, g_ref, ml_ref, mc_ref, o_ref, *, tm, k_shift, k_scale):
    is_ctx = _row_is_ctx(pl.program_id(1), tm)
    y = _rms(x_ref[0], D_MODEL) * g_ref[...]
    y = y * (1.0 + _pick_mod(is_ctx, ml_ref, mc_ref, k_scale)) + _pick_mod(is_ctx, ml_ref, mc_ref, k_shift)
    o_ref[0] = y.astype(BF16)


def _mod_norm(x, g, ml, mc, k_shift, k_scale):
    tm = TM_ROWS
    return pl.pallas_call(
        functools.partial(_norm_kernel, tm=tm, k_shift=k_shift, k_scale=k_scale),
        grid=(BATCH, T_ALL // tm),
        in_specs=[
            pl.BlockSpec((1, tm, D_MODEL), lambda b, i: (b, i, 0)),
            pl.BlockSpec((1, D_MODEL), lambda b, i: (0, 0)),
            pl.BlockSpec((1, N_MOD, D_MODEL), lambda b, i: (b, 0, 0)),
            pl.BlockSpec((1, N_MOD, D_MODEL), lambda b, i: (0, 0, 0)),
        ],
        out_specs=pl.BlockSpec((1, tm, D_MODEL), lambda b, i: (b, i, 0)),
        out_shape=jax.ShapeDtypeStruct((BATCH, T_ALL, D_MODEL), BF16),
        compiler_params=_cparams(("parallel", "parallel")),
        name="mod_norm",
    )(x, g, ml, mc)


def _matmul_kernel(a_ref, w_ref, o_ref):
    o_ref[...] = jnp.dot(a_ref[...], w_ref[...], preferred_element_type=F32).astype(o_ref.dtype)


def _in_proj(h2d, w):
    m, k = h2d.shape
    n = w.shape[1]
    tm, tn = TM_MM, Z_BLK
    return pl.pallas_call(
        _matmul_kernel,
        grid=(m // tm, n // tn),
        in_specs=[
            pl.BlockSpec((tm, k), lambda i, j: (i, 0)),
            pl.BlockSpec((k, tn), lambda i, j: (0, j)),
        ],
        out_specs=pl.BlockSpec((tm, tn), lambda i, j: (i, j)),
        out_shape=jax.ShapeDtypeStruct((m, n), BF16),
        compiler_params=_cparams(("parallel", "parallel")),
        name="in_proj",
    )(h2d, w)


def _mix_a_kernel(u_ref, v_ref, ws_ref, bs_ref, o_ref, *, n_chunk):
    def chunk(c, carry):
        r0 = pl.multiple_of(c * A_CHUNK, A_CHUNK)
        for g in range(A_GROUPS):
            cols = slice(g * LANE, (g + 1) * LANE)
            v = v_ref[0, pl.ds(r0, A_CHUNK), cols].astype(F32)
            mu = jnp.mean(v, axis=-1, keepdims=True)
            d = v - mu
            ln = d * lax.rsqrt(jnp.mean(d * d, axis=-1, keepdims=True) + EPS)
            mixed = jnp.dot(ws_ref[g], ln.astype(BF16), preferred_element_type=F32) + bs_ref[:, cols]
            u = u_ref[0, pl.ds(r0, A_CHUNK), cols].astype(F32)
            o_ref[0, pl.ds(r0, A_CHUNK), cols] = (u * mixed).astype(BF16)
        return carry

    lax.fori_loop(0, n_chunk, chunk, 0)


def _mix_a(z, ws, bs_full):
    ta = T_ALL // 2
    return pl.pallas_call(
        functools.partial(_mix_a_kernel, n_chunk=ta // A_CHUNK),
        grid=(BATCH, T_ALL // ta),
        in_specs=[
            pl.BlockSpec((1, ta, Z_BLK), lambda b, i: (b, i, ZB_U)),
            pl.BlockSpec((1, ta, Z_BLK), lambda b, i: (b, i, ZB_VA)),
            pl.BlockSpec((A_GROUPS, A_CHUNK, A_CHUNK), lambda b, i: (0, 0, 0)),
            pl.BlockSpec((A_CHUNK, D_MODEL), lambda b, i: (0, 0)),
        ],
        out_specs=pl.BlockSpec((1, ta, D_MODEL), lambda b, i: (b, i, 0)),
        out_shape=jax.ShapeDtypeStruct((BATCH, T_ALL, D_MODEL), BF16),
        compiler_params=_cparams(("parallel", "parallel")),
        name="mix_a",
    )(z, z, ws, bs_full)


_NT = (((1,), (1,)), ((), ()))
_TN = (((0,), (0,)), ((), ()))


def _log_sigmoid(x):
    return jnp.minimum(x, 0.0) - jnp.log(1.0 + jnp.exp(-jnp.abs(x)))


def _exact_tri_sum(tri, la):
    hi = la.astype(BF16)
    r1 = la - hi.astype(F32)
    mid = r1.astype(BF16)
    lo = (r1 - mid.astype(F32)).astype(BF16)
    return (jnp.dot(tri, hi, preferred_element_type=F32)
            + jnp.dot(tri, mid, preferred_element_type=F32)
            + jnp.dot(tri, lo, preferred_element_type=F32))


def _gla_block(qk_ref, v_ref, sm_ref, wdec_ref, bdec_ref, st_ref, *, reverse, emit):
    n_chunk = TB_GLA // B_CHUNK
    shift = B_CHUNK.bit_length() - 1
    ii = lax.broadcasted_iota(jnp.int32, (TB_GLA, TB_GLA), 0)
    jj = lax.broadcasted_iota(jnp.int32, (TB_GLA, TB_GLA), 1)
    keep = ((ii >> shift) == (jj >> shift)) & ((jj >= ii) if reverse else (jj <= ii))
    edge = 0 if reverse else B_CHUNK - 1

    pre = jnp.dot(sm_ref[0, :, SM_DEC:SM_DEC + LANE], wdec_ref[...], preferred_element_type=F32) + bdec_ref[...]
    la = _log_sigmoid(pre) * (1.0 / B_GATE_NORM)
    bcum = _exact_tri_sum(jnp.where(keep, 1.0, 0.0).astype(BF16), la)
    blast = jnp.concatenate(
        [jnp.broadcast_to(bcum[c * B_CHUNK + edge:c * B_CHUNK + edge + 1], (B_CHUNK, B_HEADS * B_DK))
         for c in range(n_chunk)], axis=0)
    e_fwd, e_inv, e_rest = jnp.exp(bcum), jnp.exp(-bcum), jnp.exp(blast - bcum)
    e_tot = jnp.exp(blast)
    for h in range(B_HEADS):
        kc = slice(h * B_DK, (h + 1) * B_DK)
        vc = slice(h * B_DV, (h + 1) * B_DV)
        q = qk_ref[0, :, kc].astype(F32)
        k = qk_ref[0, :, B_HEADS * B_DK + h * B_DK:B_HEADS * B_DK + (h + 1) * B_DK].astype(F32)
        v = v_ref[0, :, vc]
        qd = ((q * B_DK ** -0.5) * e_fwd[:, kc]).astype(BF16)
        kinv = (k * e_inv[:, kc]).astype(BF16)
        kdec = (k * e_rest[:, kc]).astype(BF16)
        s = lax.dot_general(qd, kinv, _NT, preferred_element_type=F32)
        o_intra = jnp.dot(jnp.where(keep, s, 0.0).astype(BF16), v, preferred_element_type=F32)
        st = st_ref[h]
        for c in (range(n_chunk - 1, -1, -1) if reverse else range(n_chunk)):
            rows = slice(c * B_CHUNK, (c + 1) * B_CHUNK)
            emit(h, rows, o_intra[rows] + lax.dot_general(qd[rows], st.astype(BF16), _NT, preferred_element_type=F32))
            kv_t = lax.dot_general(v[rows], kdec[rows], _TN, preferred_element_type=F32)
            st = e_tot[c * B_CHUNK:c * B_CHUNK + 1, kc] * st + kv_t
        st_ref[h] = st


def _gla_fwd_kernel(qk_ref, v_ref, sm_ref, wdec_ref, bdec_ref, o_ref, st_ref):
    @pl.when(pl.program_id(1) == 0)
    def _():
        st_ref[...] = jnp.zeros_like(st_ref)

    def emit(h, rows, o):
        o_ref[0, rows, h * B_DV:(h + 1) * B_DV] = o

    _gla_block(qk_ref, v_ref, sm_ref, wdec_ref, bdec_ref, st_ref, reverse=False, emit=emit)


def _gla_bwd_kernel(qk_ref, v_ref, sm_ref, wdec_ref, bdec_ref, of_ref, r_ref, g_ref, y_ref, st_ref):
    @pl.when(pl.program_id(1) == 0)
    def _():
        st_ref[...] = jnp.zeros_like(st_ref)

    def emit(h, rows, o):
        vc = slice(h * B_DV, (h + 1) * B_DV)
        o = o + of_ref[0, rows, vc]
        on = _rms(o, B_DV) * g_ref[...]
        y_ref[0, rows, vc] = (on * _silu(r_ref[0, rows, vc].astype(F32))).astype(BF16)

    _gla_block(qk_ref, v_ref, sm_ref, wdec_ref, bdec_ref, st_ref, reverse=True, emit=emit)


def _bwd_block(i):
    nb = T_ALL // TB_GLA
    return jnp.where(i == 0, 0, nb - i)


def _gla(z, wdec_f, bdec_f, wdec_b, bdec_b, g_out):
    tb = TB_GLA
    nb = T_ALL // tb
    wide = B_HEADS * B_DV

    def zspec(blk, imap):
        return pl.BlockSpec((1, tb, Z_BLK), lambda b, i: (b, imap(i), blk))

    def const2(shape):
        return pl.BlockSpec(shape, lambda b, i: (0, 0))

    state = pltpu.VMEM((B_HEADS, B_DV, B_DK), F32)
    fwd = lambda i: i
    o_f = pl.pallas_call(
        _gla_fwd_kernel,
        grid=(BATCH, nb),
        in_specs=[zspec(ZB_QK, fwd), zspec(ZB_VB, fwd), zspec(ZB_SMALL, fwd),
                  const2((LANE, B_HEADS * B_DK)), const2((1, B_HEADS * B_DK))],
        out_specs=pl.BlockSpec((1, tb, wide), lambda b, i: (b, i, 0)),
        out_shape=jax.ShapeDtypeStruct((BATCH, T_ALL, wide), F32),
        scratch_shapes=[state],
        compiler_params=_cparams(("parallel", "arbitrary")),
        name="gla_fwd",
    )(z, z, z, wdec_f, bdec_f)
    return pl.pallas_call(
        _gla_bwd_kernel,
        grid=(BATCH, nb),
        in_specs=[zspec(ZB_QK, _bwd_block), zspec(ZB_VB, _bwd_block), zspec(ZB_SMALL, _bwd_block),
                  const2((LANE, B_HEADS * B_DK)), const2((1, B_HEADS * B_DK)),
                  pl.BlockSpec((1, tb, wide), lambda b, i: (b, _bwd_block(i), 0)),
                  zspec(ZB_RB, _bwd_block), const2((1, B_DV))],
        out_specs=pl.BlockSpec((1, tb, wide), lambda b, i: (b, _bwd_block(i), 0)),
        out_shape=jax.ShapeDtypeStruct((BATCH, T_ALL, wide), BF16),
        scratch_shapes=[state],
        compiler_params=_cparams(("parallel", "arbitrary")),
        name="gla_bwd",
    )(z, z, z, wdec_b, bdec_b, o_f, z, g_out)


def _mla_prep_kernel(sm_ref, cs_ref, wq_ref, wkn_ref, wvt_ref, gcq_ref, gckv_ref, gqn_ref, gqr_ref, gkn_ref, gkr_ref,
                     q_ref, k_ref, vt_ref):
    scale = (C_NOPE + C_ROPE) ** -0.5 * LOG2_E
    vt_row = lax.broadcasted_iota(jnp.int32, (VT_ROWS, 1), 0)
    cs = cs_ref[...]
    lane = lax.broadcasted_iota(jnp.int32, (1, LANE), 1)
    low_half = (lane < C_ROPE).astype(F32)

    def rope(pack, gain):
        t = _rms(pack, LANE) * gain * cs
        return t + pltpu.roll(t, C_ROPE, axis=1)

    cq = sm_ref[0, :, SM_CQ:SM_CQ + C_Q_RANK].astype(F32)
    cqn = (_rms(cq, C_Q_RANK) * gcq_ref[...]).astype(BF16)
    q_all = jnp.dot(cqn, wq_ref[...], preferred_element_type=F32)
    ckv = sm_ref[0, :, SM_CKV:SM_CKV + C_KV_RANK].astype(F32)
    ckvn = (_rms(ckv, C_KV_RANK) * gckv_ref[...]).astype(BF16)
    kn_all = jnp.dot(ckvn, wkn_ref[...], preferred_element_type=F32)
    k_rot = rope(sm_ref[0, :, SM_KR:SM_KR + LANE].astype(F32), gkr_ref[...]).astype(BF16)
    for h in range(C_HEADS):
        c0 = h * C_QK
        qn = _rms(q_all[:, c0:c0 + C_NOPE], C_NOPE) * gqn_ref[...]
        q_ref[0, h, :, 0:C_NOPE] = (qn * scale).astype(BF16)
        q_rot = rope(q_all[:, c0 + C_NOPE:c0 + C_QK], gqr_ref[...]) * low_half
        q_ref[0, h, :, C_NOPE:C_QK] = (q_rot * scale).astype(BF16)
        kn = _rms(kn_all[:, h * C_NOPE:(h + 1) * C_NOPE], C_NOPE) * gkn_ref[...]
        k_ref[0, h, :, 0:C_NOPE] = kn.astype(BF16)
        k_ref[0, h, :, C_NOPE:C_QK] = k_rot
        vt = lax.dot_general(wvt_ref[h], ckvn, _NT, preferred_element_type=F32)
        vt_ref[0, h] = jnp.where(vt_row == C_DV, 1.0, vt).astype(BF16)


def _mla_prep(z, cs_tab, wq, wkn, wvt, gcq, gckv, gqn, gqr2, gkn, gkr2):
    tm = TM_PREP

    def const2(shape):
        return pl.BlockSpec(shape, lambda b, i: (0, 0))

    hspec = lambda w: pl.BlockSpec((1, C_HEADS, tm, w), lambda b, i: (b, 0, i, 0))
    hshape = lambda w: jax.ShapeDtypeStruct((BATCH, C_HEADS, T_ALL, w), BF16)
    return pl.pallas_call(
        _mla_prep_kernel,
        grid=(BATCH, T_ALL // tm),
        in_specs=[
            pl.BlockSpec((1, tm, Z_BLK), lambda b, i: (b, i, ZB_SMALL)),
            pl.BlockSpec((tm, LANE), lambda b, i: (i, 0)),
            const2((C_Q_RANK, C_HEADS * C_QK)), const2((C_KV_RANK, C_HEADS * C_NOPE)),
            pl.BlockSpec((C_HEADS, VT_ROWS, C_KV_RANK), lambda b, i: (0, 0, 0)),
            const2((1, C_Q_RANK)), const2((1, C_KV_RANK)),
            const2((1, C_NOPE)), const2((1, LANE)), const2((1, C_NOPE)), const2((1, LANE)),
        ],
        out_specs=[hspec(C_QK), hspec(C_QK),
                   pl.BlockSpec((1, C_HEADS, VT_ROWS, tm), lambda b, i: (b, 0, 0, i))],
        out_shape=[hshape(C_QK), hshape(C_QK),
                   jax.ShapeDtypeStruct((BATCH, C_HEADS, VT_ROWS, T_ALL), BF16)],
        compiler_params=_cparams(("parallel", "parallel")),
        name="mla_prep",
    )(z, cs_tab, wq, wkn, wvt, gcq, gckv, gqn, gqr2, gkn, gkr2)


def _attn_kernel(q_ref, k_ref, vt_ref, o_ref):
    def max_over_keys(s):
        parts = [s[i * 8:(i + 1) * 8] for i in range(s.shape[0] // 8)]
        while len(parts) > 1:
            parts = [jnp.maximum(parts[i], parts[i + 1]) for i in range(0, len(parts) - 1, 2)] + parts[len(parts) & ~1:]
        return jnp.max(parts[0], axis=0, keepdims=True)

    def attend(n_keys):
        edges = list(range(0, n_keys, KEY_BLOCK)) + [n_keys]
        blocks = [slice(a, b) for a, b in zip(edges[:-1], edges[1:])]
        heads = range(HEADS_PER_STEP)
        s = [[lax.dot_general(k_ref[0, h, kb, :], q_ref[0, h], _NT, preferred_element_type=F32) for kb in blocks]
             for h in heads]
        for h in heads:
            m = functools.reduce(jnp.maximum, [max_over_keys(sb) for sb in s[h]])
            acc = None
            for kb, sb in zip(blocks, s[h]):
                p = jnp.exp2(sb - m).astype(BF16)
                part = jnp.dot(vt_ref[0, h, :, kb], p, preferred_element_type=F32)
                acc = part if acc is None else acc + part
            o = acc[0:C_DV] * (1.0 / acc[C_DV:C_DV + 1])
            o_ref[0, :, h * C_DV:(h + 1) * C_DV] = o.T.astype(BF16)

    @pl.when(pl.program_id(2) == 0)
    def _():
        attend(CTX_LEN)

    @pl.when(pl.program_id(2) > 0)
    def _():
        attend(T_ALL)


def _attention(q, k, vt):
    tq = TQ
    hs = HEADS_PER_STEP
    return pl.pallas_call(
        _attn_kernel,
        grid=(BATCH, C_HEADS // hs, T_ALL // tq),
        in_specs=[
            pl.BlockSpec((1, hs, tq, C_QK), lambda b, h, i: (b, h, i, 0)),
            pl.BlockSpec((1, hs, T_ALL, C_QK), lambda b, h, i: (b, h, 0, 0)),
            pl.BlockSpec((1, hs, VT_ROWS, T_ALL), lambda b, h, i: (b, h, 0, 0)),
        ],
        out_specs=pl.BlockSpec((1, tq, hs * C_DV), lambda b, h, i: (b, i, h)),
        out_shape=jax.ShapeDtypeStruct((BATCH, T_ALL, C_HEADS * C_DV), BF16),
        compiler_params=_cparams(("parallel", "parallel", "parallel")),
        name="mla_attention",
    )(q, k, vt)


def _merge_kernel(ya_ref, yb_ref, yc_ref, gt_ref, bg_ref, wa_ref, wb_ref, wc_ref, wo_ref, x_ref, ml_ref, mc_ref,
                  gn_ref, wr_ref, x1_ref, h2a_ref, *, tm):
    is_ctx = _row_is_ctx(pl.program_id(1), tm)
    merged = None
    for j, (y_ref, w_ref) in enumerate(((ya_ref, wa_ref), (yb_ref, wb_ref), (yc_ref, wc_ref))):
        cols = slice(j * D_MODEL, (j + 1) * D_MODEL)
        gate = _sigmoid(gt_ref[0, :, cols].astype(F32) + bg_ref[:, cols])
        term = gate * jnp.dot(y_ref[0], w_ref[...], preferred_element_type=F32)
        merged = term if merged is None else merged + term
    y = jnp.dot(merged.astype(BF16), wo_ref[...], preferred_element_type=F32)
    x1 = x_ref[0] + _pick_mod(is_ctx, ml_ref, mc_ref, 2) * y
    x1_ref[0] = x1
    h2 = _rms(x1, D_MODEL) * gn_ref[...]
    h2 = h2 * (1.0 + _pick_mod(is_ctx, ml_ref, mc_ref, 4)) + _pick_mod(is_ctx, ml_ref, mc_ref, 3)
    logits = jnp.dot(h2.astype(BF16), wr_ref[...], preferred_element_type=F32)
    lane = lax.broadcasted_iota(jnp.int32, (1, LANE), 1)
    logits = jnp.where(lane < N_EXPERTS, logits, -jnp.inf)
    e = jnp.exp(logits - jnp.max(logits, axis=-1, keepdims=True))
    h2a_ref[0, :, 0:D_MODEL] = h2
    h2a_ref[0, :, D_MODEL:D_MODEL + LANE] = e / jnp.sum(e, axis=-1, keepdims=True)


def _merge(ya, yb, yc, z, b_gate, wa, wb, wc, wo, x, ml, mc, g_ffn, w_router):
    tm = TM_ROWS
    row = lambda w: pl.BlockSpec((1, tm, w), lambda b, i: (b, i, 0))

    def const2(shape):
        return pl.BlockSpec(shape, lambda b, i: (0, 0))

    sq = const2((D_MODEL, D_MODEL))
    return pl.pallas_call(
        functools.partial(_merge_kernel, tm=tm),
        grid=(BATCH, T_ALL // tm),
        in_specs=[
            row(D_MODEL), row(D_MODEL), row(D_MODEL), row(3 * D_MODEL), const2((1, 3 * D_MODEL)),
            sq, sq, sq, sq, row(D_MODEL),
            pl.BlockSpec((1, N_MOD, D_MODEL), lambda b, i: (b, 0, 0)),
            pl.BlockSpec((1, N_MOD, D_MODEL), lambda b, i: (0, 0, 0)),
            const2((1, D_MODEL)), const2((D_MODEL, LANE)),
        ],
        out_specs=[row(D_MODEL), row(D_MODEL + LANE)],
        out_shape=[jax.ShapeDtypeStruct((BATCH, T_ALL, D_MODEL), F32),
                   jax.ShapeDtypeStruct((BATCH, T_ALL, D_MODEL + LANE), F32)],
        compiler_params=_cparams(("parallel", "parallel")),
        name="merge",
    )(ya, yb, yc, z, b_gate, wa, wb, wc, wo, x, ml, mc, g_ffn, w_router)


N_ROUTE_ROWS = BATCH * N_EXPERTS
N_BISECT = 150


def _count(mask):
    return jnp.sum(jnp.where(mask, 1.0, 0.0), axis=1, keepdims=True)


def _lane_cumsum_blocks(x01):
    r, n = x01.shape
    nblk = n // LANE
    x3 = jnp.concatenate([x01[:, j * LANE:(j + 1) * LANE] for j in range(nblk)], axis=0).astype(BF16)
    ii = lax.broadcasted_iota(jnp.int32, (LANE, LANE), 0)
    jj = lax.broadcasted_iota(jnp.int32, (LANE, LANE), 1)
    inc = jnp.dot(x3, jnp.where(ii <= jj, 1.0, 0.0).astype(BF16), preferred_element_type=F32)
    off = jnp.zeros((r, 1), F32)
    out = []
    for j in range(nblk):
        blk = inc[j * r:(j + 1) * r]
        out.append(blk + off)
        off = off + blk[:, LANE - 1:LANE]
    return out


def _select_top(a, cap):
    def bisect(_, c):
        thr, step = c
        cand = thr + step
        return jnp.where(_count(a >= cand) >= cap, cand, thr), step * 0.5

    thr, _ = lax.fori_loop(0, N_BISECT, bisect, (jnp.zeros((a.shape[0], 1), F32), jnp.ones((a.shape[0], 1), F32)))
    gt = a > thr
    eq = a == thr
    need = cap - _count(gt)
    eq_rank = jnp.concatenate(_lane_cumsum_blocks(jnp.where(eq, 1.0, 0.0)), axis=1)
    sel = gt | (eq & (eq_rank <= need))
    return _lane_cumsum_blocks(jnp.where(sel, 1.0, 0.0))


def _route_kernel(aff_ref, idx_ref, cum_ref, end_ref, *, caps):
    aff_t = jnp.concatenate([aff_ref[b].T[:N_EXPERTS] for b in range(BATCH)], axis=0)
    lane = lax.broadcasted_iota(jnp.int32, (1, LANE), 1)
    p0 = 0
    for (t_lo, t_hi, cap) in caps:
        nblk = (t_hi - t_lo) // LANE
        blocks = _select_top(aff_t[:, t_lo:t_hi], cap)
        ends = jnp.full((N_ROUTE_ROWS, LANE), float(cap), F32)
        for j, blk in enumerate(blocks):
            cum_ref[:, j, :] = blk
            ends = jnp.where(lane == j, blk[:, LANE - 1:LANE], ends)
        end_ref[...] = ends
        p_col = lax.broadcasted_iota(jnp.int32, (cap, 1), 0).astype(F32)

        def per_row(r, res):
            n_full = _count(end_ref[pl.ds(r, 1), :] <= p_col)
            cum = cum_ref[r][0:nblk]
            cum = jnp.concatenate([cum, jnp.zeros((LANE - nblk, LANE), F32)], axis=0)
            hi = jnp.floor(cum * (1.0 / 64.0))
            lo = cum - 64.0 * hi
            pick = jnp.where(lane.astype(F32) == n_full, 1.0, 0.0).astype(BF16)
            part = (64.0 * jnp.dot(pick, hi.astype(BF16), preferred_element_type=F32)
                    + jnp.dot(pick, lo.astype(BF16), preferred_element_type=F32))
            cnt = LANE * n_full + _count(part <= p_col)
            return jnp.where(lane == r, cnt, res)

        res = lax.fori_loop(0, N_ROUTE_ROWS, per_row, jnp.zeros((cap, LANE), F32))
        if cap % LANE:
            res = jnp.concatenate([res, jnp.zeros((LANE - cap % LANE, LANE), F32)], axis=0)
        idx_ref[:, p0:p0 + cap] = (res.T[:N_ROUTE_ROWS, :cap] + float(t_lo)).astype(jnp.int32)
        p0 += cap


def _route(h2a, caps):
    n_sel = sum(c[2] for c in caps)
    return pl.pallas_call(
        functools.partial(_route_kernel, caps=caps),
        grid=(1,),
        in_specs=[pl.BlockSpec((BATCH, T_ALL, LANE), lambda i: (0, 0, D_MODEL // LANE))],
        out_specs=pl.BlockSpec((N_ROUTE_ROWS, n_sel), lambda i: (0, 0)),
        out_shape=jax.ShapeDtypeStruct((N_ROUTE_ROWS, n_sel), jnp.int32),
        scratch_shapes=[pltpu.VMEM((N_ROUTE_ROWS, SEQ // LANE, LANE), F32), pltpu.VMEM((N_ROUTE_ROWS, LANE), F32)],
        compiler_params=_cparams(("arbitrary",)),
        name="moe_route",
    )(h2a)


def _row_gather_copy(h2a_hbm, buf, sem, b, t, p):
    return pltpu.make_async_copy(h2a_hbm.at[b, pl.ds(t, 1), :], buf.at[pl.ds(p, 1), :], sem)


def _gather_kernel(idx_ref, h2a_hbm, xs_ref, g_ref, buf, sem, *, n_sel):
    b = pl.program_id(0)
    e = pl.program_id(1)

    for p in range(n_sel):
        _row_gather_copy(h2a_hbm, buf, sem, b, idx_ref[0, 0, 0, p], p).start()

    def drain(p, carry):
        _row_gather_copy(h2a_hbm, buf, sem, b, 0, p).wait()
        return carry

    lax.fori_loop(0, n_sel, drain, 0, unroll=8)
    xs_ref[0] = buf[:, 0:D_MODEL].astype(BF16)
    lane = lax.broadcasted_iota(jnp.int32, (1, LANE), 1)
    g_ref[0] = jnp.sum(jnp.where(lane == e, buf[:, D_MODEL:D_MODEL + LANE], 0.0), axis=1, keepdims=True)


def _gather(idx4, h2a):
    n_sel = idx4.shape[-1]
    return pl.pallas_call(
        functools.partial(_gather_kernel, n_sel=n_sel),
        grid=(BATCH, N_EXPERTS),
        in_specs=[pl.BlockSpec((1, 1, 1, n_sel), lambda b, e: (b, e, 0, 0), memory_space=pltpu.SMEM),
                  pl.BlockSpec(memory_space=pl.ANY)],
        out_specs=[pl.BlockSpec((1, n_sel, D_MODEL), lambda b, e: (e, b, 0)),
                   pl.BlockSpec((1, n_sel, 1), lambda b, e: (e, b, 0))],
        out_shape=[jax.ShapeDtypeStruct((N_EXPERTS, BATCH * n_sel, D_MODEL), BF16),
                   jax.ShapeDtypeStruct((N_EXPERTS, BATCH * n_sel, 1), F32)],
        scratch_shapes=[pltpu.VMEM((n_sel, D_MODEL + LANE), F32), pltpu.SemaphoreType.DMA(())],
        compiler_params=_cparams(("arbitrary", "arbitrary")),
        name="moe_gather",
    )(idx4, h2a)


N_HID_TILES = D_EXPERT // TW_MOE
N_OUT_TILES = D_MODEL // TW_MOE


def _moe_kernel(x_ref, g_ref, wg_ref, wu_ref, wd_ref, o_ref, hid_ref):
    j = pl.program_id(1)

    @pl.when(j < N_HID_TILES)
    def _():
        x = x_ref[0]
        h = (_silu(jnp.dot(x, wg_ref[0, 0].astype(BF16), preferred_element_type=F32))
             * jnp.dot(x, wu_ref[0, 0].astype(BF16), preferred_element_type=F32))
        hid_ref[jnp.minimum(j, N_HID_TILES - 1)] = h.astype(BF16)

    @pl.when(j >= N_HID_TILES)
    def _():
        hid = jnp.concatenate([hid_ref[f] for f in range(N_HID_TILES)], axis=1)
        o_ref[0] = jnp.dot(hid, wd_ref[0, 0].astype(BF16), preferred_element_type=F32) * g_ref[0]


def _moe_ffn(layer, xs, gs, wg, wu, wd):
    n_rows = xs.shape[1]
    hid_tile = lambda e, j: (layer, e, 0, jnp.minimum(j, N_HID_TILES - 1))
    out_tile = lambda j: jnp.maximum(j - N_HID_TILES, 0)
    return pl.pallas_call(
        _moe_kernel,
        grid=(N_EXPERTS, N_HID_TILES + N_OUT_TILES),
        in_specs=[
            pl.BlockSpec((1, n_rows, D_MODEL), lambda e, j: (e, 0, 0)),
            pl.BlockSpec((1, n_rows, 1), lambda e, j: (e, 0, 0)),
            pl.BlockSpec((1, 1, D_MODEL, TW_MOE), hid_tile),
            pl.BlockSpec((1, 1, D_MODEL, TW_MOE), hid_tile),
            pl.BlockSpec((1, 1, D_EXPERT, TW_MOE), lambda e, j: (layer, e, 0, out_tile(j))),
        ],
        out_specs=pl.BlockSpec((1, n_rows, TW_MOE), lambda e, j: (e, 0, out_tile(j))),
        out_shape=jax.ShapeDtypeStruct((N_EXPERTS, n_rows, D_MODEL), F32),
        scratch_shapes=[pltpu.VMEM((N_HID_TILES, n_rows, TW_MOE), BF16)],
        compiler_params=_cparams(("parallel", "arbitrary")),
        name="moe_ffn",
    )(xs, gs, wg, wu, wd)


ROW_GROUP = 8


def _combine_kernel(idx_ref, ys_ref, x1_hbm, ml_ref, mc_ref, o_ref, sem, *, n_lat, n_sel, t0):
    b = pl.program_id(0)

    @pl.when(pl.program_id(1) == 0)
    def _():
        seed = pltpu.make_async_copy(x1_hbm.at[b, pl.ds(t0, T_ALL - t0), :], o_ref.at[0], sem)
        seed.start()
        seed.wait()

    def add_rows(p_lo, p_hi, gate):
        for p0 in range(p_lo, p_hi, ROW_GROUP):
            toks = [idx_ref[0, 0, 0, p0 + r] - t0 for r in range(ROW_GROUP)]
            new = [o_ref[0, pl.ds(toks[r], 1), :] + gate * ys_ref[0, p0 + r:p0 + r + 1, :] for r in range(ROW_GROUP)]
            for r in range(ROW_GROUP):
                o_ref[0, pl.ds(toks[r], 1), :] = new[r]

    add_rows(0, n_lat, ml_ref[0, 5:6, :])
    if n_sel > n_lat:
        add_rows(n_lat, n_sel, mc_ref[0, 5:6, :])


def _combine(idx4, ys, x1, ml, mc, n_lat):
    n_sel = idx4.shape[-1]
    assert n_lat % ROW_GROUP == 0 and (n_sel - n_lat) % ROW_GROUP == 0
    t0 = 0 if n_sel > n_lat else CTX_LEN
    return pl.pallas_call(
        functools.partial(_combine_kernel, n_lat=n_lat, n_sel=n_sel, t0=t0),
        grid=(BATCH, N_EXPERTS),
        in_specs=[pl.BlockSpec((1, 1, 1, n_sel), lambda b, e: (b, e, 0, 0), memory_space=pltpu.SMEM),
                  pl.BlockSpec((1, n_sel, D_MODEL), lambda b, e: (e, b, 0)),
                  pl.BlockSpec(memory_space=pl.ANY),
                  pl.BlockSpec((1, N_MOD, D_MODEL), lambda b, e: (b, 0, 0)),
                  pl.BlockSpec((1, N_MOD, D_MODEL), lambda b, e: (0, 0, 0))],
        out_specs=pl.BlockSpec((1, T_ALL - t0, D_MODEL), lambda b, e: (b, 0, 0)),
        out_shape=jax.ShapeDtypeStruct((BATCH, T_ALL - t0, D_MODEL), F32),
        scratch_shapes=[pltpu.SemaphoreType.DMA(())],
        compiler_params=_cparams(("arbitrary", "arbitrary")),
        name="moe_combine",
    )(idx4, ys, x1, ml, mc)


def _rope_swap_perm():
    idx = np.arange(C_ROPE, dtype=np.int32).reshape(2, 2, C_ROPE // 4)
    return idx[:, ::-1, :].reshape(-1)


def _rope_table():
    rows = SEQ // GRID_W
    row = jnp.broadcast_to(jnp.arange(rows, dtype=F32)[:, None], (rows, GRID_W)).reshape(-1)
    col = jnp.broadcast_to(jnp.arange(GRID_W, dtype=F32)[None, :], (rows, GRID_W)).reshape(-1)
    axis_dim = C_ROPE // 2
    inv_freq = ROPE_BASE ** (-jnp.arange(0, axis_dim, 2, dtype=F32) / axis_dim)
    n_freq = C_ROPE // 4
    ang = jnp.stack([row[:, None] * inv_freq, col[:, None] * inv_freq], axis=1)
    cos = jnp.broadcast_to(jnp.cos(ang)[:, :, None, :], (SEQ, 2, 2, n_freq)).reshape(SEQ, C_ROPE)
    sin = jnp.sin(ang)
    sin = jnp.stack([-sin, sin], axis=2).reshape(SEQ, C_ROPE)
    lat = jnp.concatenate([cos, sin], axis=1)
    ctx = jnp.concatenate([jnp.ones((CTX_LEN, C_ROPE), F32), jnp.zeros((CTX_LEN, C_ROPE), F32)], axis=1)
    return jnp.concatenate([ctx, lat], axis=0)


def _layer_weights(i, w_in, w_gla_decay, b_gla_decay, w_uq, w_ukv, g_qr, g_kr, b_s):
    perm = _rope_swap_perm()
    o = [int(v) for v in np.cumsum((0, 1024, 1024, 512, 512, 1024, 1024, 32, C_Q_RANK, C_KV_RANK, C_ROPE, 3 * D_MODEL))]
    w = w_in[i]
    seg = lambda j: w[:, o[j]:o[j + 1]]
    u, va, qb, kb, vb, rb, dec, cq, ckv, kr, gates = (seg(j) for j in range(11))
    pad = jnp.zeros((D_MODEL, Z_BLK - (SM_DEC + 2 * B_GATE_RANK)), F32)
    w_in_ext = jnp.concatenate([gates, u, va, vb, rb, qb, kb, cq, ckv, kr, kr[:, perm], dec, pad], axis=1).astype(BF16)

    def dec_weight(j):
        wd = jnp.zeros((LANE, B_HEADS * B_DK), F32)
        return wd.at[j * B_GATE_RANK:(j + 1) * B_GATE_RANK].set(w_gla_decay[i, j]).astype(BF16)

    wq = w_uq[i].reshape(C_Q_RANK, C_HEADS, C_NOPE + C_ROPE)
    wq_ext = jnp.concatenate([wq, wq[:, :, C_NOPE:][:, :, perm]], axis=2).reshape(C_Q_RANK, C_HEADS * C_QK).astype(BF16)
    pack = lambda g: jnp.concatenate([g[i], g[i][perm]])[None, :]
    bs_full = jnp.repeat(b_s[i].T, A_CHUNK, axis=1)
    wkv = w_ukv[i].reshape(C_KV_RANK, C_HEADS, C_NOPE + C_DV)
    wkn = wkv[:, :, :C_NOPE].reshape(C_KV_RANK, C_HEADS * C_NOPE).astype(BF16)
    wvt = jnp.transpose(wkv[:, :, C_NOPE:], (1, 2, 0))
    wvt = jnp.concatenate([wvt, jnp.zeros((C_HEADS, VT_ROWS - C_DV, C_KV_RANK), F32)], axis=1).astype(BF16)
    return dict(w_in=w_in_ext, wdec_f=dec_weight(0), wdec_b=dec_weight(1),
                bdec_f=b_gla_decay[i, 0][None, :], bdec_b=b_gla_decay[i, 1][None, :],
                wq=wq_ext, wkn=wkn, wvt=wvt, gqr2=pack(g_qr), gkr2=pack(g_kr), bs_full=bs_full)


def kernel(x, c, ctx, c_ctx, w_mod, b_mod, g_norm_mix, g_norm_ffn, w_in, b_gate, w_s, b_s, w_gla_decay, b_gla_decay,
           g_gla_out, g_cq, g_ckv, w_uq, w_ukv, g_qn, g_qr, g_kn, g_kr, w_proj_a, w_proj_b, w_proj_c, w_out,
           w_router, w_e_gate, w_e_up, w_e_down):
    c8 = jnp.concatenate([c, c_ctx[None, :], jnp.zeros((8 - BATCH - 1, D_MODEL), F32)], axis=0)
    mod = _mod_table(c8, w_mod, b_mod).reshape(DEPTH, 8, N_MOD, D_MODEL)
    cs_tab = _rope_table()
    xc = jnp.concatenate([ctx, x], axis=1)
    row2 = lambda a: a[None, :]
    cap_lat = CAPACITY_FACTOR * SEQ // N_EXPERTS
    cap_ctx = CAPACITY_FACTOR * CTX_LEN // N_EXPERTS
    for i in range(DEPTH):
        lw = _layer_weights(i, w_in, w_gla_decay, b_gla_decay, w_uq, w_ukv, g_qr, g_kr, b_s)
        ml, mc = mod[i, :BATCH], mod[i, BATCH:BATCH + 1]
        h = _mod_norm(xc, row2(g_norm_mix[i]), ml, mc, 0, 1)
        z = _in_proj(h.reshape(BATCH * T_ALL, D_MODEL), lw['w_in']).reshape(BATCH, T_ALL, N_Z)
        ya = _mix_a(z, w_s[i].astype(BF16), lw['bs_full'])
        yb = _gla(z, lw['wdec_f'], lw['bdec_f'], lw['wdec_b'], lw['bdec_b'], row2(g_gla_out[i]))
        q, k, vt = _mla_prep(z, cs_tab, lw['wq'], lw['wkn'], lw['wvt'], row2(g_cq[i]), row2(g_ckv[i]),
                             row2(g_qn[i]), lw['gqr2'], row2(g_kn[i]), lw['gkr2'])
        yc = _attention(q, k, vt)
        w_r = jnp.concatenate([w_router[i], jnp.zeros((D_MODEL, LANE - N_EXPERTS), F32)], axis=1).astype(BF16)
        x1, h2a = _merge(ya, yb, yc, z, row2(b_gate[i]), w_proj_a[i].astype(BF16), w_proj_b[i].astype(BF16),
                         w_proj_c[i].astype(BF16), w_out[i].astype(BF16), xc, ml, mc, row2(g_norm_ffn[i]), w_r)
        caps = ((CTX_LEN, T_ALL, cap_lat),) + (((0, CTX_LEN, cap_ctx),) if i < DEPTH - 1 else ())
        idx = _route(h2a, caps)
        idx4 = idx.reshape(BATCH, N_EXPERTS, 1, idx.shape[-1])
        xs, gs = _gather(idx4, h2a)
        ys = _moe_ffn(i, xs, gs, w_e_gate, w_e_up, w_e_down)
        xc = _combine(idx4, ys, x1, ml, mc, cap_lat)
    return xc
```

```python
import functools

import jax
import jax.numpy as jnp
import numpy as np
from jax import lax
from jax.experimental import pallas as pl
from jax.experimental.pallas import tpu as pltpu

F32 = jnp.float32
BF16 = jnp.bfloat16

D_MODEL = 1024
BATCH = 4
SEQ = 4096
DEPTH = 2
CTX_LEN = 256
T_ALL = CTX_LEN + SEQ
GRID_W = 64
EPS = 1e-6
N_MOD = 6

A_GROUPS = 8
A_CHUNK = 128

B_HEADS = 4
B_DK = 128
B_DV = 256
B_GATE_RANK = 16
B_GATE_NORM = 16.0
B_CHUNK = 64

C_HEADS = 8
C_Q_RANK = 384
C_KV_RANK = 256
C_NOPE = 128
C_ROPE = 64
C_DV = 128
ROPE_BASE = 10000.0
C_QK = 2 * C_NOPE

N_EXPERTS = 16
CAPACITY_FACTOR = 2
D_EXPERT = 2048

Z_BLK = 1024
ZB_GATES, ZB_U, ZB_VA, ZB_VB, ZB_RB, ZB_QK, ZB_SMALL = 0, 3, 4, 5, 6, 7, 8
N_Z = 9 * Z_BLK
SM_CQ, SM_CKV, SM_KR, SM_DEC = 0, C_Q_RANK, C_Q_RANK + C_KV_RANK, C_Q_RANK + C_KV_RANK + 2 * C_ROPE

LOG2_E = 1.4426950408889634
LANE = 128
VMEM_LIMIT = 56 * 1024 * 1024

TM_ROWS = 544
TM_MM = 1088
TM_PREP = 256
TQ = 256
HEADS_PER_STEP = 4
KEY_BLOCK = 1152
VT_ROWS = C_DV + 16
TB_GLA = 256
TW_MOE = 256


def _cparams(sem):
    return pltpu.CompilerParams(dimension_semantics=sem, vmem_limit_bytes=VMEM_LIMIT)


def _rms(x, n):
    return x * lax.rsqrt(jnp.sum(x * x, axis=-1, keepdims=True) * (1.0 / n) + EPS)


def _silu(x):
    return x * (1.0 / (1.0 + jnp.exp(-x)))


def _sigmoid(x):
    return 1.0 / (1.0 + jnp.exp(-x))


def _row_is_ctx(tile_idx, tm):
    row = tile_idx * tm + lax.broadcasted_iota(jnp.int32, (tm, 1), 0)
    return row < CTX_LEN


def _pick_mod(is_ctx, ml_ref, mc_ref, k):
    return jnp.where(is_ctx, mc_ref[0, k:k + 1, :], ml_ref[0, k:k + 1, :])


def _mod_kernel(c_ref, w_ref, b_ref, o_ref):
    s = _silu(c_ref[...])
    o_ref[0] = jnp.dot(s.astype(BF16), w_ref[0].astype(BF16), preferred_element_type=F32) + b_ref[0]


def _mod_table(c8, w_mod, b_mod):
    tn = 512
    n = N_MOD * D_MODEL
    return pl.pallas_call(
        _mod_kernel,
        grid=(DEPTH, n // tn),
        in_specs=[
            pl.BlockSpec((8, D_MODEL), lambda l, j: (0, 0)),
            pl.BlockSpec((1, D_MODEL, tn), lambda l, j: (l, 0, j)),
            pl.BlockSpec((1, 1, tn), lambda l, j: (l, 0, j)),
        ],
        out_specs=pl.BlockSpec((1, 8, tn), lambda l, j: (l, 0, j)),
        out_shape=jax.ShapeDtypeStruct((DEPTH, 8, n), F32),
        compiler_params=_cparams(("parallel", "parallel")),
        name="mod_table",
    )(c8, w_mod, b_mod.reshape(DEPTH, 1, n))


def _norm_kernel(x_ref, g_ref, ml_ref, mc_ref, o_ref, *, tm, k_shift, k_scale):
    is_ctx = _row_is_ctx(pl.program_id(1), tm)
    y = _rms(x_ref[0], D_MODEL) * g_ref[...]
    y = y * (1.0 + _pick_mod(is_ctx, ml_ref, mc_ref, k_scale)) + _pick_mod(is_ctx, ml_ref, mc_ref, k_shift)
    o_ref[0] = y.astype(BF16)


def _mod_norm(x, g, ml, mc, k_shift, k_scale):
    tm = TM_ROWS
    return pl.pallas_call(
        functools.partial(_norm_kernel, tm=tm, k_shift=k_shift, k_scale=k_scale),
        grid=(BATCH, T_ALL // tm),
        in_specs=[
            pl.BlockSpec((1, tm, D_MODEL), lambda b, i: (b, i, 0)),
            pl.BlockSpec((1, D_MODEL), lambda b, i: (0, 0)),
            pl.BlockSpec((1, N_MOD, D_MODEL), lambda b, i: (b, 0, 0)),
            pl.BlockSpec((1, N_MOD, D_MODEL), lambda b, i: (0, 0, 0)),
        ],
        out_specs=pl.BlockSpec((1, tm, D_MODEL), lambda b, i: (b, i, 0)),
        out_shape=jax.ShapeDtypeStruct((BATCH, T_ALL, D_MODEL), BF16),
        compiler_params=_cparams(("parallel", "parallel")),
        name="mod_norm",
    )(x, g, ml, mc)


def _matmul_kernel(a_ref, w_ref, o_ref):
    o_ref[...] = jnp.dot(a_ref[...], w_ref[...], preferred_element_type=F32).astype(o_ref.dtype)


def _in_proj(h2d, w):
    m, k = h2d.shape
    n = w.shape[1]
    tm, tn = TM_MM, Z_BLK
    return pl.pallas_call(
        _matmul_kernel,
        grid=(m // tm, n // tn),
        in_specs=[
            pl.BlockSpec((tm, k), lambda i, j: (i, 0)),
            pl.BlockSpec((k, tn), lambda i, j: (0, j)),
        ],
        out_specs=pl.BlockSpec((tm, tn), lambda i, j: (i, j)),
        out_shape=jax.ShapeDtypeStruct((m, n), BF16),
        compiler_params=_cparams(("parallel", "parallel")),
        name="in_proj",
    )(h2d, w)


def _mix_a_kernel(u_ref, v_ref, ws_ref, bs_ref, o_ref, *, n_chunk):
    def chunk(c, carry):
        r0 = pl.multiple_of(c * A_CHUNK, A_CHUNK)
        for g in range(A_GROUPS):
            cols = slice(g * LANE, (g + 1) * LANE)
            v = v_ref[0, pl.ds(r0, A_CHUNK), cols].astype(F32)
            mu = jnp.mean(v, axis=-1, keepdims=True)
            d = v - mu
            ln = d * lax.rsqrt(jnp.mean(d * d, axis=-1, keepdims=True) + EPS)
            mixed = jnp.dot(ws_ref[g], ln.astype(BF16), preferred_element_type=F32) + bs_ref[:, cols]
            u = u_ref[0, pl.ds(r0, A_CHUNK), cols].astype(F32)
            o_ref[0, pl.ds(r0, A_CHUNK), cols] = (u * mixed).astype(BF16)
        return carry

    lax.fori_loop(0, n_chunk, chunk, 0)


def _mix_a(z, ws, bs_full):
    ta = T_ALL // 2
    return pl.pallas_call(
        functools.partial(_mix_a_kernel, n_chunk=ta // A_CHUNK),
        grid=(BATCH, T_ALL // ta),
        in_specs=[
            pl.BlockSpec((1, ta, Z_BLK), lambda b, i: (b, i, ZB_U)),
            pl.BlockSpec((1, ta, Z_BLK), lambda b, i: (b, i, ZB_VA)),
            pl.BlockSpec((A_GROUPS, A_CHUNK, A_CHUNK), lambda b, i: (0, 0, 0)),
            pl.BlockSpec((A_CHUNK, D_MODEL), lambda b, i: (0, 0)),
        ],
        out_specs=pl.BlockSpec((1, ta, D_MODEL), lambda b, i: (b, i, 0)),
        out_shape=jax.ShapeDtypeStruct((BATCH, T_ALL, D_MODEL), BF16),
        compiler_params=_cparams(("parallel", "parallel")),
        name="mix_a",
    )(z, z, ws, bs_full)


_NT = (((1,), (1,)), ((), ()))
_TN = (((0,), (0,)), ((), ()))


def _log_sigmoid(x):
    return jnp.minimum(x, 0.0) - jnp.log(1.0 + jnp.exp(-jnp.abs(x)))


def _exact_tri_sum(tri, la):
    hi = la.astype(BF16)
    r1 = la - hi.astype(F32)
    mid = r1.astype(BF16)
    lo = (r1 - mid.astype(F32)).astype(BF16)
    return (jnp.dot(tri, hi, preferred_element_type=F32)
            + jnp.dot(tri, mid, preferred_element_type=F32)
            + jnp.dot(tri, lo, preferred_element_type=F32))


def _gla_block(qk_ref, v_ref, sm_ref, wdec_ref, bdec_ref, st_ref, *, reverse, emit):
    n_chunk = TB_GLA // B_CHUNK
    shift = B_CHUNK.bit_length() - 1
    ii = lax.broadcasted_iota(jnp.int32, (TB_GLA, TB_GLA), 0)
    jj = lax.broadcasted_iota(jnp.int32, (TB_GLA, TB_GLA), 1)
    keep = ((ii >> shift) == (jj >> shift)) & ((jj >= ii) if reverse else (jj <= ii))
    edge = 0 if reverse else B_CHUNK - 1

    pre = jnp.dot(sm_ref[0, :, SM_DEC:SM_DEC + LANE], wdec_ref[...], preferred_element_type=F32) + bdec_ref[...]
    la = _log_sigmoid(pre) * (1.0 / B_GATE_NORM)
    bcum = _exact_tri_sum(jnp.where(keep, 1.0, 0.0).astype(BF16), la)
    blast = jnp.concatenate(
        [jnp.broadcast_to(bcum[c * B_CHUNK + edge:c * B_CHUNK + edge + 1], (B_CHUNK, B_HEADS * B_DK))
         for c in range(n_chunk)], axis=0)
    e_fwd, e_inv, e_rest = jnp.exp(bcum), jnp.exp(-bcum), jnp.exp(blast - bcum)
    e_tot = jnp.exp(blast)
    for h in range(B_HEADS):
        kc = slice(h * B_DK, (h + 1) * B_DK)
        vc = slice(h * B_DV, (h + 1) * B_DV)
        q = qk_ref[0, :, kc].astype(F32)
        k = qk_ref[0, :, B_HEADS * B_DK + h * B_DK:B_HEADS * B_DK + (h + 1) * B_DK].astype(F32)
        v = v_ref[0, :, vc]
        qd = ((q * B_DK ** -0.5) * e_fwd[:, kc]).astype(BF16)
        kinv = (k * e_inv[:, kc]).astype(BF16)
        kdec = (k * e_rest[:, kc]).astype(BF16)
        s = lax.dot_general(qd, kinv, _NT, preferred_element_type=F32)
        o_intra = jnp.dot(jnp.where(keep, s, 0.0).astype(BF16), v, preferred_element_type=F32)
        st = st_ref[h]
        for c in (range(n_chunk - 1, -1, -1) if reverse else range(n_chunk)):
            rows = slice(c * B_CHUNK, (c + 1) * B_CHUNK)
            emit(h, rows, o_intra[rows] + lax.dot_general(qd[rows], st.astype(BF16), _NT, preferred_element_type=F32))
            kv_t = lax.dot_general(v[rows], kdec[rows], _TN, preferred_element_type=F32)
            st = e_tot[c * B_CHUNK:c * B_CHUNK + 1, kc] * st + kv_t
        st_ref[h] = st


def _gla_fwd_kernel(qk_ref, v_ref, sm_ref, wdec_ref, bdec_ref, o_ref, st_ref):
    @pl.when(pl.program_id(1) == 0)
    def _():
        st_ref[...] = jnp.zeros_like(st_ref)

    def emit(h, rows, o):
        o_ref[0, rows, h * B_DV:(h + 1) * B_DV] = o

    _gla_block(qk_ref, v_ref, sm_ref, wdec_ref, bdec_ref, st_ref, reverse=False, emit=emit)


def _gla_bwd_kernel(qk_ref, v_ref, sm_ref, wdec_ref, bdec_ref, of_ref, r_ref, g_ref, y_ref, st_ref):
    @pl.when(pl.program_id(1) == 0)
    def _():
        st_ref[...] = jnp.zeros_like(st_ref)

    def emit(h, rows, o):
        vc = slice(h * B_DV, (h + 1) * B_DV)
        o = o + of_ref[0, rows, vc]
        on = _rms(o, B_DV) * g_ref[...]
        y_ref[0, rows, vc] = (on * _silu(r_ref[0, rows, vc].astype(F32))).astype(BF16)

    _gla_block(qk_ref, v_ref, sm_ref, wdec_ref, bdec_ref, st_ref, reverse=True, emit=emit)


def _bwd_block(i):
    nb = T_ALL // TB_GLA
    return jnp.where(i == 0, 0, nb - i)


def _gla(z, wdec_f, bdec_f, wdec_b, bdec_b, g_out):
    tb = TB_GLA
    nb = T_ALL // tb
    wide = B_HEADS * B_DV

    def zspec(blk, imap):
        return pl.BlockSpec((1, tb, Z_BLK), lambda b, i: (b, imap(i), blk))

    def const2(shape):
        return pl.BlockSpec(shape, lambda b, i: (0, 0))

    state = pltpu.VMEM((B_HEADS, B_DV, B_DK), F32)
    fwd = lambda i: i
    o_f = pl.pallas_call(
        _gla_fwd_kernel,
        grid=(BATCH, nb),
        in_specs=[zspec(ZB_QK, fwd), zspec(ZB_VB, fwd), zspec(ZB_SMALL, fwd),
                  const2((LANE, B_HEADS * B_DK)), const2((1, B_HEADS * B_DK))],
        out_specs=pl.BlockSpec((1, tb, wide), lambda b, i: (b, i, 0)),
        out_shape=jax.ShapeDtypeStruct((BATCH, T_ALL, wide), F32),
        scratch_shapes=[state],
        compiler_params=_cparams(("parallel", "arbitrary")),
        name="gla_fwd",
    )(z, z, z, wdec_f, bdec_f)
    return pl.pallas_call(
        _gla_bwd_kernel,
        grid=(BATCH, nb),
        in_specs=[zspec(ZB_QK, _bwd_block), zspec(ZB_VB, _bwd_block), zspec(ZB_SMALL, _bwd_block),
                  const2((LANE, B_HEADS * B_DK)), const2((1, B_HEADS * B_DK)),
                  pl.BlockSpec((1, tb, wide), lambda b, i: (b, _bwd_block(i), 0)),
                  zspec(ZB_RB, _bwd_block), const2((1, B_DV))],
        out_specs=pl.BlockSpec((1, tb, wide), lambda b, i: (b, _bwd_block(i), 0)),
        out_shape=jax.ShapeDtypeStruct((BATCH, T_ALL, wide), BF16),
        scratch_shapes=[state],
        compiler_params=_cparams(("parallel", "arbitrary")),
        name="gla_bwd",
    )(z, z, z, wdec_b, bdec_b, o_f, z, g_out)


def _mla_prep_kernel(sm_ref, cs_ref, wq_ref, wkn_ref, wvt_ref, gcq_ref, gckv_ref, gqn_ref, gqr_ref, gkn_ref, gkr_ref,
                     q_ref, k_ref, vt_ref):
    scale = (C_NOPE + C_ROPE) ** -0.5 * LOG2_E
    vt_row = lax.broadcasted_iota(jnp.int32, (VT_ROWS, 1), 0)
    cs = cs_ref[...]
    lane = lax.broadcasted_iota(jnp.int32, (1, LANE), 1)
    low_half = (lane < C_ROPE).astype(F32)

    def rope(pack, gain):
        t = _rms(pack, LANE) * gain * cs
        return t + pltpu.roll(t, C_ROPE, axis=1)

    cq = sm_ref[0, :, SM_CQ:SM_CQ + C_Q_RANK].astype(F32)
    cqn = (_rms(cq, C_Q_RANK) * gcq_ref[...]).astype(BF16)
    q_all = jnp.dot(cqn, wq_ref[...], preferred_element_type=F32)
    ckv = sm_ref[0, :, SM_CKV:SM_CKV + C_KV_RANK].astype(F32)
    ckvn = (_rms(ckv, C_KV_RANK) * gckv_ref[...]).astype(BF16)
    kn_all = jnp.dot(ckvn, wkn_ref[...], preferred_element_type=F32)
    k_rot = rope(sm_ref[0, :, SM_KR:SM_KR + LANE].astype(F32), gkr_ref[...]).astype(BF16)
    for h in range(C_HEADS):
        c0 = h * C_QK
        qn = _rms(q_all[:, c0:c0 + C_NOPE], C_NOPE) * gqn_ref[...]
        q_ref[0, h, :, 0:C_NOPE] = (qn * scale).astype(BF16)
        q_rot = rope(q_all[:, c0 + C_NOPE:c0 + C_QK], gqr_ref[...]) * low_half
        q_ref[0, h, :, C_NOPE:C_QK] = (q_rot * scale).astype(BF16)
        kn = _rms(kn_all[:, h * C_NOPE:(h + 1) * C_NOPE], C_NOPE) * gkn_ref[...]
        k_ref[0, h, :, 0:C_NOPE] = kn.astype(BF16)
        k_ref[0, h, :, C_NOPE:C_QK] = k_rot
        vt = lax.dot_general(wvt_ref[h], ckvn, _NT, preferred_element_type=F32)
        vt_ref[0, h] = jnp.where(vt_row == C_DV, 1.0, vt).astype(BF16)


def _mla_prep(z, cs_tab, wq, wkn, wvt, gcq, gckv, gqn, gqr2, gkn, gkr2):
    tm = TM_PREP

    def const2(shape):
        return pl.BlockSpec(shape, lambda b, i: (0, 0))

    hspec = lambda w: pl.BlockSpec((1, C_HEADS, tm, w), lambda b, i: (b, 0, i, 0))
    hshape = lambda w: jax.ShapeDtypeStruct((BATCH, C_HEADS, T_ALL, w), BF16)
    return pl.pallas_call(
        _mla_prep_kernel,
        grid=(BATCH, T_ALL // tm),
        in_specs=[
            pl.BlockSpec((1, tm, Z_BLK), lambda b, i: (b, i, ZB_SMALL)),
            pl.BlockSpec((tm, LANE), lambda b, i: (i, 0)),
            const2((C_Q_RANK, C_HEADS * C_QK)), const2((C_KV_RANK, C_HEADS * C_NOPE)),
            pl.BlockSpec((C_HEADS, VT_ROWS, C_KV_RANK), lambda b, i: (0, 0, 0)),
            const2((1, C_Q_RANK)), const2((1, C_KV_RANK)),
            const2((1, C_NOPE)), const2((1, LANE)), const2((1, C_NOPE)), const2((1, LANE)),
        ],
        out_specs=[hspec(C_QK), hspec(C_QK),
                   pl.BlockSpec((1, C_HEADS, VT_ROWS, tm), lambda b, i: (b, 0, 0, i))],
        out_shape=[hshape(C_QK), hshape(C_QK),
                   jax.ShapeDtypeStruct((BATCH, C_HEADS, VT_ROWS, T_ALL), BF16)],
        compiler_params=_cparams(("parallel", "parallel")),
        name="mla_prep",
    )(z, cs_tab, wq, wkn, wvt, gcq, gckv, gqn, gqr2, gkn, gkr2)


def _attn_kernel(q_ref, k_ref, vt_ref, o_ref):
    def max_over_keys(s):
        parts = [s[i * 8:(i + 1) * 8] for i in range(s.shape[0] // 8)]
        while len(parts) > 1:
            parts = [jnp.maximum(parts[i], parts[i + 1]) for i in range(0, len(parts) - 1, 2)] + parts[len(parts) & ~1:]
        return jnp.max(parts[0], axis=0, keepdims=True)

    def attend(n_keys):
        edges = list(range(0, n_keys, KEY_BLOCK)) + [n_keys]
        blocks = [slice(a, b) for a, b in zip(edges[:-1], edges[1:])]
        heads = range(HEADS_PER_STEP)
        s = [[lax.dot_general(k_ref[0, h, kb, :], q_ref[0, h], _NT, preferred_element_type=F32) for kb in blocks]
             for h in heads]
        for h in heads:
            m = functools.reduce(jnp.maximum, [max_over_keys(sb) for sb in s[h]])
            acc = None
            for kb, sb in zip(blocks, s[h]):
                p = jnp.exp2(sb - m).astype(BF16)
                part = jnp.dot(vt_ref[0, h, :, kb], p, preferred_element_type=F32)
                acc = part if acc is None else acc + part
            o = acc[0:C_DV] * (1.0 / acc[C_DV:C_DV + 1])
            o_ref[0, :, h * C_DV:(h + 1) * C_DV] = o.T.astype(BF16)

    @pl.when(pl.program_id(2) == 0)
    def _():
        attend(CTX_LEN)

    @pl.when(pl.program_id(2) > 0)
    def _():
        attend(T_ALL)


def _attention(q, k, vt):
    tq = TQ
    hs = HEADS_PER_STEP
    return pl.pallas_call(
        _attn_kernel,
        grid=(BATCH, C_HEADS // hs, T_ALL // tq),
        in_specs=[
            pl.BlockSpec((1, hs, tq, C_QK), lambda b, h, i: (b, h, i, 0)),
            pl.BlockSpec((1, hs, T_ALL, C_QK), lambda b, h, i: (b, h, 0, 0)),
            pl.BlockSpec((1, hs, VT_ROWS, T_ALL), lambda b, h, i: (b, h, 0, 0)),
        ],
        out_specs=pl.BlockSpec((1, tq, hs * C_DV), lambda b, h, i: (b, i, h)),
        out_shape=jax.ShapeDtypeStruct((BATCH, T_ALL, C_HEADS * C_DV), BF16),
        compiler_params=_cparams(("parallel", "parallel", "parallel")),
        name="mla_attention",
    )(q, k, vt)


def _merge_kernel(ya_ref, yb_ref, yc_ref, gt_ref, bg_ref, wa_ref, wb_ref, wc_ref, wo_ref, x_ref, ml_ref, mc_ref,
                  gn_ref, wr_ref, x1_ref, h2a_ref, *, tm):
    is_ctx = _row_is_ctx(pl.program_id(1), tm)
    merged = None
    for j, (y_ref, w_ref) in enumerate(((ya_ref, wa_ref), (yb_ref, wb_ref), (yc_ref, wc_ref))):
        cols = slice(j * D_MODEL, (j + 1) * D_MODEL)
        gate = _sigmoid(gt_ref[0, :, cols].astype(F32) + bg_ref[:, cols])
        term = gate * jnp.dot(y_ref[0], w_ref[...], preferred_element_type=F32)
        merged = term if merged is None else merged + term
    y = jnp.dot(merged.astype(BF16), wo_ref[...], preferred_element_type=F32)
    x1 = x_ref[0] + _pick_mod(is_ctx, ml_ref, mc_ref, 2) * y
    x1_ref[0] = x1
    h2 = _rms(x1, D_MODEL) * gn_ref[...]
    h2 = h2 * (1.0 + _pick_mod(is_ctx, ml_ref, mc_ref, 4)) + _pick_mod(is_ctx, ml_ref, mc_ref, 3)
    logits = jnp.dot(h2.astype(BF16), wr_ref[...], preferred_element_type=F32)
    lane = lax.broadcasted_iota(jnp.int32, (1, LANE), 1)
    logits = jnp.where(lane < N_EXPERTS, logits, -jnp.inf)
    e = jnp.exp(logits - jnp.max(logits, axis=-1, keepdims=True))
    h2a_ref[0, :, 0:D_MODEL] = h2
    h2a_ref[0, :, D_MODEL:D_MODEL + LANE] = e / jnp.sum(e, axis=-1, keepdims=True)


def _merge(ya, yb, yc, z, b_gate, wa, wb, wc, wo, x, ml, mc, g_ffn, w_router):
    tm = TM_ROWS
    row = lambda w: pl.BlockSpec((1, tm, w), lambda b, i: (b, i, 0))

    def const2(shape):
        return pl.BlockSpec(shape, lambda b, i: (0, 0))

    sq = const2((D_MODEL, D_MODEL))
    return pl.pallas_call(
        functools.partial(_merge_kernel, tm=tm),
        grid=(BATCH, T_ALL // tm),
        in_specs=[
            row(D_MODEL), row(D_MODEL), row(D_MODEL), row(3 * D_MODEL), const2((1, 3 * D_MODEL)),
            sq, sq, sq, sq, row(D_MODEL),
            pl.BlockSpec((1, N_MOD, D_MODEL), lambda b, i: (b, 0, 0)),
            pl.BlockSpec((1, N_MOD, D_MODEL), lambda b, i: (0, 0, 0)),
            const2((1, D_MODEL)), const2((D_MODEL, LANE)),
        ],
        out_specs=[row(D_MODEL), row(D_MODEL + LANE)],
        out_shape=[jax.ShapeDtypeStruct((BATCH, T_ALL, D_MODEL), F32),
                   jax.ShapeDtypeStruct((BATCH, T_ALL, D_MODEL + LANE), F32)],
        compiler_params=_cparams(("parallel", "parallel")),
        name="merge",
    )(ya, yb, yc, z, b_gate, wa, wb, wc, wo, x, ml, mc, g_ffn, w_router)


N_ROUTE_ROWS = BATCH * N_EXPERTS
N_BISECT = 150


def _count(mask):
    return jnp.sum(jnp.where(mask, 1.0, 0.0), axis=1, keepdims=True)


def _lane_cumsum_blocks(x01):
    r, n = x01.shape
    nblk = n // LANE
    x3 = jnp.concatenate([x01[:, j * LANE:(j + 1) * LANE] for j in range(nblk)], axis=0).astype(BF16)
    ii = lax.broadcasted_iota(jnp.int32, (LANE, LANE), 0)
    jj = lax.broadcasted_iota(jnp.int32, (LANE, LANE), 1)
    inc = jnp.dot(x3, jnp.where(ii <= jj, 1.0, 0.0).astype(BF16), preferred_element_type=F32)
    off = jnp.zeros((r, 1), F32)
    out = []
    for j in range(nblk):
        blk = inc[j * r:(j + 1) * r]
        out.append(blk + off)
        off = off + blk[:, LANE - 1:LANE]
    return out


def _select_top(a, cap):
    def bisect(_, c):
        thr, step = c
        cand = thr + step
        return jnp.where(_count(a >= cand) >= cap, cand, thr), step * 0.5

    thr, _ = lax.fori_loop(0, N_BISECT, bisect, (jnp.zeros((a.shape[0], 1), F32), jnp.ones((a.shape[0], 1), F32)))
    gt = a > thr
    eq = a == thr
    need = cap - _count(gt)
    eq_rank = jnp.concatenate(_lane_cumsum_blocks(jnp.where(eq, 1.0, 0.0)), axis=1)
    sel = gt | (eq & (eq_rank <= need))
    return _lane_cumsum_blocks(jnp.where(sel, 1.0, 0.0))


def _route_kernel(aff_ref, idx_ref, cum_ref, end_ref, *, caps):
    aff_t = jnp.concatenate([aff_ref[b].T[:N_EXPERTS] for b in range(BATCH)], axis=0)
    lane = lax.broadcasted_iota(jnp.int32, (1, LANE), 1)
    p0 = 0
    for (t_lo, t_hi, cap) in caps:
        nblk = (t_hi - t_lo) // LANE
        blocks = _select_top(aff_t[:, t_lo:t_hi], cap)
        ends = jnp.full((N_ROUTE_ROWS, LANE), float(cap), F32)
        for j, blk in enumerate(blocks):
            cum_ref[:, j, :] = blk
            ends = jnp.where(lane == j, blk[:, LANE - 1:LANE], ends)
        end_ref[...] = ends
        p_col = lax.broadcasted_iota(jnp.int32, (cap, 1), 0).astype(F32)

        def per_row(r, res):
            n_full = _count(end_ref[pl.ds(r, 1), :] <= p_col)
            cum = cum_ref[r][0:nblk]
            cum = jnp.concatenate([cum, jnp.zeros((LANE - nblk, LANE), F32)], axis=0)
            hi = jnp.floor(cum * (1.0 / 64.0))
            lo = cum - 64.0 * hi
            pick = jnp.where(lane.astype(F32) == n_full, 1.0, 0.0).astype(BF16)
            part = (64.0 * jnp.dot(pick, hi.astype(BF16), preferred_element_type=F32)
                    + jnp.dot(pick, lo.astype(BF16), preferred_element_type=F32))
            cnt = LANE * n_full + _count(part <= p_col)
            return jnp.where(lane == r, cnt, res)

        res = lax.fori_loop(0, N_ROUTE_ROWS, per_row, jnp.zeros((cap, LANE), F32))
        if cap % LANE:
            res = jnp.concatenate([res, jnp.zeros((LANE - cap % LANE, LANE), F32)], axis=0)
        idx_ref[:, p0:p0 + cap] = (res.T[:N_ROUTE_ROWS, :cap] + float(t_lo)).astype(jnp.int32)
        p0 += cap


def _route(h2a, caps):
    n_sel = sum(c[2] for c in caps)
    return pl.pallas_call(
        functools.partial(_route_kernel, caps=caps),
        grid=(1,),
        in_specs=[pl.BlockSpec((BATCH, T_ALL, LANE), lambda i: (0, 0, D_MODEL // LANE))],
        out_specs=pl.BlockSpec((N_ROUTE_ROWS, n_sel), lambda i: (0, 0)),
        out_shape=jax.ShapeDtypeStruct((N_ROUTE_ROWS, n_sel), jnp.int32),
        scratch_shapes=[pltpu.VMEM((N_ROUTE_ROWS, SEQ // LANE, LANE), F32), pltpu.VMEM((N_ROUTE_ROWS, LANE), F32)],
        compiler_params=_cparams(("arbitrary",)),
        name="moe_route",
    )(h2a)


N_HID_TILES = D_EXPERT // TW_MOE
N_OUT_TILES = D_MODEL // TW_MOE


N_MOE_STEPS = N_HID_TILES + N_OUT_TILES


def _moe_kernel(idx_cur_ref, idx_nxt_ref, h2a_hbm, wg_ref, wu_ref, wd_ref, o_ref,
                stage_ref, x_ref, g_ref, hid_ref, sem, *, n_rows, rows_per_step):
    e = pl.program_id(0)
    j = pl.program_id(1)
    n_copies = n_rows

    def row_copy(idx_ref, r):
        return pltpu.make_async_copy(h2a_hbm.at[pl.ds(idx_ref[0, 0, r], 1), :], stage_ref.at[pl.ds(r, 1), :], sem)

    def issue_ahead():
        for k in range(rows_per_step):
            row_copy(idx_nxt_ref, j * rows_per_step + k).start()

    def wait_stage():
        def one(r, carry):
            row_copy(idx_cur_ref, r).wait()
            return carry

        lax.fori_loop(0, n_copies, one, 0, unroll=8)

    @pl.when((e == 0) & (j == 0))
    def _():
        def one(r, carry):
            row_copy(idx_cur_ref, r).start()
            return carry

        lax.fori_loop(0, n_copies, one, 0, unroll=8)

    @pl.when(j == 0)
    def _():
        wait_stage()
        x_ref[...] = stage_ref[:, 0:D_MODEL].astype(BF16)
        lane = lax.broadcasted_iota(jnp.int32, (1, LANE), 1)
        g_ref[...] = jnp.sum(jnp.where(lane == e, stage_ref[:, D_MODEL:D_MODEL + LANE], 0.0), axis=1, keepdims=True)

    @pl.when(j < N_HID_TILES)
    def _():
        issue_ahead()
        x = x_ref[...]
        h = (_silu(jnp.dot(x, wg_ref[0, 0].astype(BF16), preferred_element_type=F32))
             * jnp.dot(x, wu_ref[0, 0].astype(BF16), preferred_element_type=F32))
        hid_ref[jnp.minimum(j, N_HID_TILES - 1)] = h.astype(BF16)

    @pl.when(j >= N_HID_TILES)
    def _():
        hid = jnp.concatenate([hid_ref[f] for f in range(N_HID_TILES)], axis=1)
        o_ref[0] = jnp.dot(hid, wd_ref[0, 0].astype(BF16), preferred_element_type=F32) * g_ref[...]

    @pl.when((e == N_EXPERTS - 1) & (j == N_MOE_STEPS - 1))
    def _():
        wait_stage()


def _moe_ffn(layer, idx_rows, h2a_rows, wg, wu, wd):
    n_rows = idx_rows.shape[-1]
    assert n_rows % N_HID_TILES == 0
    rows_per_step = n_rows // N_HID_TILES
    hid_tile = lambda e, j: (layer, e, 0, jnp.minimum(j, N_HID_TILES - 1))
    out_tile = lambda j: jnp.maximum(j - N_HID_TILES, 0)
    idx_spec = lambda step: pl.BlockSpec((1, 1, n_rows), lambda e, j: (jnp.minimum(e + step, N_EXPERTS - 1), 0, 0),
                                         memory_space=pltpu.SMEM)
    return pl.pallas_call(
        functools.partial(_moe_kernel, n_rows=n_rows, rows_per_step=rows_per_step),
        grid=(N_EXPERTS, N_MOE_STEPS),
        in_specs=[
            idx_spec(0), idx_spec(1),
            pl.BlockSpec(memory_space=pl.ANY),
            pl.BlockSpec((1, 1, D_MODEL, TW_MOE), hid_tile),
            pl.BlockSpec((1, 1, D_MODEL, TW_MOE), hid_tile),
            pl.BlockSpec((1, 1, D_EXPERT, TW_MOE), lambda e, j: (layer, e, 0, out_tile(j))),
        ],
        out_specs=pl.BlockSpec((1, n_rows, TW_MOE), lambda e, j: (e, 0, out_tile(j))),
        out_shape=jax.ShapeDtypeStruct((N_EXPERTS, n_rows, D_MODEL), F32),
        scratch_shapes=[pltpu.VMEM((n_rows, D_MODEL + LANE), F32), pltpu.VMEM((n_rows, D_MODEL), BF16),
                        pltpu.VMEM((n_rows, 1), F32), pltpu.VMEM((N_HID_TILES, n_rows, TW_MOE), BF16),
                        pltpu.SemaphoreType.DMA(())],
        compiler_params=_cparams(("arbitrary", "arbitrary")),
        name="moe_ffn",
    )(idx_rows, idx_rows, h2a_rows, wg, wu, wd)


ROW_GROUP = 8


def _combine_kernel(idx_ref, ys_ref, x1_hbm, ml_ref, mc_ref, o_ref, sem, *, n_lat, n_sel, t0):
    b = pl.program_id(0)

    @pl.when(pl.program_id(1) == 0)
    def _():
        seed = pltpu.make_async_copy(x1_hbm.at[b, pl.ds(t0, T_ALL - t0), :], o_ref.at[0], sem)
        seed.start()
        seed.wait()

    def add_rows(p_lo, p_hi, gate):
        for p0 in range(p_lo, p_hi, ROW_GROUP):
            toks = [idx_ref[0, 0, 0, p0 + r] - t0 for r in range(ROW_GROUP)]
            new = [o_ref[0, pl.ds(toks[r], 1), :] + gate * ys_ref[0, p0 + r:p0 + r + 1, :] for r in range(ROW_GROUP)]
            for r in range(ROW_GROUP):
                o_ref[0, pl.ds(toks[r], 1), :] = new[r]

    add_rows(0, n_lat, ml_ref[0, 5:6, :])
    if n_sel > n_lat:
        add_rows(n_lat, n_sel, mc_ref[0, 5:6, :])


def _combine(idx4, ys, x1, ml, mc, n_lat):
    n_sel = idx4.shape[-1]
    assert n_lat % ROW_GROUP == 0 and (n_sel - n_lat) % ROW_GROUP == 0
    t0 = 0 if n_sel > n_lat else CTX_LEN
    return pl.pallas_call(
        functools.partial(_combine_kernel, n_lat=n_lat, n_sel=n_sel, t0=t0),
        grid=(BATCH, N_EXPERTS),
        in_specs=[pl.BlockSpec((1, 1, 1, n_sel), lambda b, e: (b, e, 0, 0), memory_space=pltpu.SMEM),
                  pl.BlockSpec((1, n_sel, D_MODEL), lambda b, e: (e, b, 0)),
                  pl.BlockSpec(memory_space=pl.ANY),
                  pl.BlockSpec((1, N_MOD, D_MODEL), lambda b, e: (b, 0, 0)),
                  pl.BlockSpec((1, N_MOD, D_MODEL), lambda b, e: (0, 0, 0))],
        out_specs=pl.BlockSpec((1, T_ALL - t0, D_MODEL), lambda b, e: (b, 0, 0)),
        out_shape=jax.ShapeDtypeStruct((BATCH, T_ALL - t0, D_MODEL), F32),
        scratch_shapes=[pltpu.SemaphoreType.DMA(())],
        compiler_params=_cparams(("arbitrary", "arbitrary")),
        name="moe_combine",
    )(idx4, ys, x1, ml, mc)


def _rope_swap_perm():
    idx = np.arange(C_ROPE, dtype=np.int32).reshape(2, 2, C_ROPE // 4)
    return idx[:, ::-1, :].reshape(-1)


def _rope_table():
    rows = SEQ // GRID_W
    row = jnp.broadcast_to(jnp.arange(rows, dtype=F32)[:, None], (rows, GRID_W)).reshape(-1)
    col = jnp.broadcast_to(jnp.arange(GRID_W, dtype=F32)[None, :], (rows, GRID_W)).reshape(-1)
    axis_dim = C_ROPE // 2
    inv_freq = ROPE_BASE ** (-jnp.arange(0, axis_dim, 2, dtype=F32) / axis_dim)
    n_freq = C_ROPE // 4
    ang = jnp.stack([row[:, None] * inv_freq, col[:, None] * inv_freq], axis=1)
    cos = jnp.broadcast_to(jnp.cos(ang)[:, :, None, :], (SEQ, 2, 2, n_freq)).reshape(SEQ, C_ROPE)
    sin = jnp.sin(ang)
    sin = jnp.stack([-sin, sin], axis=2).reshape(SEQ, C_ROPE)
    lat = jnp.concatenate([cos, sin], axis=1)
    ctx = jnp.concatenate([jnp.ones((CTX_LEN, C_ROPE), F32), jnp.zeros((CTX_LEN, C_ROPE), F32)], axis=1)
    return jnp.concatenate([ctx, lat], axis=0)


def _layer_weights(i, w_in, w_gla_decay, b_gla_decay, w_uq, w_ukv, g_qr, g_kr, b_s):
    perm = _rope_swap_perm()
    o = [int(v) for v in np.cumsum((0, 1024, 1024, 512, 512, 1024, 1024, 32, C_Q_RANK, C_KV_RANK, C_ROPE, 3 * D_MODEL))]
    w = w_in[i]
    seg = lambda j: w[:, o[j]:o[j + 1]]
    u, va, qb, kb, vb, rb, dec, cq, ckv, kr, gates = (seg(j) for j in range(11))
    pad = jnp.zeros((D_MODEL, Z_BLK - (SM_DEC + 2 * B_GATE_RANK)), F32)
    w_in_ext = jnp.concatenate([gates, u, va, vb, rb, qb, kb, cq, ckv, kr, kr[:, perm], dec, pad], axis=1).astype(BF16)

    def dec_weight(j):
        wd = jnp.zeros((LANE, B_HEADS * B_DK), F32)
        return wd.at[j * B_GATE_RANK:(j + 1) * B_GATE_RANK].set(w_gla_decay[i, j]).astype(BF16)

    wq = w_uq[i].reshape(C_Q_RANK, C_HEADS, C_NOPE + C_ROPE)
    wq_ext = jnp.concatenate([wq, wq[:, :, C_NOPE:][:, :, perm]], axis=2).reshape(C_Q_RANK, C_HEADS * C_QK).astype(BF16)
    pack = lambda g: jnp.concatenate([g[i], g[i][perm]])[None, :]
    bs_full = jnp.repeat(b_s[i].T, A_CHUNK, axis=1)
    wkv = w_ukv[i].reshape(C_KV_RANK, C_HEADS, C_NOPE + C_DV)
    wkn = wkv[:, :, :C_NOPE].reshape(C_KV_RANK, C_HEADS * C_NOPE).astype(BF16)
    wvt = jnp.transpose(wkv[:, :, C_NOPE:], (1, 2, 0))
    wvt = jnp.concatenate([wvt, jnp.zeros((C_HEADS, VT_ROWS - C_DV, C_KV_RANK), F32)], axis=1).astype(BF16)
    return dict(w_in=w_in_ext, wdec_f=dec_weight(0), wdec_b=dec_weight(1),
                bdec_f=b_gla_decay[i, 0][None, :], bdec_b=b_gla_decay[i, 1][None, :],
                wq=wq_ext, wkn=wkn, wvt=wvt, gqr2=pack(g_qr), gkr2=pack(g_kr), bs_full=bs_full)


def kernel(x, c, ctx, c_ctx, w_mod, b_mod, g_norm_mix, g_norm_ffn, w_in, b_gate, w_s, b_s, w_gla_decay, b_gla_decay,
           g_gla_out, g_cq, g_ckv, w_uq, w_ukv, g_qn, g_qr, g_kn, g_kr, w_proj_a, w_proj_b, w_proj_c, w_out,
           w_router, w_e_gate, w_e_up, w_e_down):
    c8 = jnp.concatenate([c, c_ctx[None, :], jnp.zeros((8 - BATCH - 1, D_MODEL), F32)], axis=0)
    mod = _mod_table(c8, w_mod, b_mod).reshape(DEPTH, 8, N_MOD, D_MODEL)
    cs_tab = _rope_table()
    xc = jnp.concatenate([ctx, x], axis=1)
    row2 = lambda a: a[None, :]
    cap_lat = CAPACITY_FACTOR * SEQ // N_EXPERTS
    cap_ctx = CAPACITY_FACTOR * CTX_LEN // N_EXPERTS
    for i in range(DEPTH):
        lw = _layer_weights(i, w_in, w_gla_decay, b_gla_decay, w_uq, w_ukv, g_qr, g_kr, b_s)
        ml, mc = mod[i, :BATCH], mod[i, BATCH:BATCH + 1]
        h = _mod_norm(xc, row2(g_norm_mix[i]), ml, mc, 0, 1)
        z = _in_proj(h.reshape(BATCH * T_ALL, D_MODEL), lw['w_in']).reshape(BATCH, T_ALL, N_Z)
        ya = _mix_a(z, w_s[i].astype(BF16), lw['bs_full'])
        yb = _gla(z, lw['wdec_f'], lw['bdec_f'], lw['wdec_b'], lw['bdec_b'], row2(g_gla_out[i]))
        q, k, vt = _mla_prep(z, cs_tab, lw['wq'], lw['wkn'], lw['wvt'], row2(g_cq[i]), row2(g_ckv[i]),
                             row2(g_qn[i]), lw['gqr2'], row2(g_kn[i]), lw['gkr2'])
        yc = _attention(q, k, vt)
        w_r = jnp.concatenate([w_router[i], jnp.zeros((D_MODEL, LANE - N_EXPERTS), F32)], axis=1).astype(BF16)
        x1, h2a = _merge(ya, yb, yc, z, row2(b_gate[i]), w_proj_a[i].astype(BF16), w_proj_b[i].astype(BF16),
                         w_proj_c[i].astype(BF16), w_out[i].astype(BF16), xc, ml, mc, row2(g_norm_ffn[i]), w_r)
        caps = ((CTX_LEN, T_ALL, cap_lat),) + (((0, CTX_LEN, cap_ctx),) if i < DEPTH - 1 else ())
        idx = _route(h2a, caps)
        n_sel = idx.shape[-1]
        idx4 = idx.reshape(BATCH, N_EXPERTS, 1, n_sel)
        idx_rows = idx.reshape(BATCH, N_EXPERTS, n_sel) + (jnp.arange(BATCH, dtype=jnp.int32) * T_ALL)[:, None, None]
        idx_rows = jnp.swapaxes(idx_rows, 0, 1).reshape(N_EXPERTS, 1, BATCH * n_sel)
        ys = _moe_ffn(i, idx_rows, h2a.reshape(BATCH * T_ALL, D_MODEL + LANE), w_e_gate, w_e_up, w_e_down)
        xc = _combine(idx4, ys, x1, ml, mc, cap_lat)
    return xc
```

```python
import functools

import jax
import jax.numpy as jnp
import numpy as np
from jax import lax
from jax.experimental import pallas as pl
from jax.experimental.pallas import tpu as pltpu

F32 = jnp.float32
BF16 = jnp.bfloat16

D_MODEL = 1024
BATCH = 4
SEQ = 4096
DEPTH = 2
CTX_LEN = 256
T_ALL = CTX_LEN + SEQ
GRID_W = 64
EPS = 1e-6
N_MOD = 6

A_GROUPS = 8
A_CHUNK = 128

B_HEADS = 4
B_DK = 128
B_DV = 256
B_GATE_RANK = 16
B_GATE_NORM = 16.0
B_CHUNK = 64

C_HEADS = 8
C_Q_RANK = 384
C_KV_RANK = 256
C_NOPE = 128
C_ROPE = 64
C_DV = 128
ROPE_BASE = 10000.0
C_QK = 2 * C_NOPE

N_EXPERTS = 16
CAPACITY_FACTOR = 2
D_EXPERT = 2048

Z_BLK = 1024
ZB_GATES, ZB_U, ZB_VA, ZB_VB, ZB_RB, ZB_QK, ZB_SMALL = 0, 3, 4, 5, 6, 7, 8
N_Z = 9 * Z_BLK
SM_CQ, SM_CKV, SM_KR, SM_DEC = 0, C_Q_RANK, C_Q_RANK + C_KV_RANK, C_Q_RANK + C_KV_RANK + 2 * C_ROPE

LOG2_E = 1.4426950408889634
LANE = 128
VMEM_LIMIT = 56 * 1024 * 1024

TM_ROWS = 544
TM_MM = 2176
TM_PREP = 256
TQ = 256
HEADS_PER_STEP = 4
KEY_BLOCK = 1152
VT_ROWS = C_DV + 16
TB_GLA = 256
TW_MOE = 256


def _cparams(sem):
    return pltpu.CompilerParams(dimension_semantics=sem, vmem_limit_bytes=VMEM_LIMIT)


def _rms(x, n):
    return x * lax.rsqrt(jnp.sum(x * x, axis=-1, keepdims=True) * (1.0 / n) + EPS)


def _silu(x):
    return x * (1.0 / (1.0 + jnp.exp(-x)))


def _sigmoid(x):
    return 1.0 / (1.0 + jnp.exp(-x))


def _row_is_ctx(tile_idx, tm):
    row = tile_idx * tm + lax.broadcasted_iota(jnp.int32, (tm, 1), 0)
    return row < CTX_LEN


def _pick_mod(is_ctx, ml_ref, mc_ref, k):
    return jnp.where(is_ctx, mc_ref[0, k:k + 1, :], ml_ref[0, k:k + 1, :])


def _mod_kernel(c_ref, w_ref, b_ref, o_ref):
    s = _silu(c_ref[...])
    o_ref[0] = jnp.dot(s.astype(BF16), w_ref[0].astype(BF16), preferred_element_type=F32) + b_ref[0]


def _mod_table(c8, w_mod, b_mod):
    tn = 512
    n = N_MOD * D_MODEL
    return pl.pallas_call(
        _mod_kernel,
        grid=(DEPTH, n // tn),
        in_specs=[
            pl.BlockSpec((8, D_MODEL), lambda l, j: (0, 0)),
            pl.BlockSpec((1, D_MODEL, tn), lambda l, j: (l, 0, j)),
            pl.BlockSpec((1, 1, tn), lambda l, j: (l, 0, j)),
        ],
        out_specs=pl.BlockSpec((1, 8, tn), lambda l, j: (l, 0, j)),
        out_shape=jax.ShapeDtypeStruct((DEPTH, 8, n), F32),
        compiler_params=_cparams(("parallel", "parallel")),
        name="mod_table",
    )(c8, w_mod, b_mod.reshape(DEPTH, 1, n))


def _norm_kernel(x_ref, g_ref, ml_ref, mc_ref, o_ref, *, tm, k_shift, k_scale):
    is_ctx = _row_is_ctx(pl.program_id(1), tm)
    y = _rms(x_ref[0], D_MODEL) * g_ref[...]
    y = y * (1.0 + _pick_mod(is_ctx, ml_ref, mc_ref, k_scale)) + _pick_mod(is_ctx, ml_ref, mc_ref, k_shift)
    o_ref[0] = y.astype(BF16)


def _mod_norm(x, g, ml, mc, k_shift, k_scale):
    tm = TM_ROWS
    return pl.pallas_call(
        functools.partial(_norm_kernel, tm=tm, k_shift=k_shift, k_scale=k_scale),
        grid=(BATCH, T_ALL // tm),
        in_specs=[
            pl.BlockSpec((1, tm, D_MODEL), lambda b, i: (b, i, 0)),
            pl.BlockSpec((1, D_MODEL), lambda b, i: (0, 0)),
            pl.BlockSpec((1, N_MOD, D_MODEL), lambda b, i: (b, 0, 0)),
            pl.BlockSpec((1, N_MOD, D_MODEL), lambda b, i: (0, 0, 0)),
        ],
        out_specs=pl.BlockSpec((1, tm, D_MODEL), lambda b, i: (b, i, 0)),
        out_shape=jax.ShapeDtypeStruct((BATCH, T_ALL, D_MODEL), BF16),
        compiler_params=_cparams(("parallel", "parallel")),
        name="mod_norm",
    )(x, g, ml, mc)


def _matmul_kernel(a_ref, w_ref, o_ref):
    o_ref[...] = jnp.dot(a_ref[...], w_ref[...], preferred_element_type=F32).astype(o_ref.dtype)


def _in_proj(h2d, w):
    m, k = h2d.shape
    n = w.shape[1]
    tm, tn = TM_MM, Z_BLK
    return pl.pallas_call(
        _matmul_kernel,
        grid=(m // tm, n // tn),
        in_specs=[
            pl.BlockSpec((tm, k), lambda i, j: (i, 0)),
            pl.BlockSpec((k, tn), lambda i, j: (0, j)),
        ],
        out_specs=pl.BlockSpec((tm, tn), lambda i, j: (i, j)),
        out_shape=jax.ShapeDtypeStruct((m, n), BF16),
        compiler_params=_cparams(("parallel", "parallel")),
        name="in_proj",
    )(h2d, w)


def _mix_a_kernel(u_ref, v_ref, ws_ref, bs_ref, o_ref, *, n_chunk):
    def chunk(c, carry):
        r0 = pl.multiple_of(c * A_CHUNK, A_CHUNK)
        for g in range(A_GROUPS):
            cols = slice(g * LANE, (g + 1) * LANE)
            v = v_ref[0, pl.ds(r0, A_CHUNK), cols].astype(F32)
            mu = jnp.mean(v, axis=-1, keepdims=True)
            d = v - mu
            ln = d * lax.rsqrt(jnp.mean(d * d, axis=-1, keepdims=True) + EPS)
            mixed = jnp.dot(ws_ref[g], ln.astype(BF16), preferred_element_type=F32) + bs_ref[:, cols]
            u = u_ref[0, pl.ds(r0, A_CHUNK), cols].astype(F32)
            o_ref[0, pl.ds(r0, A_CHUNK), cols] = (u * mixed).astype(BF16)
        return carry

    lax.fori_loop(0, n_chunk, chunk, 0)


def _mix_a(z, ws, bs_full):
    ta = T_ALL // 2
    return pl.pallas_call(
        functools.partial(_mix_a_kernel, n_chunk=ta // A_CHUNK),
        grid=(BATCH, T_ALL // ta),
        in_specs=[
            pl.BlockSpec((1, ta, Z_BLK), lambda b, i: (b, i, ZB_U)),
            pl.BlockSpec((1, ta, Z_BLK), lambda b, i: (b, i, ZB_VA)),
            pl.BlockSpec((A_GROUPS, A_CHUNK, A_CHUNK), lambda b, i: (0, 0, 0)),
            pl.BlockSpec((A_CHUNK, D_MODEL), lambda b, i: (0, 0)),
        ],
        out_specs=pl.BlockSpec((1, ta, D_MODEL), lambda b, i: (b, i, 0)),
        out_shape=jax.ShapeDtypeStruct((BATCH, T_ALL, D_MODEL), BF16),
        compiler_params=_cparams(("parallel", "parallel")),
        name="mix_a",
    )(z, z, ws, bs_full)


_NT = (((1,), (1,)), ((), ()))
_TN = (((0,), (0,)), ((), ()))


def _log_sigmoid(x):
    return jnp.minimum(x, 0.0) - jnp.log(1.0 + jnp.exp(-jnp.abs(x)))


def _exact_tri_sum(tri, la):
    hi = la.astype(BF16)
    r1 = la - hi.astype(F32)
    mid = r1.astype(BF16)
    lo = (r1 - mid.astype(F32)).astype(BF16)
    return (jnp.dot(tri, hi, preferred_element_type=F32)
            + jnp.dot(tri, mid, preferred_element_type=F32)
            + jnp.dot(tri, lo, preferred_element_type=F32))


def _gla_block(qk_ref, v_ref, sm_ref, wdec_ref, bdec_ref, st_ref, *, reverse, emit):
    n_chunk = TB_GLA // B_CHUNK
    shift = B_CHUNK.bit_length() - 1
    ii = lax.broadcasted_iota(jnp.int32, (TB_GLA, TB_GLA), 0)
    jj = lax.broadcasted_iota(jnp.int32, (TB_GLA, TB_GLA), 1)
    keep = ((ii >> shift) == (jj >> shift)) & ((jj >= ii) if reverse else (jj <= ii))
    edge = 0 if reverse else B_CHUNK - 1

    pre = jnp.dot(sm_ref[0, :, SM_DEC:SM_DEC + LANE], wdec_ref[...], preferred_element_type=F32) + bdec_ref[...]
    la = _log_sigmoid(pre) * (1.0 / B_GATE_NORM)
    bcum = _exact_tri_sum(jnp.where(keep, 1.0, 0.0).astype(BF16), la)
    blast = jnp.concatenate(
        [jnp.broadcast_to(bcum[c * B_CHUNK + edge:c * B_CHUNK + edge + 1], (B_CHUNK, B_HEADS * B_DK))
         for c in range(n_chunk)], axis=0)
    e_fwd, e_inv, e_rest = jnp.exp(bcum), jnp.exp(-bcum), jnp.exp(blast - bcum)
    e_tot = jnp.exp(blast)
    for h in range(B_HEADS):
        kc = slice(h * B_DK, (h + 1) * B_DK)
        vc = slice(h * B_DV, (h + 1) * B_DV)
        q = qk_ref[0, :, kc].astype(F32)
        k = qk_ref[0, :, B_HEADS * B_DK + h * B_DK:B_HEADS * B_DK + (h + 1) * B_DK].astype(F32)
        v = v_ref[0, :, vc]
        qd = ((q * B_DK ** -0.5) * e_fwd[:, kc]).astype(BF16)
        kinv = (k * e_inv[:, kc]).astype(BF16)
        kdec = (k * e_rest[:, kc]).astype(BF16)
        s = lax.dot_general(qd, kinv, _NT, preferred_element_type=F32)
        o_intra = jnp.dot(jnp.where(keep, s, 0.0).astype(BF16), v, preferred_element_type=F32)
        st = st_ref[h]
        for c in (range(n_chunk - 1, -1, -1) if reverse else range(n_chunk)):
            rows = slice(c * B_CHUNK, (c + 1) * B_CHUNK)
            emit(h, rows, o_intra[rows] + lax.dot_general(qd[rows], st.astype(BF16), _NT, preferred_element_type=F32))
            kv_t = lax.dot_general(v[rows], kdec[rows], _TN, preferred_element_type=F32)
            st = e_tot[c * B_CHUNK:c * B_CHUNK + 1, kc] * st + kv_t
        st_ref[h] = st


def _gla_fwd_kernel(qk_ref, v_ref, sm_ref, wdec_ref, bdec_ref, o_ref, st_ref):
    @pl.when(pl.program_id(1) == 0)
    def _():
        st_ref[...] = jnp.zeros_like(st_ref)

    def emit(h, rows, o):
        o_ref[0, rows, h * B_DV:(h + 1) * B_DV] = o

    _gla_block(qk_ref, v_ref, sm_ref, wdec_ref, bdec_ref, st_ref, reverse=False, emit=emit)


def _gla_bwd_kernel(qk_ref, v_ref, sm_ref, wdec_ref, bdec_ref, of_ref, r_ref, g_ref, y_ref, st_ref):
    @pl.when(pl.program_id(1) == 0)
    def _():
        st_ref[...] = jnp.zeros_like(st_ref)

    def emit(h, rows, o):
        vc = slice(h * B_DV, (h + 1) * B_DV)
        o = o + of_ref[0, rows, vc]
        on = _rms(o, B_DV) * g_ref[...]
        y_ref[0, rows, vc] = (on * _silu(r_ref[0, rows, vc].astype(F32))).astype(BF16)

    _gla_block(qk_ref, v_ref, sm_ref, wdec_ref, bdec_ref, st_ref, reverse=True, emit=emit)


def _bwd_block(i):
    nb = T_ALL // TB_GLA
    return jnp.where(i == 0, 0, nb - i)


def _gla(z, wdec_f, bdec_f, wdec_b, bdec_b, g_out):
    tb = TB_GLA
    nb = T_ALL // tb
    wide = B_HEADS * B_DV

    def zspec(blk, imap):
        return pl.BlockSpec((1, tb, Z_BLK), lambda b, i: (b, imap(i), blk))

    def const2(shape):
        return pl.BlockSpec(shape, lambda b, i: (0, 0))

    state = pltpu.VMEM((B_HEADS, B_DV, B_DK), F32)
    fwd = lambda i: i
    o_f = pl.pallas_call(
        _gla_fwd_kernel,
        grid=(BATCH, nb),
        in_specs=[zspec(ZB_QK, fwd), zspec(ZB_VB, fwd), zspec(ZB_SMALL, fwd),
                  const2((LANE, B_HEADS * B_DK)), const2((1, B_HEADS * B_DK))],
        out_specs=pl.BlockSpec((1, tb, wide), lambda b, i: (b, i, 0)),
        out_shape=jax.ShapeDtypeStruct((BATCH, T_ALL, wide), F32),
        scratch_shapes=[state],
        compiler_params=_cparams(("parallel", "arbitrary")),
        name="gla_fwd",
    )(z, z, z, wdec_f, bdec_f)
    return pl.pallas_call(
        _gla_bwd_kernel,
        grid=(BATCH, nb),
        in_specs=[zspec(ZB_QK, _bwd_block), zspec(ZB_VB, _bwd_block), zspec(ZB_SMALL, _bwd_block),
                  const2((LANE, B_HEADS * B_DK)), const2((1, B_HEADS * B_DK)),
                  pl.BlockSpec((1, tb, wide), lambda b, i: (b, _bwd_block(i), 0)),
                  zspec(ZB_RB, _bwd_block), const2((1, B_DV))],
        out_specs=pl.BlockSpec((1, tb, wide), lambda b, i: (b, _bwd_block(i), 0)),
        out_shape=jax.ShapeDtypeStruct((BATCH, T_ALL, wide), BF16),
        scratch_shapes=[state],
        compiler_params=_cparams(("parallel", "arbitrary")),
        name="gla_bwd",
    )(z, z, z, wdec_b, bdec_b, o_f, z, g_out)


def _mla_prep_kernel(sm_ref, cs_ref, wq_ref, wkn_ref, wvt_ref, gcq_ref, gckv_ref, gqn_ref, gqr_ref, gkn_ref, gkr_ref,
                     q_ref, k_ref, vt_ref):
    scale = (C_NOPE + C_ROPE) ** -0.5 * LOG2_E
    vt_row = lax.broadcasted_iota(jnp.int32, (VT_ROWS, 1), 0)
    cs = cs_ref[...]
    lane = lax.broadcasted_iota(jnp.int32, (1, LANE), 1)
    low_half = (lane < C_ROPE).astype(F32)

    def rope(pack, gain):
        t = _rms(pack, LANE) * gain * cs
        return t + pltpu.roll(t, C_ROPE, axis=1)

    cq = sm_ref[0, :, SM_CQ:SM_CQ + C_Q_RANK].astype(F32)
    cqn = (_rms(cq, C_Q_RANK) * gcq_ref[...]).astype(BF16)
    q_all = jnp.dot(cqn, wq_ref[...], preferred_element_type=F32)
    ckv = sm_ref[0, :, SM_CKV:SM_CKV + C_KV_RANK].astype(F32)
    ckvn = (_rms(ckv, C_KV_RANK) * gckv_ref[...]).astype(BF16)
    kn_all = jnp.dot(ckvn, wkn_ref[...], preferred_element_type=F32)
    k_rot = rope(sm_ref[0, :, SM_KR:SM_KR + LANE].astype(F32), gkr_ref[...]).astype(BF16)
    for h in range(C_HEADS):
        c0 = h * C_QK
        qn = _rms(q_all[:, c0:c0 + C_NOPE], C_NOPE) * gqn_ref[...]
        q_ref[0, h, :, 0:C_NOPE] = (qn * scale).astype(BF16)
        q_rot = rope(q_all[:, c0 + C_NOPE:c0 + C_QK], gqr_ref[...]) * low_half
        q_ref[0, h, :, C_NOPE:C_QK] = (q_rot * scale).astype(BF16)
        kn = _rms(kn_all[:, h * C_NOPE:(h + 1) * C_NOPE], C_NOPE) * gkn_ref[...]
        k_ref[0, h, :, 0:C_NOPE] = kn.astype(BF16)
        k_ref[0, h, :, C_NOPE:C_QK] = k_rot
        vt = lax.dot_general(wvt_ref[h], ckvn, _NT, preferred_element_type=F32)
        vt_ref[0, h] = jnp.where(vt_row == C_DV, 1.0, vt).astype(BF16)


def _mla_prep(z, cs_tab, wq, wkn, wvt, gcq, gckv, gqn, gqr2, gkn, gkr2):
    tm = TM_PREP

    def const2(shape):
        return pl.BlockSpec(shape, lambda b, i: (0, 0))

    hspec = lambda w: pl.BlockSpec((1, C_HEADS, tm, w), lambda b, i: (b, 0, i, 0))
    hshape = lambda w: jax.ShapeDtypeStruct((BATCH, C_HEADS, T_ALL, w), BF16)
    return pl.pallas_call(
        _mla_prep_kernel,
        grid=(BATCH, T_ALL // tm),
        in_specs=[
            pl.BlockSpec((1, tm, Z_BLK), lambda b, i: (b, i, ZB_SMALL)),
            pl.BlockSpec((tm, LANE), lambda b, i: (i, 0)),
            const2((C_Q_RANK, C_HEADS * C_QK)), const2((C_KV_RANK, C_HEADS * C_NOPE)),
            pl.BlockSpec((C_HEADS, VT_ROWS, C_KV_RANK), lambda b, i: (0, 0, 0)),
            const2((1, C_Q_RANK)), const2((1, C_KV_RANK)),
            const2((1, C_NOPE)), const2((1, LANE)), const2((1, C_NOPE)), const2((1, LANE)),
        ],
        out_specs=[hspec(C_QK), hspec(C_QK),
                   pl.BlockSpec((1, C_HEADS, VT_ROWS, tm), lambda b, i: (b, 0, 0, i))],
        out_shape=[hshape(C_QK), hshape(C_QK),
                   jax.ShapeDtypeStruct((BATCH, C_HEADS, VT_ROWS, T_ALL), BF16)],
        compiler_params=_cparams(("parallel", "parallel")),
        name="mla_prep",
    )(z, cs_tab, wq, wkn, wvt, gcq, gckv, gqn, gqr2, gkn, gkr2)


def _attn_kernel(q_ref, k_ref, vt_ref, o_ref):
    def max_over_keys(s):
        parts = [s[i * 8:(i + 1) * 8] for i in range(s.shape[0] // 8)]
        while len(parts) > 1:
            parts = [jnp.maximum(parts[i], parts[i + 1]) for i in range(0, len(parts) - 1, 2)] + parts[len(parts) & ~1:]
        return jnp.max(parts[0], axis=0, keepdims=True)

    def attend(n_keys):
        edges = list(range(0, n_keys, KEY_BLOCK)) + [n_keys]
        blocks = [slice(a, b) for a, b in zip(edges[:-1], edges[1:])]
        heads = range(HEADS_PER_STEP)
        s = [[lax.dot_general(k_ref[0, h, kb, :], q_ref[0, h], _NT, preferred_element_type=F32) for kb in blocks]
             for h in heads]
        for h in heads:
            m = functools.reduce(jnp.maximum, [max_over_keys(sb) for sb in s[h]])
            acc = None
            for kb, sb in zip(blocks, s[h]):
                p = jnp.exp2(sb - m).astype(BF16)
                part = jnp.dot(vt_ref[0, h, :, kb], p, preferred_element_type=F32)
                acc = part if acc is None else acc + part
            o = acc[0:C_DV] * (1.0 / acc[C_DV:C_DV + 1])
            o_ref[0, :, h * C_DV:(h + 1) * C_DV] = o.T.astype(BF16)

    @pl.when(pl.program_id(2) == 0)
    def _():
        attend(CTX_LEN)

    @pl.when(pl.program_id(2) > 0)
    def _():
        attend(T_ALL)


def _attention(q, k, vt):
    tq = TQ
    hs = HEADS_PER_STEP
    return pl.pallas_call(
        _attn_kernel,
        grid=(BATCH, C_HEADS // hs, T_ALL // tq),
        in_specs=[
            pl.BlockSpec((1, hs, tq, C_QK), lambda b, h, i: (b, h, i, 0)),
            pl.BlockSpec((1, hs, T_ALL, C_QK), lambda b, h, i: (b, h, 0, 0)),
            pl.BlockSpec((1, hs, VT_ROWS, T_ALL), lambda b, h, i: (b, h, 0, 0)),
        ],
        out_specs=pl.BlockSpec((1, tq, hs * C_DV), lambda b, h, i: (b, i, h)),
        out_shape=jax.ShapeDtypeStruct((BATCH, T_ALL, C_HEADS * C_DV), BF16),
        compiler_params=_cparams(("parallel", "parallel", "parallel")),
        name="mla_attention",
    )(q, k, vt)


def _merge_kernel(ya_ref, yb_ref, yc_ref, gt_ref, bg_ref, wa_ref, wb_ref, wc_ref, wo_ref, x_ref, ml_ref, mc_ref,
                  gn_ref, wr_ref, x1_ref, h2a_ref, *, tm):
    is_ctx = _row_is_ctx(pl.program_id(1), tm)
    merged = None
    for j, (y_ref, w_ref) in enumerate(((ya_ref, wa_ref), (yb_ref, wb_ref), (yc_ref, wc_ref))):
        cols = slice(j * D_MODEL, (j + 1) * D_MODEL)
        gate = _sigmoid(gt_ref[0, :, cols].astype(F32) + bg_ref[:, cols])
        term = gate * jnp.dot(y_ref[0], w_ref[...], preferred_element_type=F32)
        merged = term if merged is None else merged + term
    y = jnp.dot(merged.astype(BF16), wo_ref[...], preferred_element_type=F32)
    x1 = x_ref[0] + _pick_mod(is_ctx, ml_ref, mc_ref, 2) * y
    x1_ref[0] = x1
    h2 = _rms(x1, D_MODEL) * gn_ref[...]
    h2 = h2 * (1.0 + _pick_mod(is_ctx, ml_ref, mc_ref, 4)) + _pick_mod(is_ctx, ml_ref, mc_ref, 3)
    logits = jnp.dot(h2.astype(BF16), wr_ref[...], preferred_element_type=F32)
    lane = lax.broadcasted_iota(jnp.int32, (1, LANE), 1)
    logits = jnp.where(lane < N_EXPERTS, logits, -jnp.inf)
    e = jnp.exp(logits - jnp.max(logits, axis=-1, keepdims=True))
    h2a_ref[0, :, 0:D_MODEL] = h2
    h2a_ref[0, :, D_MODEL:D_MODEL + LANE] = e / jnp.sum(e, axis=-1, keepdims=True)


def _merge(ya, yb, yc, z, b_gate, wa, wb, wc, wo, x, ml, mc, g_ffn, w_router):
    tm = TM_ROWS
    row = lambda w: pl.BlockSpec((1, tm, w), lambda b, i: (b, i, 0))

    def const2(shape):
        return pl.BlockSpec(shape, lambda b, i: (0, 0))

    sq = const2((D_MODEL, D_MODEL))
    return pl.pallas_call(
        functools.partial(_merge_kernel, tm=tm),
        grid=(BATCH, T_ALL // tm),
        in_specs=[
            row(D_MODEL), row(D_MODEL), row(D_MODEL), row(3 * D_MODEL), const2((1, 3 * D_MODEL)),
            sq, sq, sq, sq, row(D_MODEL),
            pl.BlockSpec((1, N_MOD, D_MODEL), lambda b, i: (b, 0, 0)),
            pl.BlockSpec((1, N_MOD, D_MODEL), lambda b, i: (0, 0, 0)),
            const2((1, D_MODEL)), const2((D_MODEL, LANE)),
        ],
        out_specs=[row(D_MODEL), row(D_MODEL + LANE)],
        out_shape=[jax.ShapeDtypeStruct((BATCH, T_ALL, D_MODEL), F32),
                   jax.ShapeDtypeStruct((BATCH, T_ALL, D_MODEL + LANE), F32)],
        compiler_params=_cparams(("parallel", "parallel")),
        name="merge",
    )(ya, yb, yc, z, b_gate, wa, wb, wc, wo, x, ml, mc, g_ffn, w_router)


N_ROUTE_ROWS = BATCH * N_EXPERTS
N_BISECT = 150


def _count(mask):
    return jnp.sum(jnp.where(mask, 1.0, 0.0), axis=1, keepdims=True)


def _lane_cumsum_blocks(x01):
    r, n = x01.shape
    nblk = n // LANE
    x3 = jnp.concatenate([x01[:, j * LANE:(j + 1) * LANE] for j in range(nblk)], axis=0).astype(BF16)
    ii = lax.broadcasted_iota(jnp.int32, (LANE, LANE), 0)
    jj = lax.broadcasted_iota(jnp.int32, (LANE, LANE), 1)
    inc = jnp.dot(x3, jnp.where(ii <= jj, 1.0, 0.0).astype(BF16), preferred_element_type=F32)
    off = jnp.zeros((r, 1), F32)
    out = []
    for j in range(nblk):
        blk = inc[j * r:(j + 1) * r]
        out.append(blk + off)
        off = off + blk[:, LANE - 1:LANE]
    return out


def _select_top(a, cap):
    def more(c):
        return (c[2] > 0.0) & (c[3] < N_BISECT)

    def bisect(c):
        thr, step, _, i = c
        cand = thr + step
        thr = jnp.where(_count(a >= cand) >= cap, cand, thr)
        step = step * 0.5
        live = jnp.max(jnp.where(thr + step != thr, 1.0, 0.0))
        return thr, step, live, i + 1

    thr = lax.while_loop(more, bisect, (jnp.zeros((a.shape[0], 1), F32), jnp.ones((a.shape[0], 1), F32),
                                        jnp.float32(1.0), jnp.int32(0)))[0]
    gt = a > thr
    eq = a == thr
    need = cap - _count(gt)
    eq_rank = jnp.concatenate(_lane_cumsum_blocks(jnp.where(eq, 1.0, 0.0)), axis=1)
    sel = gt | (eq & (eq_rank <= need))
    return _lane_cumsum_blocks(jnp.where(sel, 1.0, 0.0))


def _route_kernel(aff_ref, idx_ref, cum_ref, end_ref, *, caps):
    aff_t = jnp.concatenate([aff_ref[b].T[:N_EXPERTS] for b in range(BATCH)], axis=0)
    lane = lax.broadcasted_iota(jnp.int32, (1, LANE), 1)
    p0 = 0
    for (t_lo, t_hi, cap) in caps:
        nblk = (t_hi - t_lo) // LANE
        blocks = _select_top(aff_t[:, t_lo:t_hi], cap)
        ends = jnp.full((N_ROUTE_ROWS, LANE), float(cap), F32)
        for j, blk in enumerate(blocks):
            cum_ref[:, j, :] = blk
            ends = jnp.where(lane == j, blk[:, LANE - 1:LANE], ends)
        end_ref[...] = ends
        p_col = lax.broadcasted_iota(jnp.int32, (cap, 1), 0).astype(F32)

        def per_row(r, res):
            n_full = _count(end_ref[pl.ds(r, 1), :] <= p_col)
            cum = cum_ref[r][0:nblk]
            cum = jnp.concatenate([cum, jnp.zeros((LANE - nblk, LANE), F32)], axis=0)
            hi = jnp.floor(cum * (1.0 / 64.0))
            lo = cum - 64.0 * hi
            pick = jnp.where(lane.astype(F32) == n_full, 1.0, 0.0).astype(BF16)
            part = (64.0 * jnp.dot(pick, hi.astype(BF16), preferred_element_type=F32)
                    + jnp.dot(pick, lo.astype(BF16), preferred_element_type=F32))
            cnt = LANE * n_full + _count(part <= p_col)
            return jnp.where(lane == r, cnt, res)

        res = lax.fori_loop(0, N_ROUTE_ROWS, per_row, jnp.zeros((cap, LANE), F32))
        if cap % LANE:
            res = jnp.concatenate([res, jnp.zeros((LANE - cap % LANE, LANE), F32)], axis=0)
        idx_ref[:, p0:p0 + cap] = (res.T[:N_ROUTE_ROWS, :cap] + float(t_lo)).astype(jnp.int32)
        p0 += cap


def _route(h2a, caps):
    n_sel = sum(c[2] for c in caps)
    return pl.pallas_call(
        functools.partial(_route_kernel, caps=caps),
        grid=(1,),
        in_specs=[pl.BlockSpec((BATCH, T_ALL, LANE), lambda i: (0, 0, D_MODEL // LANE))],
        out_specs=pl.BlockSpec((N_ROUTE_ROWS, n_sel), lambda i: (0, 0)),
        out_shape=jax.ShapeDtypeStruct((N_ROUTE_ROWS, n_sel), jnp.int32),
        scratch_shapes=[pltpu.VMEM((N_ROUTE_ROWS, SEQ // LANE, LANE), F32), pltpu.VMEM((N_ROUTE_ROWS, LANE), F32)],
        compiler_params=_cparams(("arbitrary",)),
        name="moe_route",
    )(h2a)


N_HID_TILES = D_EXPERT // TW_MOE
N_OUT_TILES = D_MODEL // TW_MOE


N_MOE_STEPS = N_HID_TILES + N_OUT_TILES


def _moe_kernel(idx_cur_ref, idx_nxt_ref, h2a_hbm, wg_ref, wu_ref, wd_ref, o_ref,
                stage_ref, x_ref, g_ref, hid_ref, sem, *, n_rows, rows_per_step):
    e = pl.program_id(0)
    j = pl.program_id(1)
    n_copies = n_rows

    def row_copy(idx_ref, r):
        return pltpu.make_async_copy(h2a_hbm.at[pl.ds(idx_ref[0, 0, r], 1), :], stage_ref.at[pl.ds(r, 1), :], sem)

    def issue_ahead():
        for k in range(rows_per_step):
            row_copy(idx_nxt_ref, j * rows_per_step + k).start()

    def wait_stage():
        def one(r, carry):
            row_copy(idx_cur_ref, r).wait()
            return carry

        lax.fori_loop(0, n_copies, one, 0, unroll=8)

    @pl.when((e == 0) & (j == 0))
    def _():
        def one(r, carry):
            row_copy(idx_cur_ref, r).start()
            return carry

        lax.fori_loop(0, n_copies, one, 0, unroll=8)

    @pl.when(j == 0)
    def _():
        wait_stage()
        x_ref[...] = stage_ref[:, 0:D_MODEL].astype(BF16)
        lane = lax.broadcasted_iota(jnp.int32, (1, LANE), 1)
        g_ref[...] = jnp.sum(jnp.where(lane == e, stage_ref[:, D_MODEL:D_MODEL + LANE], 0.0), axis=1, keepdims=True)

    @pl.when(j < N_HID_TILES)
    def _():
        issue_ahead()
        x = x_ref[...]
        h = (_silu(jnp.dot(x, wg_ref[0, 0].astype(BF16), preferred_element_type=F32))
             * jnp.dot(x, wu_ref[0, 0].astype(BF16), preferred_element_type=F32))
        hid_ref[jnp.minimum(j, N_HID_TILES - 1)] = h.astype(BF16)

    @pl.when(j >= N_HID_TILES)
    def _():
        hid = jnp.concatenate([hid_ref[f] for f in range(N_HID_TILES)], axis=1)
        o_ref[0] = jnp.dot(hid, wd_ref[0, 0].astype(BF16), preferred_element_type=F32) * g_ref[...]

    @pl.when((e == N_EXPERTS - 1) & (j == N_MOE_STEPS - 1))
    def _():
        wait_stage()


def _moe_ffn(layer, idx_rows, h2a_rows, wg, wu, wd):
    n_rows = idx_rows.shape[-1]
    assert n_rows % N_HID_TILES == 0
    rows_per_step = n_rows // N_HID_TILES
    hid_tile = lambda e, j: (layer, e, 0, jnp.minimum(j, N_HID_TILES - 1))
    out_tile = lambda j: jnp.maximum(j - N_HID_TILES, 0)
    idx_spec = lambda step: pl.BlockSpec((1, 1, n_rows), lambda e, j: (jnp.minimum(e + step, N_EXPERTS - 1), 0, 0),
                                         memory_space=pltpu.SMEM)
    return pl.pallas_call(
        functools.partial(_moe_kernel, n_rows=n_rows, rows_per_step=rows_per_step),
        grid=(N_EXPERTS, N_MOE_STEPS),
        in_specs=[
            idx_spec(0), idx_spec(1),
            pl.BlockSpec(memory_space=pl.ANY),
            pl.BlockSpec((1, 1, D_MODEL, TW_MOE), hid_tile),
            pl.BlockSpec((1, 1, D_MODEL, TW_MOE), hid_tile),
            pl.BlockSpec((1, 1, D_EXPERT, TW_MOE), lambda e, j: (layer, e, 0, out_tile(j))),
        ],
        out_specs=pl.BlockSpec((1, n_rows, TW_MOE), lambda e, j: (e, 0, out_tile(j))),
        out_shape=jax.ShapeDtypeStruct((N_EXPERTS, n_rows, D_MODEL), F32),
        scratch_shapes=[pltpu.VMEM((n_rows, D_MODEL + LANE), F32), pltpu.VMEM((n_rows, D_MODEL), BF16),
                        pltpu.VMEM((n_rows, 1), F32), pltpu.VMEM((N_HID_TILES, n_rows, TW_MOE), BF16),
                        pltpu.SemaphoreType.DMA(())],
        compiler_params=_cparams(("arbitrary", "arbitrary")),
        name="moe_ffn",
    )(idx_rows, idx_rows, h2a_rows, wg, wu, wd)


ROW_GROUP = 8


def _combine_kernel(idx_ref, ys_ref, x1_hbm, ml_ref, mc_ref, o_ref, sem, *, n_lat, n_sel, t0):
    b = pl.program_id(0)

    @pl.when(pl.program_id(1) == 0)
    def _():
        seed = pltpu.make_async_copy(x1_hbm.at[b, pl.ds(t0, T_ALL - t0), :], o_ref.at[0], sem)
        seed.start()
        seed.wait()

    def add_rows(p_lo, p_hi, gate):
        for p0 in range(p_lo, p_hi, ROW_GROUP):
            toks = [idx_ref[0, 0, 0, p0 + r] - t0 for r in range(ROW_GROUP)]
            new = [o_ref[0, pl.ds(toks[r], 1), :] + gate * ys_ref[0, p0 + r:p0 + r + 1, :] for r in range(ROW_GROUP)]
            for r in range(ROW_GROUP):
                o_ref[0, pl.ds(toks[r], 1), :] = new[r]

    add_rows(0, n_lat, ml_ref[0, 5:6, :])
    if n_sel > n_lat:
        add_rows(n_lat, n_sel, mc_ref[0, 5:6, :])


def _combine(idx4, ys, x1, ml, mc, n_lat):
    n_sel = idx4.shape[-1]
    assert n_lat % ROW_GROUP == 0 and (n_sel - n_lat) % ROW_GROUP == 0
    t0 = 0 if n_sel > n_lat else CTX_LEN
    return pl.pallas_call(
        functools.partial(_combine_kernel, n_lat=n_lat, n_sel=n_sel, t0=t0),
        grid=(BATCH, N_EXPERTS),
        in_specs=[pl.BlockSpec((1, 1, 1, n_sel), lambda b, e: (b, e, 0, 0), memory_space=pltpu.SMEM),
                  pl.BlockSpec((1, n_sel, D_MODEL), lambda b, e: (e, b, 0)),
                  pl.BlockSpec(memory_space=pl.ANY),
                  pl.BlockSpec((1, N_MOD, D_MODEL), lambda b, e: (b, 0, 0)),
                  pl.BlockSpec((1, N_MOD, D_MODEL), lambda b, e: (0, 0, 0))],
        out_specs=pl.BlockSpec((1, T_ALL - t0, D_MODEL), lambda b, e: (b, 0, 0)),
        out_shape=jax.ShapeDtypeStruct((BATCH, T_ALL - t0, D_MODEL), F32),
        scratch_shapes=[pltpu.SemaphoreType.DMA(())],
        compiler_params=_cparams(("arbitrary", "arbitrary")),
        name="moe_combine",
    )(idx4, ys, x1, ml, mc)


def _rope_swap_perm():
    idx = np.arange(C_ROPE, dtype=np.int32).reshape(2, 2, C_ROPE // 4)
    return idx[:, ::-1, :].reshape(-1)


def _rope_table():
    rows = SEQ // GRID_W
    row = jnp.broadcast_to(jnp.arange(rows, dtype=F32)[:, None], (rows, GRID_W)).reshape(-1)
    col = jnp.broadcast_to(jnp.arange(GRID_W, dtype=F32)[None, :], (rows, GRID_W)).reshape(-1)
    axis_dim = C_ROPE // 2
    inv_freq = ROPE_BASE ** (-jnp.arange(0, axis_dim, 2, dtype=F32) / axis_dim)
    n_freq = C_ROPE // 4
    ang = jnp.stack([row[:, None] * inv_freq, col[:, None] * inv_freq], axis=1)
    cos = jnp.broadcast_to(jnp.cos(ang)[:, :, None, :], (SEQ, 2, 2, n_freq)).reshape(SEQ, C_ROPE)
    sin = jnp.sin(ang)
    sin = jnp.stack([-sin, sin], axis=2).reshape(SEQ, C_ROPE)
    lat = jnp.concatenate([cos, sin], axis=1)
    ctx = jnp.concatenate([jnp.ones((CTX_LEN, C_ROPE), F32), jnp.zeros((CTX_LEN, C_ROPE), F32)], axis=1)
    return jnp.concatenate([ctx, lat], axis=0)


def _layer_weights(i, w_in, w_gla_decay, b_gla_decay, w_uq, w_ukv, g_qr, g_kr, b_s):
    perm = _rope_swap_perm()
    o = [int(v) for v in np.cumsum((0, 1024, 1024, 512, 512, 1024, 1024, 32, C_Q_RANK, C_KV_RANK, C_ROPE, 3 * D_MODEL))]
    w = w_in[i]
    seg = lambda j: w[:, o[j]:o[j + 1]]
    u, va, qb, kb, vb, rb, dec, cq, ckv, kr, gates = (seg(j) for j in range(11))
    pad = jnp.zeros((D_MODEL, Z_BLK - (SM_DEC + 2 * B_GATE_RANK)), F32)
    w_in_ext = jnp.concatenate([gates, u, va, vb, rb, qb, kb, cq, ckv, kr, kr[:, perm], dec, pad], axis=1).astype(BF16)

    def dec_weight(j):
        wd = jnp.zeros((LANE, B_HEADS * B_DK), F32)
        return wd.at[j * B_GATE_RANK:(j + 1) * B_GATE_RANK].set(w_gla_decay[i, j]).astype(BF16)

    wq = w_uq[i].reshape(C_Q_RANK, C_HEADS, C_NOPE + C_ROPE)
    wq_ext = jnp.concatenate([wq, wq[:, :, C_NOPE:][:, :, perm]], axis=2).reshape(C_Q_RANK, C_HEADS * C_QK).astype(BF16)
    pack = lambda g: jnp.concatenate([g[i], g[i][perm]])[None, :]
    bs_full = jnp.repeat(b_s[i].T, A_CHUNK, axis=1)
    wkv = w_ukv[i].reshape(C_KV_RANK, C_HEADS, C_NOPE + C_DV)
    wkn = wkv[:, :, :C_NOPE].reshape(C_KV_RANK, C_HEADS * C_NOPE).astype(BF16)
    wvt = jnp.transpose(wkv[:, :, C_NOPE:], (1, 2, 0))
    wvt = jnp.concatenate([wvt, jnp.zeros((C_HEADS, VT_ROWS - C_DV, C_KV_RANK), F32)], axis=1).astype(BF16)
    return dict(w_in=w_in_ext, wdec_f=dec_weight(0), wdec_b=dec_weight(1),
                bdec_f=b_gla_decay[i, 0][None, :], bdec_b=b_gla_decay[i, 1][None, :],
                wq=wq_ext, wkn=wkn, wvt=wvt, gqr2=pack(g_qr), gkr2=pack(g_kr), bs_full=bs_full)


def kernel(x, c, ctx, c_ctx, w_mod, b_mod, g_norm_mix, g_norm_ffn, w_in, b_gate, w_s, b_s, w_gla_decay, b_gla_decay,
           g_gla_out, g_cq, g_ckv, w_uq, w_ukv, g_qn, g_qr, g_kn, g_kr, w_proj_a, w_proj_b, w_proj_c, w_out,
           w_router, w_e_gate, w_e_up, w_e_down):
    c8 = jnp.concatenate([c, c_ctx[None, :], jnp.zeros((8 - BATCH - 1, D_MODEL), F32)], axis=0)
    mod = _mod_table(c8, w_mod, b_mod).reshape(DEPTH, 8, N_MOD, D_MODEL)
    cs_tab = _rope_table()
    xc = jnp.concatenate([ctx, x], axis=1)
    row2 = lambda a: a[None, :]
    cap_lat = CAPACITY_FACTOR * SEQ // N_EXPERTS
    cap_ctx = CAPACITY_FACTOR * CTX_LEN // N_EXPERTS
    for i in range(DEPTH):
        lw = _layer_weights(i, w_in, w_gla_decay, b_gla_decay, w_uq, w_ukv, g_qr, g_kr, b_s)
        ml, mc = mod[i, :BATCH], mod[i, BATCH:BATCH + 1]
        h = _mod_norm(xc, row2(g_norm_mix[i]), ml, mc, 0, 1)
        z = _in_proj(h.reshape(BATCH * T_ALL, D_MODEL), lw['w_in']).reshape(BATCH, T_ALL, N_Z)
        ya = _mix_a(z, w_s[i].astype(BF16), lw['bs_full'])
        yb = _gla(z, lw['wdec_f'], lw['bdec_f'], lw['wdec_b'], lw['bdec_b'], row2(g_gla_out[i]))
        q, k, vt = _mla_prep(z, cs_tab, lw['wq'], lw['wkn'], lw['wvt'], row2(g_cq[i]), row2(g_ckv[i]),
                             row2(g_qn[i]), lw['gqr2'], row2(g_kn[i]), lw['gkr2'])
        yc = _attention(q, k, vt)
        w_r = jnp.concatenate([w_router[i], jnp.zeros((D_MODEL, LANE - N_EXPERTS), F32)], axis=1).astype(BF16)
        x1, h2a = _merge(ya, yb, yc, z, row2(b_gate[i]), w_proj_a[i].astype(BF16), w_proj_b[i].astype(BF16),
                         w_proj_c[i].astype(BF16), w_out[i].astype(BF16), xc, ml, mc, row2(g_norm_ffn[i]), w_r)
        caps = ((CTX_LEN, T_ALL, cap_lat),) + (((0, CTX_LEN, cap_ctx),) if i < DEPTH - 1 else ())
        idx = _route(h2a, caps)
        n_sel = idx.shape[-1]
        idx4 = idx.reshape(BATCH, N_EXPERTS, 1, n_sel)
        idx_rows = idx.reshape(BATCH, N_EXPERTS, n_sel) + (jnp.arange(BATCH, dtype=jnp.int32) * T_ALL)[:, None, None]
        idx_rows = jnp.swapaxes(idx_rows, 0, 1).reshape(N_EXPERTS, 1, BATCH * n_sel)
        ys = _moe_ffn(i, idx_rows, h2a.reshape(BATCH * T_ALL, D_MODEL + LANE), w_e_gate, w_e_up, w_e_down)
        xc = _combine(idx4, ys, x1, ml, mc, cap_lat)
    return xc
```

```python
import functools

import jax
import jax.numpy as jnp
import numpy as np
from jax import lax
from jax.experimental import pallas as pl
from jax.experimental.pallas import tpu as pltpu

F32 = jnp.float32
BF16 = jnp.bfloat16

D_MODEL = 1024
BATCH = 4
SEQ = 4096
DEPTH = 2
CTX_LEN = 256
T_ALL = CTX_LEN + SEQ
GRID_W = 64
EPS = 1e-6
N_MOD = 6

A_GROUPS = 8
A_CHUNK = 128

B_HEADS = 4
B_DK = 128
B_DV = 256
B_GATE_RANK = 16
B_GATE_NORM = 16.0
B_CHUNK = 64

C_HEADS = 8
C_Q_RANK = 384
C_KV_RANK = 256
C_NOPE = 128
C_ROPE = 64
C_DV = 128
ROPE_BASE = 10000.0
C_QK = 2 * C_NOPE

N_EXPERTS = 16
CAPACITY_FACTOR = 2
D_EXPERT = 2048

Z_BLK = 1024
ZB_GATES, ZB_U, ZB_VA, ZB_VB, ZB_RB, ZB_QK, ZB_SMALL = 0, 3, 4, 5, 6, 7, 8
N_Z = 9 * Z_BLK
SM_CQ, SM_CKV, SM_KR, SM_DEC = 0, C_Q_RANK, C_Q_RANK + C_KV_RANK, C_Q_RANK + C_KV_RANK + 2 * C_ROPE

LOG2_E = 1.4426950408889634
LANE = 128
VMEM_LIMIT = 56 * 1024 * 1024

TM_ROWS = 544
TM_MM = 2176
TM_PREP = 256
TQ = 256
HEADS_PER_STEP = 4
KEY_BLOCK = 1152
VT_ROWS = C_DV + 16
TB_GLA = 256
TW_MOE = 256


def _cparams(sem):
    return pltpu.CompilerParams(dimension_semantics=sem, vmem_limit_bytes=VMEM_LIMIT)


def _rms(x, n):
    return x * lax.rsqrt(jnp.sum(x * x, axis=-1, keepdims=True) * (1.0 / n) + EPS)


def _silu(x):
    return x * (1.0 / (1.0 + jnp.exp(-x)))


def _sigmoid(x):
    return 1.0 / (1.0 + jnp.exp(-x))


def _row_is_ctx(tile_idx, tm):
    row = tile_idx * tm + lax.broadcasted_iota(jnp.int32, (tm, 1), 0)
    return row < CTX_LEN


def _pick_mod(is_ctx, ml_ref, mc_ref, k):
    return jnp.where(is_ctx, mc_ref[0, k:k + 1, :], ml_ref[0, k:k + 1, :])


def _mod_kernel(c_ref, w_ref, b_ref, o_ref):
    s = _silu(c_ref[...])
    o_ref[0] = jnp.dot(s.astype(BF16), w_ref[0].astype(BF16), preferred_element_type=F32) + b_ref[0]


def _mod_table(c8, w_mod, b_mod):
    tn = 512
    n = N_MOD * D_MODEL
    return pl.pallas_call(
        _mod_kernel,
        grid=(DEPTH, n // tn),
        in_specs=[
            pl.BlockSpec((8, D_MODEL), lambda l, j: (0, 0)),
            pl.BlockSpec((1, D_MODEL, tn), lambda l, j: (l, 0, j)),
            pl.BlockSpec((1, 1, tn), lambda l, j: (l, 0, j)),
        ],
        out_specs=pl.BlockSpec((1, 8, tn), lambda l, j: (l, 0, j)),
        out_shape=jax.ShapeDtypeStruct((DEPTH, 8, n), F32),
        compiler_params=_cparams(("parallel", "parallel")),
        name="mod_table",
    )(c8, w_mod, b_mod.reshape(DEPTH, 1, n))


def _norm_kernel(x_ref, g_ref, ml_ref, mc_ref, o_ref, *, tm, k_shift, k_scale):
    is_ctx = _row_is_ctx(pl.program_id(1), tm)
    y = _rms(x_ref[0], D_MODEL) * g_ref[...]
    y = y * (1.0 + _pick_mod(is_ctx, ml_ref, mc_ref, k_scale)) + _pick_mod(is_ctx, ml_ref, mc_ref, k_shift)
    o_ref[0] = y.astype(BF16)


def _mod_norm(x, g, ml, mc, k_shift, k_scale):
    tm = TM_ROWS
    return pl.pallas_call(
        functools.partial(_norm_kernel, tm=tm, k_shift=k_shift, k_scale=k_scale),
        grid=(BATCH, T_ALL // tm),
        in_specs=[
            pl.BlockSpec((1, tm, D_MODEL), lambda b, i: (b, i, 0)),
            pl.BlockSpec((1, D_MODEL), lambda b, i: (0, 0)),
            pl.BlockSpec((1, N_MOD, D_MODEL), lambda b, i: (b, 0, 0)),
            pl.BlockSpec((1, N_MOD, D_MODEL), lambda b, i: (0, 0, 0)),
        ],
        out_specs=pl.BlockSpec((1, tm, D_MODEL), lambda b, i: (b, i, 0)),
        out_shape=jax.ShapeDtypeStruct((BATCH, T_ALL, D_MODEL), BF16),
        compiler_params=_cparams(("parallel", "parallel")),
        name="mod_norm",
    )(x, g, ml, mc)


def _matmul_kernel(a_ref, w_ref, o_ref):
    o_ref[...] = jnp.dot(a_ref[...], w_ref[...], preferred_element_type=F32).astype(o_ref.dtype)


def _in_proj(h2d, w):
    m, k = h2d.shape
    n = w.shape[1]
    tm, tn = TM_MM, Z_BLK
    return pl.pallas_call(
        _matmul_kernel,
        grid=(m // tm, n // tn),
        in_specs=[
            pl.BlockSpec((tm, k), lambda i, j: (i, 0)),
            pl.BlockSpec((k, tn), lambda i, j: (0, j)),
        ],
        out_specs=pl.BlockSpec((tm, tn), lambda i, j: (i, j)),
        out_shape=jax.ShapeDtypeStruct((m, n), BF16),
        compiler_params=_cparams(("parallel", "parallel")),
        name="in_proj",
    )(h2d, w)


def _mix_a_block(u_ref, v_ref, ws_ref, bs_ref, o_ref):
    for r0 in range(0, u_ref.shape[1], A_CHUNK):
        rows = slice(r0, r0 + A_CHUNK)
        for g in range(A_GROUPS):
            cols = slice(g * LANE, (g + 1) * LANE)
            v = v_ref[0, rows, cols].astype(F32)
            mu = jnp.mean(v, axis=-1, keepdims=True)
            d = v - mu
            ln = d * lax.rsqrt(jnp.mean(d * d, axis=-1, keepdims=True) + EPS)
            mixed = jnp.dot(ws_ref[g], ln.astype(BF16), preferred_element_type=F32) + bs_ref[:, cols]
            o_ref[0, rows, cols] = (u_ref[0, rows, cols].astype(F32) * mixed).astype(BF16)


_NT = (((1,), (1,)), ((), ()))
_TN = (((0,), (0,)), ((), ()))


def _log_sigmoid(x):
    return jnp.minimum(x, 0.0) - jnp.log(1.0 + jnp.exp(-jnp.abs(x)))


def _exact_tri_sum(tri, la):
    hi = la.astype(BF16)
    r1 = la - hi.astype(F32)
    mid = r1.astype(BF16)
    lo = (r1 - mid.astype(F32)).astype(BF16)
    return (jnp.dot(tri, hi, preferred_element_type=F32)
            + jnp.dot(tri, mid, preferred_element_type=F32)
            + jnp.dot(tri, lo, preferred_element_type=F32))


def _gla_block(qk_ref, v_ref, sm_ref, wdec_ref, bdec_ref, st_ref, *, reverse, emit):
    n_chunk = TB_GLA // B_CHUNK
    shift = B_CHUNK.bit_length() - 1
    ii = lax.broadcasted_iota(jnp.int32, (TB_GLA, TB_GLA), 0)
    jj = lax.broadcasted_iota(jnp.int32, (TB_GLA, TB_GLA), 1)
    keep = ((ii >> shift) == (jj >> shift)) & ((jj >= ii) if reverse else (jj <= ii))
    edge = 0 if reverse else B_CHUNK - 1

    pre = jnp.dot(sm_ref[0, :, SM_DEC:SM_DEC + LANE], wdec_ref[...], preferred_element_type=F32) + bdec_ref[...]
    la = _log_sigmoid(pre) * (1.0 / B_GATE_NORM)
    bcum = _exact_tri_sum(jnp.where(keep, 1.0, 0.0).astype(BF16), la)
    b_tot = [bcum[c * B_CHUNK + edge:c * B_CHUNK + edge + 1] for c in range(n_chunk)]
    blast = jnp.concatenate([jnp.broadcast_to(bt, (B_CHUNK, B_HEADS * B_DK)) for bt in b_tot], axis=0)
    e_fwd, e_inv, e_rest = jnp.exp(bcum), jnp.exp(-bcum), jnp.exp(blast - bcum)
    e_tot = [jnp.exp(bt) for bt in b_tot]
    for h in range(B_HEADS):
        kc = slice(h * B_DK, (h + 1) * B_DK)
        vc = slice(h * B_DV, (h + 1) * B_DV)
        q = qk_ref[0, :, kc].astype(F32)
        k = qk_ref[0, :, B_HEADS * B_DK + h * B_DK:B_HEADS * B_DK + (h + 1) * B_DK].astype(F32)
        v = v_ref[0, :, vc]
        qd = ((q * B_DK ** -0.5) * e_fwd[:, kc]).astype(BF16)
        kinv = (k * e_inv[:, kc]).astype(BF16)
        kdec = (k * e_rest[:, kc]).astype(BF16)
        s = lax.dot_general(qd, kinv, _NT, preferred_element_type=F32)
        o_intra = jnp.dot(jnp.where(keep, s, 0.0).astype(BF16), v, preferred_element_type=F32)
        st = st_ref[h]
        for c in (range(n_chunk - 1, -1, -1) if reverse else range(n_chunk)):
            rows = slice(c * B_CHUNK, (c + 1) * B_CHUNK)
            emit(h, rows, o_intra[rows] + lax.dot_general(qd[rows], st.astype(BF16), _NT, preferred_element_type=F32))
            kv_t = lax.dot_general(v[rows], kdec[rows], _TN, preferred_element_type=F32)
            st = e_tot[c][:, kc] * st + kv_t
        st_ref[h] = st


def _gla_fwd_kernel(qk_ref, v_ref, sm_ref, wdec_ref, bdec_ref, u_ref, va_ref, ws_ref, bs_ref, o_ref, ya_ref, st_ref):
    @pl.when(pl.program_id(1) == 0)
    def _():
        st_ref[...] = jnp.zeros_like(st_ref)

    def emit(h, rows, o):
        o_ref[0, rows, h * B_DV:(h + 1) * B_DV] = o

    _mix_a_block(u_ref, va_ref, ws_ref, bs_ref, ya_ref)
    _gla_block(qk_ref, v_ref, sm_ref, wdec_ref, bdec_ref, st_ref, reverse=False, emit=emit)


def _gla_bwd_kernel(qk_ref, v_ref, sm_ref, wdec_ref, bdec_ref, of_ref, r_ref, g_ref, y_ref, st_ref):
    @pl.when(pl.program_id(1) == 0)
    def _():
        st_ref[...] = jnp.zeros_like(st_ref)

    def emit(h, rows, o):
        vc = slice(h * B_DV, (h + 1) * B_DV)
        o = o + of_ref[0, rows, vc]
        on = _rms(o, B_DV) * g_ref[...]
        y_ref[0, rows, vc] = (on * _silu(r_ref[0, rows, vc].astype(F32))).astype(BF16)

    _gla_block(qk_ref, v_ref, sm_ref, wdec_ref, bdec_ref, st_ref, reverse=True, emit=emit)


def _bwd_block(i):
    nb = T_ALL // TB_GLA
    return jnp.where(i == 0, 0, nb - i)


def _mix_ab(z, wdec_f, bdec_f, wdec_b, bdec_b, g_out, ws, bs_full):
    tb = TB_GLA
    nb = T_ALL // tb
    wide = B_HEADS * B_DV

    def zspec(blk, imap):
        return pl.BlockSpec((1, tb, Z_BLK), lambda b, i: (b, imap(i), blk))

    def const2(shape):
        return pl.BlockSpec(shape, lambda b, i: (0, 0))

    state = pltpu.VMEM((B_HEADS, B_DV, B_DK), F32)
    fwd = lambda i: i
    o_f, ya = pl.pallas_call(
        _gla_fwd_kernel,
        grid=(BATCH, nb),
        in_specs=[zspec(ZB_QK, fwd), zspec(ZB_VB, fwd), zspec(ZB_SMALL, fwd),
                  const2((LANE, B_HEADS * B_DK)), const2((1, B_HEADS * B_DK)),
                  zspec(ZB_U, fwd), zspec(ZB_VA, fwd),
                  pl.BlockSpec((A_GROUPS, A_CHUNK, A_CHUNK), lambda b, i: (0, 0, 0)), const2((A_CHUNK, D_MODEL))],
        out_specs=[pl.BlockSpec((1, tb, wide), lambda b, i: (b, i, 0)),
                   pl.BlockSpec((1, tb, D_MODEL), lambda b, i: (b, i, 0))],
        out_shape=[jax.ShapeDtypeStruct((BATCH, T_ALL, wide), F32),
                   jax.ShapeDtypeStruct((BATCH, T_ALL, D_MODEL), BF16)],
        scratch_shapes=[state],
        compiler_params=_cparams(("parallel", "arbitrary")),
        name="gla_fwd",
    )(z, z, z, wdec_f, bdec_f, z, z, ws, bs_full)
    return ya, pl.pallas_call(
        _gla_bwd_kernel,
        grid=(BATCH, nb),
        in_specs=[zspec(ZB_QK, _bwd_block), zspec(ZB_VB, _bwd_block), zspec(ZB_SMALL, _bwd_block),
                  const2((LANE, B_HEADS * B_DK)), const2((1, B_HEADS * B_DK)),
                  pl.BlockSpec((1, tb, wide), lambda b, i: (b, _bwd_block(i), 0)),
                  zspec(ZB_RB, _bwd_block), const2((1, B_DV))],
        out_specs=pl.BlockSpec((1, tb, wide), lambda b, i: (b, _bwd_block(i), 0)),
        out_shape=jax.ShapeDtypeStruct((BATCH, T_ALL, wide), BF16),
        scratch_shapes=[state],
        compiler_params=_cparams(("parallel", "arbitrary")),
        name="gla_bwd",
    )(z, z, z, wdec_b, bdec_b, o_f, z, g_out)


def _mla_prep_kernel(sm_ref, cs_ref, wq_ref, wkn_ref, wvt_ref, gcq_ref, gckv_ref, gqn_ref, gqr_ref, gkn_ref, gkr_ref,
                     q_ref, k_ref, vt_ref):
    scale = (C_NOPE + C_ROPE) ** -0.5 * LOG2_E
    vt_row = lax.broadcasted_iota(jnp.int32, (VT_ROWS, 1), 0)
    cs = cs_ref[...]
    lane = lax.broadcasted_iota(jnp.int32, (1, LANE), 1)
    low_half = (lane < C_ROPE).astype(F32)

    def rope(pack, gain):
        t = _rms(pack, LANE) * gain * cs
        return t + pltpu.roll(t, C_ROPE, axis=1)

    cq = sm_ref[0, :, SM_CQ:SM_CQ + C_Q_RANK].astype(F32)
    cqn = (_rms(cq, C_Q_RANK) * gcq_ref[...]).astype(BF16)
    q_all = jnp.dot(cqn, wq_ref[...], preferred_element_type=F32)
    ckv = sm_ref[0, :, SM_CKV:SM_CKV + C_KV_RANK].astype(F32)
    ckvn = (_rms(ckv, C_KV_RANK) * gckv_ref[...]).astype(BF16)
    kn_all = jnp.dot(ckvn, wkn_ref[...], preferred_element_type=F32)
    k_rot = rope(sm_ref[0, :, SM_KR:SM_KR + LANE].astype(F32), gkr_ref[...]).astype(BF16)
    for h in range(C_HEADS):
        c0 = h * C_QK
        qn = _rms(q_all[:, c0:c0 + C_NOPE], C_NOPE) * gqn_ref[...]
        q_ref[0, h, :, 0:C_NOPE] = (qn * scale).astype(BF16)
        q_rot = rope(q_all[:, c0 + C_NOPE:c0 + C_QK], gqr_ref[...]) * low_half
        q_ref[0, h, :, C_NOPE:C_QK] = (q_rot * scale).astype(BF16)
        kn = _rms(kn_all[:, h * C_NOPE:(h + 1) * C_NOPE], C_NOPE) * gkn_ref[...]
        k_ref[0, h, :, 0:C_NOPE] = kn.astype(BF16)
        k_ref[0, h, :, C_NOPE:C_QK] = k_rot
        vt = lax.dot_general(wvt_ref[h], ckvn, _NT, preferred_element_type=F32)
        vt_ref[0, h] = jnp.where(vt_row == C_DV, 1.0, vt).astype(BF16)


def _mla_prep(z, cs_tab, wq, wkn, wvt, gcq, gckv, gqn, gqr2, gkn, gkr2):
    tm = TM_PREP

    def const2(shape):
        return pl.BlockSpec(shape, lambda b, i: (0, 0))

    hspec = lambda w: pl.BlockSpec((1, C_HEADS, tm, w), lambda b, i: (b, 0, i, 0))
    hshape = lambda w: jax.ShapeDtypeStruct((BATCH, C_HEADS, T_ALL, w), BF16)
    return pl.pallas_call(
        _mla_prep_kernel,
        grid=(BATCH, T_ALL // tm),
        in_specs=[
            pl.BlockSpec((1, tm, Z_BLK), lambda b, i: (b, i, ZB_SMALL)),
            pl.BlockSpec((tm, LANE), lambda b, i: (i, 0)),
            const2((C_Q_RANK, C_HEADS * C_QK)), const2((C_KV_RANK, C_HEADS * C_NOPE)),
            pl.BlockSpec((C_HEADS, VT_ROWS, C_KV_RANK), lambda b, i: (0, 0, 0)),
            const2((1, C_Q_RANK)), const2((1, C_KV_RANK)),
            const2((1, C_NOPE)), const2((1, LANE)), const2((1, C_NOPE)), const2((1, LANE)),
        ],
        out_specs=[hspec(C_QK), hspec(C_QK),
                   pl.BlockSpec((1, C_HEADS, VT_ROWS, tm), lambda b, i: (b, 0, 0, i))],
        out_shape=[hshape(C_QK), hshape(C_QK),
                   jax.ShapeDtypeStruct((BATCH, C_HEADS, VT_ROWS, T_ALL), BF16)],
        compiler_params=_cparams(("parallel", "parallel")),
        name="mla_prep",
    )(z, cs_tab, wq, wkn, wvt, gcq, gckv, gqn, gqr2, gkn, gkr2)


def _attn_kernel(q_ref, k_ref, vt_ref, o_ref):
    def max_over_keys(s):
        parts = [s[i * 8:(i + 1) * 8] for i in range(s.shape[0] // 8)]
        while len(parts) > 1:
            parts = [jnp.maximum(parts[i], parts[i + 1]) for i in range(0, len(parts) - 1, 2)] + parts[len(parts) & ~1:]
        return jnp.max(parts[0], axis=0, keepdims=True)

    def attend(n_keys):
        edges = list(range(0, n_keys, KEY_BLOCK)) + [n_keys]
        blocks = [slice(a, b) for a, b in zip(edges[:-1], edges[1:])]
        heads = range(HEADS_PER_STEP)
        s = [[lax.dot_general(k_ref[0, h, kb, :], q_ref[0, h], _NT, preferred_element_type=F32) for kb in blocks]
             for h in heads]
        for h in heads:
            m = functools.reduce(jnp.maximum, [max_over_keys(sb) for sb in s[h]])
            acc = None
            for kb, sb in zip(blocks, s[h]):
                p = jnp.exp2(sb - m).astype(BF16)
                part = jnp.dot(vt_ref[0, h, :, kb], p, preferred_element_type=F32)
                acc = part if acc is None else acc + part
            o = acc[0:C_DV] * (1.0 / acc[C_DV:C_DV + 1])
            o_ref[0, :, h * C_DV:(h + 1) * C_DV] = o.T.astype(BF16)

    @pl.when(pl.program_id(2) == 0)
    def _():
        attend(CTX_LEN)

    @pl.when(pl.program_id(2) > 0)
    def _():
        attend(T_ALL)


def _attention(q, k, vt):
    tq = TQ
    hs = HEADS_PER_STEP
    return pl.pallas_call(
        _attn_kernel,
        grid=(BATCH, C_HEADS // hs, T_ALL // tq),
        in_specs=[
            pl.BlockSpec((1, hs, tq, C_QK), lambda b, h, i: (b, h, i, 0)),
            pl.BlockSpec((1, hs, T_ALL, C_QK), lambda b, h, i: (b, h, 0, 0)),
            pl.BlockSpec((1, hs, VT_ROWS, T_ALL), lambda b, h, i: (b, h, 0, 0)),
        ],
        out_specs=pl.BlockSpec((1, tq, hs * C_DV), lambda b, h, i: (b, i, h)),
        out_shape=jax.ShapeDtypeStruct((BATCH, T_ALL, C_HEADS * C_DV), BF16),
        compiler_params=_cparams(("parallel", "parallel", "parallel")),
        name="mla_attention",
    )(q, k, vt)


def _merge_kernel(ya_ref, yb_ref, yc_ref, gt_ref, bg_ref, wa_ref, wb_ref, wc_ref, wo_ref, x_ref, ml_ref, mc_ref,
                  gn_ref, wr_ref, x1_ref, h2a_ref, *, tm):
    is_ctx = _row_is_ctx(pl.program_id(1), tm)
    merged = None
    for j, (y_ref, w_ref) in enumerate(((ya_ref, wa_ref), (yb_ref, wb_ref), (yc_ref, wc_ref))):
        cols = slice(j * D_MODEL, (j + 1) * D_MODEL)
        gate = _sigmoid(gt_ref[0, :, cols].astype(F32) + bg_ref[:, cols])
        term = gate * jnp.dot(y_ref[0], w_ref[...], preferred_element_type=F32)
        merged = term if merged is None else merged + term
    y = jnp.dot(merged.astype(BF16), wo_ref[...], preferred_element_type=F32)
    x1 = x_ref[0] + _pick_mod(is_ctx, ml_ref, mc_ref, 2) * y
    x1_ref[0] = x1
    h2 = _rms(x1, D_MODEL) * gn_ref[...]
    h2 = h2 * (1.0 + _pick_mod(is_ctx, ml_ref, mc_ref, 4)) + _pick_mod(is_ctx, ml_ref, mc_ref, 3)
    logits = jnp.dot(h2.astype(BF16), wr_ref[...], preferred_element_type=F32)
    lane = lax.broadcasted_iota(jnp.int32, (1, LANE), 1)
    logits = jnp.where(lane < N_EXPERTS, logits, -jnp.inf)
    e = jnp.exp(logits - jnp.max(logits, axis=-1, keepdims=True))
    h2a_ref[0, :, 0:D_MODEL] = h2
    h2a_ref[0, :, D_MODEL:D_MODEL + LANE] = e / jnp.sum(e, axis=-1, keepdims=True)


def _merge(ya, yb, yc, z, b_gate, wa, wb, wc, wo, x, ml, mc, g_ffn, w_router):
    tm = TM_ROWS
    row = lambda w: pl.BlockSpec((1, tm, w), lambda b, i: (b, i, 0))

    def const2(shape):
        return pl.BlockSpec(shape, lambda b, i: (0, 0))

    sq = const2((D_MODEL, D_MODEL))
    return pl.pallas_call(
        functools.partial(_merge_kernel, tm=tm),
        grid=(BATCH, T_ALL // tm),
        in_specs=[
            row(D_MODEL), row(D_MODEL), row(D_MODEL), row(3 * D_MODEL), const2((1, 3 * D_MODEL)),
            sq, sq, sq, sq, row(D_MODEL),
            pl.BlockSpec((1, N_MOD, D_MODEL), lambda b, i: (b, 0, 0)),
            pl.BlockSpec((1, N_MOD, D_MODEL), lambda b, i: (0, 0, 0)),
            const2((1, D_MODEL)), const2((D_MODEL, LANE)),
        ],
        out_specs=[row(D_MODEL), row(D_MODEL + LANE)],
        out_shape=[jax.ShapeDtypeStruct((BATCH, T_ALL, D_MODEL), F32),
                   jax.ShapeDtypeStruct((BATCH, T_ALL, D_MODEL + LANE), F32)],
        compiler_params=_cparams(("parallel", "parallel")),
        name="merge",
    )(ya, yb, yc, z, b_gate, wa, wb, wc, wo, x, ml, mc, g_ffn, w_router)


N_ROUTE_ROWS = BATCH * N_EXPERTS
N_BISECT = 150


def _count(mask):
    return jnp.sum(jnp.where(mask, 1.0, 0.0), axis=1, keepdims=True)


def _lane_cumsum_blocks(x01):
    r, n = x01.shape
    nblk = n // LANE
    x3 = jnp.concatenate([x01[:, j * LANE:(j + 1) * LANE] for j in range(nblk)], axis=0).astype(BF16)
    ii = lax.broadcasted_iota(jnp.int32, (LANE, LANE), 0)
    jj = lax.broadcasted_iota(jnp.int32, (LANE, LANE), 1)
    inc = jnp.dot(x3, jnp.where(ii <= jj, 1.0, 0.0).astype(BF16), preferred_element_type=F32)
    off = jnp.zeros((r, 1), F32)
    out = []
    for j in range(nblk):
        blk = inc[j * r:(j + 1) * r]
        out.append(blk + off)
        off = off + blk[:, LANE - 1:LANE]
    return out


def _select_top(a, cap):
    def more(c):
        return (c[2] > 0.0) & (c[3] < N_BISECT)

    def bisect(c):
        thr, step, _, i = c
        cand = thr + step
        thr = jnp.where(_count(a >= cand) >= cap, cand, thr)
        step = step * 0.5
        live = jnp.max(jnp.where(thr + step != thr, 1.0, 0.0))
        return thr, step, live, i + 1

    thr = lax.while_loop(more, bisect, (jnp.zeros((a.shape[0], 1), F32), jnp.ones((a.shape[0], 1), F32),
                                        jnp.float32(1.0), jnp.int32(0)))[0]
    gt = a > thr
    eq = a == thr
    need = cap - _count(gt)
    eq_rank = jnp.concatenate(_lane_cumsum_blocks(jnp.where(eq, 1.0, 0.0)), axis=1)
    sel = gt | (eq & (eq_rank <= need))
    return _lane_cumsum_blocks(jnp.where(sel, 1.0, 0.0))


def _route_kernel(aff_ref, idx_ref, cum_ref, end_ref, *, caps):
    aff_t = jnp.concatenate([aff_ref[b].T[:N_EXPERTS] for b in range(BATCH)], axis=0)
    lane = lax.broadcasted_iota(jnp.int32, (1, LANE), 1)
    p0 = 0
    for (t_lo, t_hi, cap) in caps:
        nblk = (t_hi - t_lo) // LANE
        blocks = _select_top(aff_t[:, t_lo:t_hi], cap)
        ends = jnp.full((N_ROUTE_ROWS, LANE), float(cap), F32)
        for j, blk in enumerate(blocks):
            cum_ref[:, j, :] = blk
            ends = jnp.where(lane == j, blk[:, LANE - 1:LANE], ends)
        end_ref[...] = ends
        p_col = lax.broadcasted_iota(jnp.int32, (cap, 1), 0).astype(F32)

        def per_row(r, res):
            n_full = _count(end_ref[pl.ds(r, 1), :] <= p_col)
            cum = cum_ref[r][0:nblk]
            cum = jnp.concatenate([cum, jnp.zeros((LANE - nblk, LANE), F32)], axis=0)
            hi = jnp.floor(cum * (1.0 / 64.0))
            lo = cum - 64.0 * hi
            pick = jnp.where(lane.astype(F32) == n_full, 1.0, 0.0).astype(BF16)
            part = (64.0 * jnp.dot(pick, hi.astype(BF16), preferred_element_type=F32)
                    + jnp.dot(pick, lo.astype(BF16), preferred_element_type=F32))
            cnt = LANE * n_full + _count(part <= p_col)
            return jnp.where(lane == r, cnt, res)

        res = lax.fori_loop(0, N_ROUTE_ROWS, per_row, jnp.zeros((cap, LANE), F32))
        if cap % LANE:
            res = jnp.concatenate([res, jnp.zeros((LANE - cap % LANE, LANE), F32)], axis=0)
        idx_ref[:, p0:p0 + cap] = (res.T[:N_ROUTE_ROWS, :cap] + float(t_lo)).astype(jnp.int32)
        p0 += cap


def _route(h2a, caps):
    n_sel = sum(c[2] for c in caps)
    return pl.pallas_call(
        functools.partial(_route_kernel, caps=caps),
        grid=(1,),
        in_specs=[pl.BlockSpec((BATCH, T_ALL, LANE), lambda i: (0, 0, D_MODEL // LANE))],
        out_specs=pl.BlockSpec((N_ROUTE_ROWS, n_sel), lambda i: (0, 0)),
        out_shape=jax.ShapeDtypeStruct((N_ROUTE_ROWS, n_sel), jnp.int32),
        scratch_shapes=[pltpu.VMEM((N_ROUTE_ROWS, SEQ // LANE, LANE), F32), pltpu.VMEM((N_ROUTE_ROWS, LANE), F32)],
        compiler_params=_cparams(("arbitrary",)),
        name="moe_route",
    )(h2a)


N_HID_TILES = D_EXPERT // TW_MOE
N_OUT_TILES = D_MODEL // TW_MOE


N_MOE_STEPS = N_HID_TILES + N_OUT_TILES


def _moe_kernel(idx_cur_ref, idx_nxt_ref, h2a_hbm, wg_ref, wu_ref, wd_ref, o_ref,
                stage_ref, x_ref, g_ref, hid_ref, sem, *, n_rows, rows_per_step):
    e = pl.program_id(0)
    j = pl.program_id(1)
    n_copies = n_rows

    def row_copy(idx_ref, r):
        return pltpu.make_async_copy(h2a_hbm.at[pl.ds(idx_ref[0, 0, r], 1), :], stage_ref.at[pl.ds(r, 1), :], sem)

    def issue_ahead():
        for k in range(rows_per_step):
            row_copy(idx_nxt_ref, j * rows_per_step + k).start()

    def wait_stage():
        def one(r, carry):
            row_copy(idx_cur_ref, r).wait()
            return carry

        lax.fori_loop(0, n_copies, one, 0, unroll=8)

    @pl.when((e == 0) & (j == 0))
    def _():
        def one(r, carry):
            row_copy(idx_cur_ref, r).start()
            return carry

        lax.fori_loop(0, n_copies, one, 0, unroll=8)

    @pl.when(j == 0)
    def _():
        wait_stage()
        x_ref[...] = stage_ref[:, 0:D_MODEL].astype(BF16)
        lane = lax.broadcasted_iota(jnp.int32, (1, LANE), 1)
        g_ref[...] = jnp.sum(jnp.where(lane == e, stage_ref[:, D_MODEL:D_MODEL + LANE], 0.0), axis=1, keepdims=True)

    @pl.when(j < N_HID_TILES)
    def _():
        issue_ahead()
        x = x_ref[...]
        h = (_silu(jnp.dot(x, wg_ref[0, 0].astype(BF16), preferred_element_type=F32))
             * jnp.dot(x, wu_ref[0, 0].astype(BF16), preferred_element_type=F32))
        hid_ref[jnp.minimum(j, N_HID_TILES - 1)] = h.astype(BF16)

    @pl.when(j >= N_HID_TILES)
    def _():
        hid = jnp.concatenate([hid_ref[f] for f in range(N_HID_TILES)], axis=1)
        o_ref[0] = jnp.dot(hid, wd_ref[0, 0].astype(BF16), preferred_element_type=F32) * g_ref[...]

    @pl.when((e == N_EXPERTS - 1) & (j == N_MOE_STEPS - 1))
    def _():
        wait_stage()


def _moe_ffn(layer, idx_rows, h2a_rows, wg, wu, wd):
    n_rows = idx_rows.shape[-1]
    assert n_rows % N_HID_TILES == 0
    rows_per_step = n_rows // N_HID_TILES
    hid_tile = lambda e, j: (layer, e, 0, jnp.minimum(j, N_HID_TILES - 1))
    out_tile = lambda j: jnp.maximum(j - N_HID_TILES, 0)
    idx_spec = lambda step: pl.BlockSpec((1, 1, n_rows), lambda e, j: (jnp.minimum(e + step, N_EXPERTS - 1), 0, 0),
                                         memory_space=pltpu.SMEM)
    return pl.pallas_call(
        functools.partial(_moe_kernel, n_rows=n_rows, rows_per_step=rows_per_step),
        grid=(N_EXPERTS, N_MOE_STEPS),
        in_specs=[
            idx_spec(0), idx_spec(1),
            pl.BlockSpec(memory_space=pl.ANY),
            pl.BlockSpec((1, 1, D_MODEL, TW_MOE), hid_tile),
            pl.BlockSpec((1, 1, D_MODEL, TW_MOE), hid_tile),
            pl.BlockSpec((1, 1, D_EXPERT, TW_MOE), lambda e, j: (layer, e, 0, out_tile(j))),
        ],
        out_specs=pl.BlockSpec((1, n_rows, TW_MOE), lambda e, j: (e, 0, out_tile(j))),
        out_shape=jax.ShapeDtypeStruct((N_EXPERTS, n_rows, D_MODEL), F32),
        scratch_shapes=[pltpu.VMEM((n_rows, D_MODEL + LANE), F32), pltpu.VMEM((n_rows, D_MODEL), BF16),
                        pltpu.VMEM((n_rows, 1), F32), pltpu.VMEM((N_HID_TILES, n_rows, TW_MOE), BF16),
                        pltpu.SemaphoreType.DMA(())],
        compiler_params=_cparams(("arbitrary", "arbitrary")),
        name="moe_ffn",
    )(idx_rows, idx_rows, h2a_rows, wg, wu, wd)


ROW_GROUP = 8


def _combine_kernel(idx_ref, ys_ref, x1_hbm, ml_ref, mc_ref, o_ref, sem, *, n_lat, n_sel, t0):
    b = pl.program_id(0)

    @pl.when(pl.program_id(1) == 0)
    def _():
        seed = pltpu.make_async_copy(x1_hbm.at[b, pl.ds(t0, T_ALL - t0), :], o_ref.at[0], sem)
        seed.start()
        seed.wait()

    def add_rows(p_lo, p_hi, gate):
        for p0 in range(p_lo, p_hi, ROW_GROUP):
            toks = [idx_ref[0, 0, 0, p0 + r] - t0 for r in range(ROW_GROUP)]
            new = [o_ref[0, pl.ds(toks[r], 1), :] + gate * ys_ref[0, p0 + r:p0 + r + 1, :] for r in range(ROW_GROUP)]
            for r in range(ROW_GROUP):
                o_ref[0, pl.ds(toks[r], 1), :] = new[r]

    add_rows(0, n_lat, ml_ref[0, 5:6, :])
    if n_sel > n_lat:
        add_rows(n_lat, n_sel, mc_ref[0, 5:6, :])


def _combine(idx4, ys, x1, ml, mc, n_lat):
    n_sel = idx4.shape[-1]
    assert n_lat % ROW_GROUP == 0 and (n_sel - n_lat) % ROW_GROUP == 0
    t0 = 0 if n_sel > n_lat else CTX_LEN
    return pl.pallas_call(
        functools.partial(_combine_kernel, n_lat=n_lat, n_sel=n_sel, t0=t0),
        grid=(BATCH, N_EXPERTS),
        in_specs=[pl.BlockSpec((1, 1, 1, n_sel), lambda b, e: (b, e, 0, 0), memory_space=pltpu.SMEM),
                  pl.BlockSpec((1, n_sel, D_MODEL), lambda b, e: (e, b, 0)),
                  pl.BlockSpec(memory_space=pl.ANY),
                  pl.BlockSpec((1, N_MOD, D_MODEL), lambda b, e: (b, 0, 0)),
                  pl.BlockSpec((1, N_MOD, D_MODEL), lambda b, e: (0, 0, 0))],
        out_specs=pl.BlockSpec((1, T_ALL - t0, D_MODEL), lambda b, e: (b, 0, 0)),
        out_shape=jax.ShapeDtypeStruct((BATCH, T_ALL - t0, D_MODEL), F32),
        scratch_shapes=[pltpu.SemaphoreType.DMA(())],
        compiler_params=_cparams(("arbitrary", "arbitrary")),
        name="moe_combine",
    )(idx4, ys, x1, ml, mc)


def _rope_swap_perm():
    idx = np.arange(C_ROPE, dtype=np.int32).reshape(2, 2, C_ROPE // 4)
    return idx[:, ::-1, :].reshape(-1)


def _rope_table():
    rows = SEQ // GRID_W
    row = jnp.broadcast_to(jnp.arange(rows, dtype=F32)[:, None], (rows, GRID_W)).reshape(-1)
    col = jnp.broadcast_to(jnp.arange(GRID_W, dtype=F32)[None, :], (rows, GRID_W)).reshape(-1)
    axis_dim = C_ROPE // 2
    inv_freq = ROPE_BASE ** (-jnp.arange(0, axis_dim, 2, dtype=F32) / axis_dim)
    n_freq = C_ROPE // 4
    ang = jnp.stack([row[:, None] * inv_freq, col[:, None] * inv_freq], axis=1)
    cos = jnp.broadcast_to(jnp.cos(ang)[:, :, None, :], (SEQ, 2, 2, n_freq)).reshape(SEQ, C_ROPE)
    sin = jnp.sin(ang)
    sin = jnp.stack([-sin, sin], axis=2).reshape(SEQ, C_ROPE)
    lat = jnp.concatenate([cos, sin], axis=1)
    ctx = jnp.concatenate([jnp.ones((CTX_LEN, C_ROPE), F32), jnp.zeros((CTX_LEN, C_ROPE), F32)], axis=1)
    return jnp.concatenate([ctx, lat], axis=0)


def _layer_weights(i, w_in, w_gla_decay, b_gla_decay, w_uq, w_ukv, g_qr, g_kr, b_s):
    perm = _rope_swap_perm()
    o = [int(v) for v in np.cumsum((0, 1024, 1024, 512, 512, 1024, 1024, 32, C_Q_RANK, C_KV_RANK, C_ROPE, 3 * D_MODEL))]
    w = w_in[i]
    seg = lambda j: w[:, o[j]:o[j + 1]]
    u, va, qb, kb, vb, rb, dec, cq, ckv, kr, gates = (seg(j) for j in range(11))
    pad = jnp.zeros((D_MODEL, Z_BLK - (SM_DEC + 2 * B_GATE_RANK)), F32)
    w_in_ext = jnp.concatenate([gates, u, va, vb, rb, qb, kb, cq, ckv, kr, kr[:, perm], dec, pad], axis=1).astype(BF16)

    def dec_weight(j):
        wd = jnp.zeros((LANE, B_HEADS * B_DK), F32)
        return wd.at[j * B_GATE_RANK:(j + 1) * B_GATE_RANK].set(w_gla_decay[i, j]).astype(BF16)

    wq = w_uq[i].reshape(C_Q_RANK, C_HEADS, C_NOPE + C_ROPE)
    wq_ext = jnp.concatenate([wq, wq[:, :, C_NOPE:][:, :, perm]], axis=2).reshape(C_Q_RANK, C_HEADS * C_QK).astype(BF16)
    pack = lambda g: jnp.concatenate([g[i], g[i][perm]])[None, :]
    bs_full = jnp.repeat(b_s[i].T, A_CHUNK, axis=1)
    wkv = w_ukv[i].reshape(C_KV_RANK, C_HEADS, C_NOPE + C_DV)
    wkn = wkv[:, :, :C_NOPE].reshape(C_KV_RANK, C_HEADS * C_NOPE).astype(BF16)
    wvt = jnp.transpose(wkv[:, :, C_NOPE:], (1, 2, 0))
    wvt = jnp.concatenate([wvt, jnp.zeros((C_HEADS, VT_ROWS - C_DV, C_KV_RANK), F32)], axis=1).astype(BF16)
    return dict(w_in=w_in_ext, wdec_f=dec_weight(0), wdec_b=dec_weight(1),
                bdec_f=b_gla_decay[i, 0][None, :], bdec_b=b_gla_decay[i, 1][None, :],
                wq=wq_ext, wkn=wkn, wvt=wvt, gqr2=pack(g_qr), gkr2=pack(g_kr), bs_full=bs_full)


def kernel(x, c, ctx, c_ctx, w_mod, b_mod, g_norm_mix, g_norm_ffn, w_in, b_gate, w_s, b_s, w_gla_decay, b_gla_decay,
           g_gla_out, g_cq, g_ckv, w_uq, w_ukv, g_qn, g_qr, g_kn, g_kr, w_proj_a, w_proj_b, w_proj_c, w_out,
           w_router, w_e_gate, w_e_up, w_e_down):
    c8 = jnp.concatenate([c, c_ctx[None, :], jnp.zeros((8 - BATCH - 1, D_MODEL), F32)], axis=0)
    mod = _mod_table(c8, w_mod, b_mod).reshape(DEPTH, 8, N_MOD, D_MODEL)
    cs_tab = _rope_table()
    xc = jnp.concatenate([ctx, x], axis=1)
    row2 = lambda a: a[None, :]
    cap_lat = CAPACITY_FACTOR * SEQ // N_EXPERTS
    cap_ctx = CAPACITY_FACTOR * CTX_LEN // N_EXPERTS
    for i in range(DEPTH):
        lw = _layer_weights(i, w_in, w_gla_decay, b_gla_decay, w_uq, w_ukv, g_qr, g_kr, b_s)
        ml, mc = mod[i, :BATCH], mod[i, BATCH:BATCH + 1]
        h = _mod_norm(xc, row2(g_norm_mix[i]), ml, mc, 0, 1)
        z = _in_proj(h.reshape(BATCH * T_ALL, D_MODEL), lw['w_in']).reshape(BATCH, T_ALL, N_Z)
        ya, yb = _mix_ab(z, lw['wdec_f'], lw['bdec_f'], lw['wdec_b'], lw['bdec_b'], row2(g_gla_out[i]),
                         w_s[i].astype(BF16), lw['bs_full'])
        q, k, vt = _mla_prep(z, cs_tab, lw['wq'], lw['wkn'], lw['wvt'], row2(g_cq[i]), row2(g_ckv[i]),
                             row2(g_qn[i]), lw['gqr2'], row2(g_kn[i]), lw['gkr2'])
        yc = _attention(q, k, vt)
        w_r = jnp.concatenate([w_router[i], jnp.zeros((D_MODEL, LANE - N_EXPERTS), F32)], axis=1).astype(BF16)
        x1, h2a = _merge(ya, yb, yc, z, row2(b_gate[i]), w_proj_a[i].astype(BF16), w_proj_b[i].astype(BF16),
                         w_proj_c[i].astype(BF16), w_out[i].astype(BF16), xc, ml, mc, row2(g_norm_ffn[i]), w_r)
        caps = ((CTX_LEN, T_ALL, cap_lat),) + (((0, CTX_LEN, cap_ctx),) if i < DEPTH - 1 else ())
        idx = _route(h2a, caps)
        n_sel = idx.shape[-1]
        idx4 = idx.reshape(BATCH, N_EXPERTS, 1, n_sel)
        idx_rows = idx.reshape(BATCH, N_EXPERTS, n_sel) + (jnp.arange(BATCH, dtype=jnp.int32) * T_ALL)[:, None, None]
        idx_rows = jnp.swapaxes(idx_rows, 0, 1).reshape(N_EXPERTS, 1, BATCH * n_sel)
        ys = _moe_ffn(i, idx_rows, h2a.reshape(BATCH * T_ALL, D_MODEL + LANE), w_e_gate, w_e_up, w_e_down)
        xc = _combine(idx4, ys, x1, ml, mc, cap_lat)
    return xc
```

```python
import functools

import jax
import jax.numpy as jnp
import numpy as np
from jax import lax
from jax.experimental import pallas as pl
from jax.experimental.pallas import tpu as pltpu

F32 = jnp.float32
BF16 = jnp.bfloat16

D_MODEL = 1024
BATCH = 4
SEQ = 4096
DEPTH = 2
CTX_LEN = 256
T_ALL = CTX_LEN + SEQ
GRID_W = 64
EPS = 1e-6
N_MOD = 6

A_GROUPS = 8
A_CHUNK = 128

B_HEADS = 4
B_DK = 128
B_DV = 256
B_GATE_RANK = 16
B_GATE_NORM = 16.0
B_CHUNK = 64

C_HEADS = 8
C_Q_RANK = 384
C_KV_RANK = 256
C_NOPE = 128
C_ROPE = 64
C_DV = 128
ROPE_BASE = 10000.0
C_QK = 2 * C_NOPE

N_EXPERTS = 16
CAPACITY_FACTOR = 2
D_EXPERT = 2048

Z_BLK = 1024
ZB_GATES, ZB_U, ZB_VA, ZB_VB, ZB_RB, ZB_QK, ZB_SMALL = 0, 3, 4, 5, 6, 7, 8
N_Z = 9 * Z_BLK
SM_CQ, SM_CKV, SM_KR, SM_DEC = 0, C_Q_RANK, C_Q_RANK + C_KV_RANK, C_Q_RANK + C_KV_RANK + 2 * C_ROPE

LOG2_E = 1.4426950408889634
LANE = 128
VMEM_LIMIT = 56 * 1024 * 1024

TM_ROWS = 544
TM_MM = 2176
TM_PREP = 256
TQ = 256
HEADS_PER_STEP = 4
KEY_BLOCK = 1152
VT_ROWS = C_DV + 16
TB_GLA = 256
TW_MOE = 256


def _cparams(sem):
    return pltpu.CompilerParams(dimension_semantics=sem, vmem_limit_bytes=VMEM_LIMIT)


def _rms(x, n):
    return x * lax.rsqrt(jnp.sum(x * x, axis=-1, keepdims=True) * (1.0 / n) + EPS)


def _silu(x):
    return x * (1.0 / (1.0 + jnp.exp(-x)))


def _sigmoid(x):
    return 1.0 / (1.0 + jnp.exp(-x))


def _row_is_ctx(tile_idx, tm):
    row = tile_idx * tm + lax.broadcasted_iota(jnp.int32, (tm, 1), 0)
    return row < CTX_LEN


def _pick_mod(is_ctx, ml_ref, mc_ref, k):
    return jnp.where(is_ctx, mc_ref[0, k:k + 1, :], ml_ref[0, k:k + 1, :])


def _mod_kernel(c_ref, w_ref, b_ref, o_ref):
    s = _silu(c_ref[...])
    o_ref[0] = jnp.dot(s.astype(BF16), w_ref[0].astype(BF16), preferred_element_type=F32) + b_ref[0]


def _mod_table(c8, w_mod, b_mod):
    tn = 512
    n = N_MOD * D_MODEL
    return pl.pallas_call(
        _mod_kernel,
        grid=(DEPTH, n // tn),
        in_specs=[
            pl.BlockSpec((8, D_MODEL), lambda l, j: (0, 0)),
            pl.BlockSpec((1, D_MODEL, tn), lambda l, j: (l, 0, j)),
            pl.BlockSpec((1, 1, tn), lambda l, j: (l, 0, j)),
        ],
        out_specs=pl.BlockSpec((1, 8, tn), lambda l, j: (l, 0, j)),
        out_shape=jax.ShapeDtypeStruct((DEPTH, 8, n), F32),
        compiler_params=_cparams(("parallel", "parallel")),
        name="mod_table",
    )(c8, w_mod, b_mod.reshape(DEPTH, 1, n))


NORM_ROWS = 272


def _in_proj_kernel(x_ref, g_ref, ml_ref, mc_ref, w_ref, o_ref, h_ref, *, tm):
    @pl.when(pl.program_id(1) == 0)
    def _():
        tile_in_sample = pl.program_id(0) % (T_ALL // tm)

        def part(c, carry):
            r0 = pl.multiple_of(c * NORM_ROWS, 16)
            rows = pl.ds(r0, NORM_ROWS)
            row = tile_in_sample * tm + r0 + lax.broadcasted_iota(jnp.int32, (NORM_ROWS, 1), 0)
            is_ctx = row < CTX_LEN
            y = _rms(x_ref[0, rows, :], D_MODEL) * g_ref[...]
            y = y * (1.0 + _pick_mod(is_ctx, ml_ref, mc_ref, 1)) + _pick_mod(is_ctx, ml_ref, mc_ref, 0)
            h_ref[rows, :] = y.astype(BF16)
            return carry

        lax.fori_loop(0, tm // NORM_ROWS, part, 0)

    o_ref[0] = jnp.dot(h_ref[...], w_ref[...], preferred_element_type=F32).astype(BF16)


def _in_proj(x, g, ml, mc, w):
    tm, tn = TM_MM, Z_BLK
    per = T_ALL // tm
    n = w.shape[1]
    return pl.pallas_call(
        functools.partial(_in_proj_kernel, tm=tm),
        grid=(BATCH * per, n // tn),
        in_specs=[
            pl.BlockSpec((1, tm, D_MODEL), lambda i, j: (i // per, i % per, 0)),
            pl.BlockSpec((1, D_MODEL), lambda i, j: (0, 0)),
            pl.BlockSpec((1, N_MOD, D_MODEL), lambda i, j: (i // per, 0, 0)),
            pl.BlockSpec((1, N_MOD, D_MODEL), lambda i, j: (0, 0, 0)),
            pl.BlockSpec((D_MODEL, tn), lambda i, j: (0, j)),
        ],
        out_specs=pl.BlockSpec((1, tm, tn), lambda i, j: (i // per, i % per, j)),
        out_shape=jax.ShapeDtypeStruct((BATCH, T_ALL, n), BF16),
        scratch_shapes=[pltpu.VMEM((tm, D_MODEL), BF16)],
        compiler_params=_cparams(("parallel", "arbitrary")),
        name="in_proj",
    )(x, g, ml, mc, w)


def _mix_a_block(u_ref, v_ref, ws_ref, bs_ref, o_ref):
    for r0 in range(0, u_ref.shape[1], A_CHUNK):
        rows = slice(r0, r0 + A_CHUNK)
        for g in range(A_GROUPS):
            cols = slice(g * LANE, (g + 1) * LANE)
            v = v_ref[0, rows, cols].astype(F32)
            mu = jnp.mean(v, axis=-1, keepdims=True)
            d = v - mu
            ln = d * lax.rsqrt(jnp.mean(d * d, axis=-1, keepdims=True) + EPS)
            mixed = jnp.dot(ws_ref[g], ln.astype(BF16), preferred_element_type=F32) + bs_ref[:, cols]
            o_ref[0, rows, cols] = (u_ref[0, rows, cols].astype(F32) * mixed).astype(BF16)


_NT = (((1,), (1,)), ((), ()))
_TN = (((0,), (0,)), ((), ()))


def _log_sigmoid(x):
    return jnp.minimum(x, 0.0) - jnp.log(1.0 + jnp.exp(-jnp.abs(x)))


def _exact_tri_sum(tri, la):
    hi = la.astype(BF16)
    r1 = la - hi.astype(F32)
    mid = r1.astype(BF16)
    lo = (r1 - mid.astype(F32)).astype(BF16)
    return (jnp.dot(tri, hi, preferred_element_type=F32)
            + jnp.dot(tri, mid, preferred_element_type=F32)
            + jnp.dot(tri, lo, preferred_element_type=F32))


def _gla_block(qk_ref, v_ref, sm_ref, wdec_ref, bdec_ref, st_ref, *, reverse, emit):
    n_chunk = TB_GLA // B_CHUNK
    shift = B_CHUNK.bit_length() - 1
    ii = lax.broadcasted_iota(jnp.int32, (TB_GLA, TB_GLA), 0)
    jj = lax.broadcasted_iota(jnp.int32, (TB_GLA, TB_GLA), 1)
    keep = ((ii >> shift) == (jj >> shift)) & ((jj >= ii) if reverse else (jj <= ii))
    edge = 0 if reverse else B_CHUNK - 1

    pre = jnp.dot(sm_ref[0, :, SM_DEC:SM_DEC + LANE], wdec_ref[...], preferred_element_type=F32) + bdec_ref[...]
    la = _log_sigmoid(pre) * (1.0 / B_GATE_NORM)
    bcum = _exact_tri_sum(jnp.where(keep, 1.0, 0.0).astype(BF16), la)
    b_tot = [bcum[c * B_CHUNK + edge:c * B_CHUNK + edge + 1] for c in range(n_chunk)]
    blast = jnp.concatenate([jnp.broadcast_to(bt, (B_CHUNK, B_HEADS * B_DK)) for bt in b_tot], axis=0)
    e_fwd, e_inv, e_rest = jnp.exp(bcum), jnp.exp(-bcum), jnp.exp(blast - bcum)
    e_tot = [jnp.exp(bt) for bt in b_tot]
    for h in range(B_HEADS):
        kc = slice(h * B_DK, (h + 1) * B_DK)
        vc = slice(h * B_DV, (h + 1) * B_DV)
        q = qk_ref[0, :, kc].astype(F32)
        k = qk_ref[0, :, B_HEADS * B_DK + h * B_DK:B_HEADS * B_DK + (h + 1) * B_DK].astype(F32)
        v = v_ref[0, :, vc]
        qd = ((q * B_DK ** -0.5) * e_fwd[:, kc]).astype(BF16)
        kinv = (k * e_inv[:, kc]).astype(BF16)
        kdec = (k * e_rest[:, kc]).astype(BF16)
        s = lax.dot_general(qd, kinv, _NT, preferred_element_type=F32)
        o_intra = jnp.dot(jnp.where(keep, s, 0.0).astype(BF16), v, preferred_element_type=F32)
        st = st_ref[h]
        for c in (range(n_chunk - 1, -1, -1) if reverse else range(n_chunk)):
            rows = slice(c * B_CHUNK, (c + 1) * B_CHUNK)
            emit(h, rows, o_intra[rows] + lax.dot_general(qd[rows], st.astype(BF16), _NT, preferred_element_type=F32))
            kv_t = lax.dot_general(v[rows], kdec[rows], _TN, preferred_element_type=F32)
            st = e_tot[c][:, kc] * st + kv_t
        st_ref[h] = st


def _gla_fwd_kernel(qk_ref, v_ref, sm_ref, wdec_ref, bdec_ref, u_ref, va_ref, ws_ref, bs_ref, o_ref, ya_ref, st_ref):
    @pl.when(pl.program_id(1) == 0)
    def _():
        st_ref[...] = jnp.zeros_like(st_ref)

    def emit(h, rows, o):
        o_ref[0, rows, h * B_DV:(h + 1) * B_DV] = o

    _mix_a_block(u_ref, va_ref, ws_ref, bs_ref, ya_ref)
    _gla_block(qk_ref, v_ref, sm_ref, wdec_ref, bdec_ref, st_ref, reverse=False, emit=emit)


def _gla_bwd_kernel(qk_ref, v_ref, sm_ref, wdec_ref, bdec_ref, of_ref, r_ref, g_ref, y_ref, st_ref):
    @pl.when(pl.program_id(1) == 0)
    def _():
        st_ref[...] = jnp.zeros_like(st_ref)

    def emit(h, rows, o):
        vc = slice(h * B_DV, (h + 1) * B_DV)
        o = o + of_ref[0, rows, vc]
        on = _rms(o, B_DV) * g_ref[...]
        y_ref[0, rows, vc] = (on * _silu(r_ref[0, rows, vc].astype(F32))).astype(BF16)

    _gla_block(qk_ref, v_ref, sm_ref, wdec_ref, bdec_ref, st_ref, reverse=True, emit=emit)


def _bwd_block(i):
    nb = T_ALL // TB_GLA
    return jnp.where(i == 0, 0, nb - i)


def _mix_ab(z, wdec_f, bdec_f, wdec_b, bdec_b, g_out, ws, bs_full):
    tb = TB_GLA
    nb = T_ALL // tb
    wide = B_HEADS * B_DV

    def zspec(blk, imap):
        return pl.BlockSpec((1, tb, Z_BLK), lambda b, i: (b, imap(i), blk))

    def const2(shape):
        return pl.BlockSpec(shape, lambda b, i: (0, 0))

    state = pltpu.VMEM((B_HEADS, B_DV, B_DK), F32)
    fwd = lambda i: i
    o_f, ya = pl.pallas_call(
        _gla_fwd_kernel,
        grid=(BATCH, nb),
        in_specs=[zspec(ZB_QK, fwd), zspec(ZB_VB, fwd), zspec(ZB_SMALL, fwd),
                  const2((LANE, B_HEADS * B_DK)), const2((1, B_HEADS * B_DK)),
                  zspec(ZB_U, fwd), zspec(ZB_VA, fwd),
                  pl.BlockSpec((A_GROUPS, A_CHUNK, A_CHUNK), lambda b, i: (0, 0, 0)), const2((A_CHUNK, D_MODEL))],
        out_specs=[pl.BlockSpec((1, tb, wide), lambda b, i: (b, i, 0)),
                   pl.BlockSpec((1, tb, D_MODEL), lambda b, i: (b, i, 0))],
        out_shape=[jax.ShapeDtypeStruct((BATCH, T_ALL, wide), F32),
                   jax.ShapeDtypeStruct((BATCH, T_ALL, D_MODEL), BF16)],
        scratch_shapes=[state],
        compiler_params=_cparams(("parallel", "arbitrary")),
        name="gla_fwd",
    )(z, z, z, wdec_f, bdec_f, z, z, ws, bs_full)
    return ya, pl.pallas_call(
        _gla_bwd_kernel,
        grid=(BATCH, nb),
        in_specs=[zspec(ZB_QK, _bwd_block), zspec(ZB_VB, _bwd_block), zspec(ZB_SMALL, _bwd_block),
                  const2((LANE, B_HEADS * B_DK)), const2((1, B_HEADS * B_DK)),
                  pl.BlockSpec((1, tb, wide), lambda b, i: (b, _bwd_block(i), 0)),
                  zspec(ZB_RB, _bwd_block), const2((1, B_DV))],
        out_specs=pl.BlockSpec((1, tb, wide), lambda b, i: (b, _bwd_block(i), 0)),
        out_shape=jax.ShapeDtypeStruct((BATCH, T_ALL, wide), BF16),
        scratch_shapes=[state],
        compiler_params=_cparams(("parallel", "arbitrary")),
        name="gla_bwd",
    )(z, z, z, wdec_b, bdec_b, o_f, z, g_out)


def _mla_prep_kernel(sm_ref, cs_ref, wq_ref, wkn_ref, wvt_ref, gcq_ref, gckv_ref, gqn_ref, gqr_ref, gkn_ref, gkr_ref,
                     q_ref, k_ref, vt_ref):
    scale = (C_NOPE + C_ROPE) ** -0.5 * LOG2_E
    vt_row = lax.broadcasted_iota(jnp.int32, (VT_ROWS, 1), 0)
    cs = cs_ref[...]
    lane = lax.broadcasted_iota(jnp.int32, (1, LANE), 1)
    low_half = (lane < C_ROPE).astype(F32)

    def rope(pack, gain):
        t = _rms(pack, LANE) * gain * cs
        return t + pltpu.roll(t, C_ROPE, axis=1)

    cq = sm_ref[0, :, SM_CQ:SM_CQ + C_Q_RANK].astype(F32)
    cqn = (_rms(cq, C_Q_RANK) * gcq_ref[...]).astype(BF16)
    q_all = jnp.dot(cqn, wq_ref[...], preferred_element_type=F32)
    ckv = sm_ref[0, :, SM_CKV:SM_CKV + C_KV_RANK].astype(F32)
    ckvn = (_rms(ckv, C_KV_RANK) * gckv_ref[...]).astype(BF16)
    kn_all = jnp.dot(ckvn, wkn_ref[...], preferred_element_type=F32)
    k_rot = rope(sm_ref[0, :, SM_KR:SM_KR + LANE].astype(F32), gkr_ref[...]).astype(BF16)
    for h in range(C_HEADS):
        c0 = h * C_QK
        qn = _rms(q_all[:, c0:c0 + C_NOPE], C_NOPE) * gqn_ref[...]
        q_ref[0, h, :, 0:C_NOPE] = (qn * scale).astype(BF16)
        q_rot = rope(q_all[:, c0 + C_NOPE:c0 + C_QK], gqr_ref[...]) * low_half
        q_ref[0, h, :, C_NOPE:C_QK] = (q_rot * scale).astype(BF16)
        kn = _rms(kn_all[:, h * C_NOPE:(h + 1) * C_NOPE], C_NOPE) * gkn_ref[...]
        k_ref[0, h, :, 0:C_NOPE] = kn.astype(BF16)
        k_ref[0, h, :, C_NOPE:C_QK] = k_rot
        vt = lax.dot_general(wvt_ref[h], ckvn, _NT, preferred_element_type=F32)
        vt_ref[0, h] = jnp.where(vt_row == C_DV, 1.0, vt).astype(BF16)


def _mla_prep(z, cs_tab, wq, wkn, wvt, gcq, gckv, gqn, gqr2, gkn, gkr2):
    tm = TM_PREP

    def const2(shape):
        return pl.BlockSpec(shape, lambda b, i: (0, 0))

    hspec = lambda w: pl.BlockSpec((1, C_HEADS, tm, w), lambda b, i: (b, 0, i, 0))
    hshape = lambda w: jax.ShapeDtypeStruct((BATCH, C_HEADS, T_ALL, w), BF16)
    return pl.pallas_call(
        _mla_prep_kernel,
        grid=(BATCH, T_ALL // tm),
        in_specs=[
            pl.BlockSpec((1, tm, Z_BLK), lambda b, i: (b, i, ZB_SMALL)),
            pl.BlockSpec((tm, LANE), lambda b, i: (i, 0)),
            const2((C_Q_RANK, C_HEADS * C_QK)), const2((C_KV_RANK, C_HEADS * C_NOPE)),
            pl.BlockSpec((C_HEADS, VT_ROWS, C_KV_RANK), lambda b, i: (0, 0, 0)),
            const2((1, C_Q_RANK)), const2((1, C_KV_RANK)),
            const2((1, C_NOPE)), const2((1, LANE)), const2((1, C_NOPE)), const2((1, LANE)),
        ],
        out_specs=[hspec(C_QK), hspec(C_QK),
                   pl.BlockSpec((1, C_HEADS, VT_ROWS, tm), lambda b, i: (b, 0, 0, i))],
        out_shape=[hshape(C_QK), hshape(C_QK),
                   jax.ShapeDtypeStruct((BATCH, C_HEADS, VT_ROWS, T_ALL), BF16)],
        compiler_params=_cparams(("parallel", "parallel")),
        name="mla_prep",
    )(z, cs_tab, wq, wkn, wvt, gcq, gckv, gqn, gqr2, gkn, gkr2)


def _attn_kernel(q_ref, k_ref, vt_ref, o_ref):
    def max_over_keys(s):
        parts = [s[i * 8:(i + 1) * 8] for i in range(s.shape[0] // 8)]
        while len(parts) > 1:
            parts = [jnp.maximum(parts[i], parts[i + 1]) for i in range(0, len(parts) - 1, 2)] + parts[len(parts) & ~1:]
        return jnp.max(parts[0], axis=0, keepdims=True)

    def attend(n_keys):
        edges = list(range(0, n_keys, KEY_BLOCK)) + [n_keys]
        blocks = [slice(a, b) for a, b in zip(edges[:-1], edges[1:])]
        heads = range(HEADS_PER_STEP)
        s = [[lax.dot_general(k_ref[0, h, kb, :], q_ref[0, h], _NT, preferred_element_type=F32) for kb in blocks]
             for h in heads]
        for h in heads:
            m = functools.reduce(jnp.maximum, [max_over_keys(sb) for sb in s[h]])
            acc = None
            for kb, sb in zip(blocks, s[h]):
                p = jnp.exp2(sb - m).astype(BF16)
                part = jnp.dot(vt_ref[0, h, :, kb], p, preferred_element_type=F32)
                acc = part if acc is None else acc + part
            o = acc[0:C_DV] * (1.0 / acc[C_DV:C_DV + 1])
            o_ref[0, :, h * C_DV:(h + 1) * C_DV] = o.T.astype(BF16)

    @pl.when(pl.program_id(2) == 0)
    def _():
        attend(CTX_LEN)

    @pl.when(pl.program_id(2) > 0)
    def _():
        attend(T_ALL)


def _attention(q, k, vt):
    tq = TQ
    hs = HEADS_PER_STEP
    return pl.pallas_call(
        _attn_kernel,
        grid=(BATCH, C_HEADS // hs, T_ALL // tq),
        in_specs=[
            pl.BlockSpec((1, hs, tq, C_QK), lambda b, h, i: (b, h, i, 0)),
            pl.BlockSpec((1, hs, T_ALL, C_QK), lambda b, h, i: (b, h, 0, 0)),
            pl.BlockSpec((1, hs, VT_ROWS, T_ALL), lambda b, h, i: (b, h, 0, 0)),
        ],
        out_specs=pl.BlockSpec((1, tq, hs * C_DV), lambda b, h, i: (b, i, h)),
        out_shape=jax.ShapeDtypeStruct((BATCH, T_ALL, C_HEADS * C_DV), BF16),
        compiler_params=_cparams(("parallel", "parallel", "parallel")),
        name="mla_attention",
    )(q, k, vt)


def _merge_kernel(ya_ref, yb_ref, yc_ref, gt_ref, bg_ref, wa_ref, wb_ref, wc_ref, wo_ref, x_ref, ml_ref, mc_ref,
                  gn_ref, wr_ref, x1_ref, h2a_ref, *, tm):
    is_ctx = _row_is_ctx(pl.program_id(1), tm)
    merged = None
    for j, (y_ref, w_ref) in enumerate(((ya_ref, wa_ref), (yb_ref, wb_ref), (yc_ref, wc_ref))):
        cols = slice(j * D_MODEL, (j + 1) * D_MODEL)
        gate = _sigmoid(gt_ref[0, :, cols].astype(F32) + bg_ref[:, cols])
        term = gate * jnp.dot(y_ref[0], w_ref[...], preferred_element_type=F32)
        merged = term if merged is None else merged + term
    y = jnp.dot(merged.astype(BF16), wo_ref[...], preferred_element_type=F32)
    x1 = x_ref[0] + _pick_mod(is_ctx, ml_ref, mc_ref, 2) * y
    x1_ref[0] = x1
    h2 = _rms(x1, D_MODEL) * gn_ref[...]
    h2 = h2 * (1.0 + _pick_mod(is_ctx, ml_ref, mc_ref, 4)) + _pick_mod(is_ctx, ml_ref, mc_ref, 3)
    logits = jnp.dot(h2.astype(BF16), wr_ref[...], preferred_element_type=F32)
    lane = lax.broadcasted_iota(jnp.int32, (1, LANE), 1)
    logits = jnp.where(lane < N_EXPERTS, logits, -jnp.inf)
    e = jnp.exp(logits - jnp.max(logits, axis=-1, keepdims=True))
    h2a_ref[0, :, 0:D_MODEL] = h2
    h2a_ref[0, :, D_MODEL:D_MODEL + LANE] = e / jnp.sum(e, axis=-1, keepdims=True)


def _merge(ya, yb, yc, z, b_gate, wa, wb, wc, wo, x, ml, mc, g_ffn, w_router):
    tm = TM_ROWS
    row = lambda w: pl.BlockSpec((1, tm, w), lambda b, i: (b, i, 0))

    def const2(shape):
        return pl.BlockSpec(shape, lambda b, i: (0, 0))

    sq = const2((D_MODEL, D_MODEL))
    return pl.pallas_call(
        functools.partial(_merge_kernel, tm=tm),
        grid=(BATCH, T_ALL // tm),
        in_specs=[
            row(D_MODEL), row(D_MODEL), row(D_MODEL), row(3 * D_MODEL), const2((1, 3 * D_MODEL)),
            sq, sq, sq, sq, row(D_MODEL),
            pl.BlockSpec((1, N_MOD, D_MODEL), lambda b, i: (b, 0, 0)),
            pl.BlockSpec((1, N_MOD, D_MODEL), lambda b, i: (0, 0, 0)),
            const2((1, D_MODEL)), const2((D_MODEL, LANE)),
        ],
        out_specs=[row(D_MODEL), row(D_MODEL + LANE)],
        out_shape=[jax.ShapeDtypeStruct((BATCH, T_ALL, D_MODEL), F32),
                   jax.ShapeDtypeStruct((BATCH, T_ALL, D_MODEL + LANE), F32)],
        compiler_params=_cparams(("parallel", "parallel")),
        name="merge",
    )(ya, yb, yc, z, b_gate, wa, wb, wc, wo, x, ml, mc, g_ffn, w_router)


N_ROUTE_ROWS = BATCH * N_EXPERTS
N_BISECT = 150


def _count(mask):
    return jnp.sum(jnp.where(mask, 1.0, 0.0), axis=1, keepdims=True)


def _lane_cumsum_blocks(x01):
    r, n = x01.shape
    nblk = n // LANE
    x3 = jnp.concatenate([x01[:, j * LANE:(j + 1) * LANE] for j in range(nblk)], axis=0).astype(BF16)
    ii = lax.broadcasted_iota(jnp.int32, (LANE, LANE), 0)
    jj = lax.broadcasted_iota(jnp.int32, (LANE, LANE), 1)
    inc = jnp.dot(x3, jnp.where(ii <= jj, 1.0, 0.0).astype(BF16), preferred_element_type=F32)
    off = jnp.zeros((r, 1), F32)
    out = []
    for j in range(nblk):
        blk = inc[j * r:(j + 1) * r]
        out.append(blk + off)
        off = off + blk[:, LANE - 1:LANE]
    return out


def _select_top(a, cap):
    def more(c):
        return (c[2] > 0.0) & (c[3] < N_BISECT)

    def bisect(c):
        thr, step, _, i = c
        cand = thr + step
        thr = jnp.where(_count(a >= cand) >= cap, cand, thr)
        step = step * 0.5
        live = jnp.max(jnp.where(thr + step != thr, 1.0, 0.0))
        return thr, step, live, i + 1

    thr = lax.while_loop(more, bisect, (jnp.zeros((a.shape[0], 1), F32), jnp.ones((a.shape[0], 1), F32),
                                        jnp.float32(1.0), jnp.int32(0)))[0]
    gt = a > thr
    eq = a == thr
    need = cap - _count(gt)
    eq_rank = jnp.concatenate(_lane_cumsum_blocks(jnp.where(eq, 1.0, 0.0)), axis=1)
    sel = gt | (eq & (eq_rank <= need))
    return _lane_cumsum_blocks(jnp.where(sel, 1.0, 0.0))


def _route_kernel(aff_ref, idx_ref, cum_ref, end_ref, *, caps):
    aff_t = jnp.concatenate([aff_ref[b].T[:N_EXPERTS] for b in range(BATCH)], axis=0)
    lane = lax.broadcasted_iota(jnp.int32, (1, LANE), 1)
    p0 = 0
    for (t_lo, t_hi, cap) in caps:
        nblk = (t_hi - t_lo) // LANE
        blocks = _select_top(aff_t[:, t_lo:t_hi], cap)
        ends = jnp.full((N_ROUTE_ROWS, LANE), float(cap), F32)
        for j, blk in enumerate(blocks):
            cum_ref[:, j, :] = blk
            ends = jnp.where(lane == j, blk[:, LANE - 1:LANE], ends)
        end_ref[...] = ends
        p_col = lax.broadcasted_iota(jnp.int32, (cap, 1), 0).astype(F32)

        def per_row(r, res):
            n_full = _count(end_ref[pl.ds(r, 1), :] <= p_col)
            cum = cum_ref[r][0:nblk]
            cum = jnp.concatenate([cum, jnp.zeros((LANE - nblk, LANE), F32)], axis=0)
            hi = jnp.floor(cum * (1.0 / 64.0))
            lo = cum - 64.0 * hi
            pick = jnp.where(lane.astype(F32) == n_full, 1.0, 0.0).astype(BF16)
            part = (64.0 * jnp.dot(pick, hi.astype(BF16), preferred_element_type=F32)
                    + jnp.dot(pick, lo.astype(BF16), preferred_element_type=F32))
            cnt = LANE * n_full + _count(part <= p_col)
            return jnp.where(lane == r, cnt, res)

        res = lax.fori_loop(0, N_ROUTE_ROWS, per_row, jnp.zeros((cap, LANE), F32))
        if cap % LANE:
            res = jnp.concatenate([res, jnp.zeros((LANE - cap % LANE, LANE), F32)], axis=0)
        idx_ref[:, p0:p0 + cap] = (res.T[:N_ROUTE_ROWS, :cap] + float(t_lo)).astype(jnp.int32)
        p0 += cap


def _route(h2a, caps):
    n_sel = sum(c[2] for c in caps)
    return pl.pallas_call(
        functools.partial(_route_kernel, caps=caps),
        grid=(1,),
        in_specs=[pl.BlockSpec((BATCH, T_ALL, LANE), lambda i: (0, 0, D_MODEL // LANE))],
        out_specs=pl.BlockSpec((N_ROUTE_ROWS, n_sel), lambda i: (0, 0)),
        out_shape=jax.ShapeDtypeStruct((N_ROUTE_ROWS, n_sel), jnp.int32),
        scratch_shapes=[pltpu.VMEM((N_ROUTE_ROWS, SEQ // LANE, LANE), F32), pltpu.VMEM((N_ROUTE_ROWS, LANE), F32)],
        compiler_params=_cparams(("arbitrary",)),
        name="moe_route",
    )(h2a)


N_HID_TILES = D_EXPERT // TW_MOE
N_OUT_TILES = D_MODEL // TW_MOE


N_MOE_STEPS = N_HID_TILES + N_OUT_TILES


def _moe_kernel(idx_cur_ref, idx_nxt_ref, h2a_hbm, wg_ref, wu_ref, wd_ref, o_ref,
                stage_ref, x_ref, g_ref, hid_ref, sem, *, n_rows, rows_per_step):
    e = pl.program_id(0)
    j = pl.program_id(1)
    n_copies = n_rows

    def row_copy(idx_ref, r):
        return pltpu.make_async_copy(h2a_hbm.at[pl.ds(idx_ref[0, 0, r], 1), :], stage_ref.at[pl.ds(r, 1), :], sem)

    def issue_ahead():
        for k in range(rows_per_step):
            row_copy(idx_nxt_ref, j * rows_per_step + k).start()

    def wait_stage():
        def one(r, carry):
            row_copy(idx_cur_ref, r).wait()
            return carry

        lax.fori_loop(0, n_copies, one, 0, unroll=8)

    @pl.when((e == 0) & (j == 0))
    def _():
        def one(r, carry):
            row_copy(idx_cur_ref, r).start()
            return carry

        lax.fori_loop(0, n_copies, one, 0, unroll=8)

    @pl.when(j == 0)
    def _():
        wait_stage()
        x_ref[...] = stage_ref[:, 0:D_MODEL].astype(BF16)
        lane = lax.broadcasted_iota(jnp.int32, (1, LANE), 1)
        g_ref[...] = jnp.sum(jnp.where(lane == e, stage_ref[:, D_MODEL:D_MODEL + LANE], 0.0), axis=1, keepdims=True)

    @pl.when(j < N_HID_TILES)
    def _():
        issue_ahead()
        x = x_ref[...]
        h = (_silu(jnp.dot(x, wg_ref[0, 0].astype(BF16), preferred_element_type=F32))
             * jnp.dot(x, wu_ref[0, 0].astype(BF16), preferred_element_type=F32))
        hid_ref[jnp.minimum(j, N_HID_TILES - 1)] = h.astype(BF16)

    @pl.when(j >= N_HID_TILES)
    def _():
        hid = jnp.concatenate([hid_ref[f] for f in range(N_HID_TILES)], axis=1)
        o_ref[0] = jnp.dot(hid, wd_ref[0, 0].astype(BF16), preferred_element_type=F32) * g_ref[...]

    @pl.when((e == N_EXPERTS - 1) & (j == N_MOE_STEPS - 1))
    def _():
        wait_stage()


def _moe_ffn(layer, idx_rows, h2a_rows, wg, wu, wd):
    n_rows = idx_rows.shape[-1]
    assert n_rows % N_HID_TILES == 0
    rows_per_step = n_rows // N_HID_TILES
    hid_tile = lambda e, j: (layer, e, 0, jnp.minimum(j, N_HID_TILES - 1))
    out_tile = lambda j: jnp.maximum(j - N_HID_TILES, 0)
    idx_spec = lambda step: pl.BlockSpec((1, 1, n_rows), lambda e, j: (jnp.minimum(e + step, N_EXPERTS - 1), 0, 0),
                                         memory_space=pltpu.SMEM)
    return pl.pallas_call(
        functools.partial(_moe_kernel, n_rows=n_rows, rows_per_step=rows_per_step),
        grid=(N_EXPERTS, N_MOE_STEPS),
        in_specs=[
            idx_spec(0), idx_spec(1),
            pl.BlockSpec(memory_space=pl.ANY),
            pl.BlockSpec((1, 1, D_MODEL, TW_MOE), hid_tile),
            pl.BlockSpec((1, 1, D_MODEL, TW_MOE), hid_tile),
            pl.BlockSpec((1, 1, D_EXPERT, TW_MOE), lambda e, j: (layer, e, 0, out_tile(j))),
        ],
        out_specs=pl.BlockSpec((1, n_rows, TW_MOE), lambda e, j: (e, 0, out_tile(j))),
        out_shape=jax.ShapeDtypeStruct((N_EXPERTS, n_rows, D_MODEL), F32),
        scratch_shapes=[pltpu.VMEM((n_rows, D_MODEL + LANE), F32), pltpu.VMEM((n_rows, D_MODEL), BF16),
                        pltpu.VMEM((n_rows, 1), F32), pltpu.VMEM((N_HID_TILES, n_rows, TW_MOE), BF16),
                        pltpu.SemaphoreType.DMA(())],
        compiler_params=_cparams(("arbitrary", "arbitrary")),
        name="moe_ffn",
    )(idx_rows, idx_rows, h2a_rows, wg, wu, wd)


ROW_GROUP = 8


def _combine_kernel(idx_ref, ys_ref, x1_hbm, ml_ref, mc_ref, o_ref, sem, *, n_lat, n_sel, t0):
    b = pl.program_id(0)

    @pl.when(pl.program_id(1) == 0)
    def _():
        seed = pltpu.make_async_copy(x1_hbm.at[b, pl.ds(t0, T_ALL - t0), :], o_ref.at[0], sem)
        seed.start()
        seed.wait()

    def add_rows(p_lo, p_hi, gate):
        for p0 in range(p_lo, p_hi, ROW_GROUP):
            toks = [idx_ref[0, 0, 0, p0 + r] - t0 for r in range(ROW_GROUP)]
            new = [o_ref[0, pl.ds(toks[r], 1), :] + gate * ys_ref[0, p0 + r:p0 + r + 1, :] for r in range(ROW_GROUP)]
            for r in range(ROW_GROUP):
                o_ref[0, pl.ds(toks[r], 1), :] = new[r]

    add_rows(0, n_lat, ml_ref[0, 5:6, :])
    if n_sel > n_lat:
        add_rows(n_lat, n_sel, mc_ref[0, 5:6, :])


def _combine(idx4, ys, x1, ml, mc, n_lat):
    n_sel = idx4.shape[-1]
    assert n_lat % ROW_GROUP == 0 and (n_sel - n_lat) % ROW_GROUP == 0
    t0 = 0 if n_sel > n_lat else CTX_LEN
    return pl.pallas_call(
        functools.partial(_combine_kernel, n_lat=n_lat, n_sel=n_sel, t0=t0),
        grid=(BATCH, N_EXPERTS),
        in_specs=[pl.BlockSpec((1, 1, 1, n_sel), lambda b, e: (b, e, 0, 0), memory_space=pltpu.SMEM),
                  pl.BlockSpec((1, n_sel, D_MODEL), lambda b, e: (e, b, 0)),
                  pl.BlockSpec(memory_space=pl.ANY),
                  pl.BlockSpec((1, N_MOD, D_MODEL), lambda b, e: (b, 0, 0)),
                  pl.BlockSpec((1, N_MOD, D_MODEL), lambda b, e: (0, 0, 0))],
        out_specs=pl.BlockSpec((1, T_ALL - t0, D_MODEL), lambda b, e: (b, 0, 0)),
        out_shape=jax.ShapeDtypeStruct((BATCH, T_ALL - t0, D_MODEL), F32),
        scratch_shapes=[pltpu.SemaphoreType.DMA(())],
        compiler_params=_cparams(("arbitrary", "arbitrary")),
        name="moe_combine",
    )(idx4, ys, x1, ml, mc)


def _rope_swap_perm():
    idx = np.arange(C_ROPE, dtype=np.int32).reshape(2, 2, C_ROPE // 4)
    return idx[:, ::-1, :].reshape(-1)


def _rope_table():
    rows = SEQ // GRID_W
    row = jnp.broadcast_to(jnp.arange(rows, dtype=F32)[:, None], (rows, GRID_W)).reshape(-1)
    col = jnp.broadcast_to(jnp.arange(GRID_W, dtype=F32)[None, :], (rows, GRID_W)).reshape(-1)
    axis_dim = C_ROPE // 2
    inv_freq = ROPE_BASE ** (-jnp.arange(0, axis_dim, 2, dtype=F32) / axis_dim)
    n_freq = C_ROPE // 4
    ang = jnp.stack([row[:, None] * inv_freq, col[:, None] * inv_freq], axis=1)
    cos = jnp.broadcast_to(jnp.cos(ang)[:, :, None, :], (SEQ, 2, 2, n_freq)).reshape(SEQ, C_ROPE)
    sin = jnp.sin(ang)
    sin = jnp.stack([-sin, sin], axis=2).reshape(SEQ, C_ROPE)
    lat = jnp.concatenate([cos, sin], axis=1)
    ctx = jnp.concatenate([jnp.ones((CTX_LEN, C_ROPE), F32), jnp.zeros((CTX_LEN, C_ROPE), F32)], axis=1)
    return jnp.concatenate([ctx, lat], axis=0)


def _layer_weights(i, w_in, w_gla_decay, b_gla_decay, w_uq, w_ukv, g_qr, g_kr, b_s):
    perm = _rope_swap_perm()
    o = [int(v) for v in np.cumsum((0, 1024, 1024, 512, 512, 1024, 1024, 32, C_Q_RANK, C_KV_RANK, C_ROPE, 3 * D_MODEL))]
    w = w_in[i]
    seg = lambda j: w[:, o[j]:o[j + 1]]
    u, va, qb, kb, vb, rb, dec, cq, ckv, kr, gates = (seg(j) for j in range(11))
    pad = jnp.zeros((D_MODEL, Z_BLK - (SM_DEC + 2 * B_GATE_RANK)), F32)
    w_in_ext = jnp.concatenate([gates, u, va, vb, rb, qb, kb, cq, ckv, kr, kr[:, perm], dec, pad], axis=1).astype(BF16)

    def dec_weight(j):
        wd = jnp.zeros((LANE, B_HEADS * B_DK), F32)
        return wd.at[j * B_GATE_RANK:(j + 1) * B_GATE_RANK].set(w_gla_decay[i, j]).astype(BF16)

    wq = w_uq[i].reshape(C_Q_RANK, C_HEADS, C_NOPE + C_ROPE)
    wq_ext = jnp.concatenate([wq, wq[:, :, C_NOPE:][:, :, perm]], axis=2).reshape(C_Q_RANK, C_HEADS * C_QK).astype(BF16)
    pack = lambda g: jnp.concatenate([g[i], g[i][perm]])[None, :]
    bs_full = jnp.repeat(b_s[i].T, A_CHUNK, axis=1)
    wkv = w_ukv[i].reshape(C_KV_RANK, C_HEADS, C_NOPE + C_DV)
    wkn = wkv[:, :, :C_NOPE].reshape(C_KV_RANK, C_HEADS * C_NOPE).astype(BF16)
    wvt = jnp.transpose(wkv[:, :, C_NOPE:], (1, 2, 0))
    wvt = jnp.concatenate([wvt, jnp.zeros((C_HEADS, VT_ROWS - C_DV, C_KV_RANK), F32)], axis=1).astype(BF16)
    return dict(w_in=w_in_ext, wdec_f=dec_weight(0), wdec_b=dec_weight(1),
                bdec_f=b_gla_decay[i, 0][None, :], bdec_b=b_gla_decay[i, 1][None, :],
                wq=wq_ext, wkn=wkn, wvt=wvt, gqr2=pack(g_qr), gkr2=pack(g_kr), bs_full=bs_full)


def kernel(x, c, ctx, c_ctx, w_mod, b_mod, g_norm_mix, g_norm_ffn, w_in, b_gate, w_s, b_s, w_gla_decay, b_gla_decay,
           g_gla_out, g_cq, g_ckv, w_uq, w_ukv, g_qn, g_qr, g_kn, g_kr, w_proj_a, w_proj_b, w_proj_c, w_out,
           w_router, w_e_gate, w_e_up, w_e_down):
    c8 = jnp.concatenate([c, c_ctx[None, :], jnp.zeros((8 - BATCH - 1, D_MODEL), F32)], axis=0)
    mod = _mod_table(c8, w_mod, b_mod).reshape(DEPTH, 8, N_MOD, D_MODEL)
    cs_tab = _rope_table()
    xc = jnp.concatenate([ctx, x], axis=1)
    row2 = lambda a: a[None, :]
    cap_lat = CAPACITY_FACTOR * SEQ // N_EXPERTS
    cap_ctx = CAPACITY_FACTOR * CTX_LEN // N_EXPERTS
    for i in range(DEPTH):
        lw = _layer_weights(i, w_in, w_gla_decay, b_gla_decay, w_uq, w_ukv, g_qr, g_kr, b_s)
        ml, mc = mod[i, :BATCH], mod[i, BATCH:BATCH + 1]
        z = _in_proj(xc, row2(g_norm_mix[i]), ml, mc, lw['w_in'])
        ya, yb = _mix_ab(z, lw['wdec_f'], lw['bdec_f'], lw['wdec_b'], lw['bdec_b'], row2(g_gla_out[i]),
                         w_s[i].astype(BF16), lw['bs_full'])
        q, k, vt = _mla_prep(z, cs_tab, lw['wq'], lw['wkn'], lw['wvt'], row2(g_cq[i]), row2(g_ckv[i]),
                             row2(g_qn[i]), lw['gqr2'], row2(g_kn[i]), lw['gkr2'])
        yc = _attention(q, k, vt)
        w_r = jnp.concatenate([w_router[i], jnp.zeros((D_MODEL, LANE - N_EXPERTS), F32)], axis=1).astype(BF16)
        x1, h2a = _merge(ya, yb, yc, z, row2(b_gate[i]), w_proj_a[i].astype(BF16), w_proj_b[i].astype(BF16),
                         w_proj_c[i].astype(BF16), w_out[i].astype(BF16), xc, ml, mc, row2(g_norm_ffn[i]), w_r)
        caps = ((CTX_LEN, T_ALL, cap_lat),) + (((0, CTX_LEN, cap_ctx),) if i < DEPTH - 1 else ())
        idx = _route(h2a, caps)
        n_sel = idx.shape[-1]
        idx4 = idx.reshape(BATCH, N_EXPERTS, 1, n_sel)
        idx_rows = idx.reshape(BATCH, N_EXPERTS, n_sel) + (jnp.arange(BATCH, dtype=jnp.int32) * T_ALL)[:, None, None]
        idx_rows = jnp.swapaxes(idx_rows, 0, 1).reshape(N_EXPERTS, 1, BATCH * n_sel)
        ys = _moe_ffn(i, idx_rows, h2a.reshape(BATCH * T_ALL, D_MODEL + LANE), w_e_gate, w_e_up, w_e_down)
        xc = _combine(idx4, ys, x1, ml, mc, cap_lat)
    return xc
```
